```python
import math
import jax, jax.numpy as jnp
from jax import lax
import numpy as np

D_MODEL = 1024
BATCH = 32
SEQ = 256
DEPTH = 4
DEC_BATCH = 2
DEC_SEQ = 4096
PAST_LEN = 256

GRID_W = 64
N_EVEN = (DEPTH + 1) // 2
N_ODD = DEPTH // 2
ROPE_THETA = 10000.0
Q_BLOCK = 128
EPS = 1e-6
A_HEADS = 8
A_NOPE = 64
A_ROPE = 32
A_QK = A_NOPE + A_ROPE
A_V = 64
A_Q_LORA = 384
A_KV_LORA = 256
B_HEADS = 8
B_KV_HEADS = 2
B_GROUP = B_HEADS // B_KV_HEADS
B_HD = 64
EVEN_SPLITS = [A_Q_LORA, A_Q_LORA + A_KV_LORA, A_Q_LORA + A_KV_LORA + A_ROPE,
               A_Q_LORA + A_KV_LORA + A_ROPE + B_HEADS * B_HD,
               A_Q_LORA + A_KV_LORA + A_ROPE + B_HEADS * B_HD + B_KV_HEADS * B_HD]
EVEN_IN = EVEN_SPLITS[-1] + B_KV_HEADS * B_HD
EVEN_MIX = A_HEADS * A_V + B_HEADS * B_HD
C_HEADS = 16
C_HD = 64
C_GROUPS = 2
C_HPG = C_HEADS // C_GROUPS
C_STATE = 128
C_INNER = C_HEADS * C_HD
C_CONV_K = 3
C_CONV_DIM = C_INNER + 2 * C_GROUPS * C_STATE
SSD_CHUNK = 128
D_GROUP_SIZE = 16
D_GROUPS = 32
D_WIDTH = D_GROUPS * D_GROUP_SIZE
D_STATE = 64
ODD_SPLITS = [C_INNER, C_INNER + C_CONV_DIM, C_INNER + C_CONV_DIM + 2 * C_HEADS]
ODD_IN = ODD_SPLITS[-1] + D_WIDTH
ODD_MIX = C_INNER + D_WIDTH
FF_DENSE = 2816
N_EXPERTS = 8
TOP_K = 2
FF_EXPERT = 1408

kernel_name = 'hybrid_mla_gqa_ssd_s5_diffusion_step'


def rms_norm(x, g):
    xf = x.astype(jnp.float32)
    y = xf * lax.rsqrt(jnp.mean(xf * xf, axis=-1, keepdims=True) + EPS)
    return (y * g.astype(jnp.float32)).astype(x.dtype)


def adaln(cond, w, b):
    m = jax.nn.silu(cond.astype(jnp.float32)) @ w.astype(jnp.float32) + b.astype(jnp.float32)
    m = m.reshape(cond.shape[0], 1, 6, D_MODEL)
    return tuple(m[:, :, i] for i in range(6))


def modulate(x, g, shift, scale):
    return (rms_norm(x, g).astype(jnp.float32) * (1.0 + scale) + shift).astype(x.dtype)


def axial_rope_tables(rows, dim):
    half = dim // 2
    inv = ROPE_THETA ** (-jnp.arange(0, half, 2, dtype=jnp.float32) / half)
    r = jnp.repeat(jnp.arange(rows, dtype=jnp.float32), GRID_W)
    col = jnp.tile(jnp.arange(GRID_W, dtype=jnp.float32), rows)
    ang = jnp.concatenate([r[:, None] * inv, col[:, None] * inv], axis=-1)
    return jnp.cos(ang), jnp.sin(ang)


def apply_rope(x, cos, sin):
    x1 = x[..., 0::2].astype(jnp.float32)
    x2 = x[..., 1::2].astype(jnp.float32)
    o = jnp.stack([x1 * cos - x2 * sin, x1 * sin + x2 * cos], axis=-1).reshape(x.shape)
    return o.astype(x.dtype)


def rope_tail(x, cos, sin, rdim):
    return jnp.concatenate([x[..., :-rdim], apply_rope(x[..., -rdim:], cos, sin)], axis=-1)


def blocked_attention(q, k, v):
    bsz, nkv, ng, lq, dk = q.shape
    dv = v.shape[-1]
    nb = lq // Q_BLOCK
    qb = q.reshape(bsz, nkv, ng, nb, Q_BLOCK, dk).transpose(3, 0, 1, 2, 4, 5)
    scale = dk ** -0.5

    def one_block(qi):
        s = jnp.einsum('bkgqd,bksd->bkgqs', qi, k, preferred_element_type=jnp.float32) * scale
        p = jax.nn.softmax(s, axis=-1).astype(v.dtype)
        return jnp.einsum('bkgqs,bksd->bkgqd', p, v)

    o = lax.map(one_block, qb)
    return o.transpose(1, 2, 3, 0, 4, 5).reshape(bsz, nkv, ng, lq, dv)


def mla_kv(ckv, krope, w_ukv, k_g):
    kv = jnp.einsum('blc,chd->bhld', ckv, w_ukv)
    k_nope, v = kv[..., :A_NOPE], kv[..., A_NOPE:]
    kr = jnp.broadcast_to(krope[:, None], k_nope.shape[:-1] + (A_ROPE,))
    k = rms_norm(jnp.concatenate([k_nope, kr], axis=-1), k_g)
    return k, v


def even_mixer(h, ew, ctx, rope_a, rope_b):
    bsz, L, _ = h.shape
    proj = h @ ew['w_in']
    cq, ckv, krope, qb, kb, vb = jnp.split(proj, EVEN_SPLITS, axis=-1)
    qa = jnp.einsum('blc,chd->bhld', rms_norm(cq, ew['a_q_lora_g']), ew['a_w_uq'])
    qa = rms_norm(qa, ew['a_q_norm_g'])
    ckv = rms_norm(ckv, ew['a_kv_lora_g'])
    ka, va = mla_kv(ckv, krope, ew['a_w_ukv'], ew['a_k_norm_g'])
    qb = rms_norm(qb.reshape(bsz, L, B_HEADS, B_HD).transpose(0, 2, 1, 3), ew['b_q_norm_g'])
    kb = rms_norm(kb.reshape(bsz, L, B_KV_HEADS, B_HD).transpose(0, 2, 1, 3), ew['b_k_norm_g'])
    vb = vb.reshape(bsz, L, B_KV_HEADS, B_HD).transpose(0, 2, 1, 3)
    if ctx is None:
        new_ctx = (ckv, krope, kb, vb)
    else:
        new_ctx = None
        cos_a, sin_a = rope_a
        cos_b, sin_b = rope_b
        qa = rope_tail(qa, cos_a, sin_a, A_ROPE)
        ka = rope_tail(ka, cos_a, sin_a, A_ROPE)
        qb = apply_rope(qb, cos_b, sin_b)
        kb = apply_rope(kb, cos_b, sin_b)
        c_ckv, c_krope, c_kb, c_vb = ctx
        cka, cva = mla_kv(c_ckv, c_krope, ew['a_w_ukv'], ew['a_k_norm_g'])
        ka = jnp.concatenate([cka, ka], axis=2)
        va = jnp.concatenate([cva, va], axis=2)
        kb = jnp.concatenate([c_kb, kb], axis=2)
        vb = jnp.concatenate([c_vb, vb], axis=2)
    oa = blocked_attention(qa[:, :, None], ka, va)
    ob = blocked_attention(qb.reshape(bsz, B_KV_HEADS, B_GROUP, L, B_HD), kb, vb)
    merge = lambda o: o.transpose(0, 3, 1, 2, 4).reshape(bsz, L, -1)
    out = jnp.concatenate([merge(oa), merge(ob)], axis=-1) @ ew['w_out']
    return out, new_ctx


def centred_dwconv(x, w, b):
    k = w.shape[0]
    pad = (k - 1) // 2
    y = lax.conv_general_dilated(x, w[:, None, :].astype(x.dtype), window_strides=(1,),
                                 padding=[(pad, k - 1 - pad)],
                                 dimension_numbers=('NWC', 'WIO', 'NWC'),
                                 feature_group_count=x.shape[-1])
    return y + b.astype(x.dtype)


def ssd_scan(x, dt, a_head, bm, cm, h0):
    bsz, L, G, E, P = x.shape
    N = bm.shape[-1]
    nc, Q = L // SSD_CHUNK, SSD_CHUNK
    la = (dt * a_head).reshape(bsz, nc, Q, G, E)
    xdt = (x * dt[..., None]).reshape(bsz, nc, Q, G, E, P)
    bm = bm.reshape(bsz, nc, Q, G, N)
    cm = cm.reshape(bsz, nc, Q, G, N)
    acum = jnp.cumsum(la, axis=2)
    seg = acum[:, :, :, None] - acum[:, :, None, :]
    causal = jnp.tril(jnp.ones((Q, Q), dtype=bool))[:, :, None, None]
    decay = jnp.exp(jnp.where(causal, seg, -jnp.inf))
    scores = jnp.einsum('bclgn,bcsgn->bclsg', cm, bm)[..., None] * decay
    y = jnp.einsum('bclsge,bcsgep->bclgep', scores, xdt)
    to_end = jnp.exp(acum[:, :, -1:] - acum)
    chunk_states = jnp.einsum('bclgn,bclgep->bcgepn', bm, xdt * to_end[..., None])
    chunk_decay = jnp.exp(acum[:, :, -1])

    def step(hc, inp):
        s, d = inp
        return hc * d[..., None, None] + s, hc

    h_last, h_in = lax.scan(step, h0, (jnp.moveaxis(chunk_states, 1, 0), jnp.moveaxis(chunk_decay, 1, 0)))
    h_in = jnp.moveaxis(h_in, 0, 1)
    y = y + jnp.einsum('bclgn,bcgepn->bclgep', cm, h_in) * jnp.exp(acum)[..., None]
    return y.reshape(bsz, L, G, E, P), h_last


def _s5_combine(e1, e2):
    a1r, a1i, b1r, b1i = e1
    a2r, a2i, b2r, b2i = e2
    return (a2r * a1r - a2i * a1i, a2r * a1i + a2i * a1r,
            a2r * b1r - a2i * b1i + b2r, a2r * b1i + a2i * b1r + b2i)


def s5_scan(u, lam_re, lam_im, log_dt, b_re, b_im, h0_re, h0_im):
    f32 = jnp.float32
    lr, li = lam_re.astype(f32), lam_im.astype(f32)
    dt = jnp.exp(log_dt.astype(f32))[:, None]
    mag = jnp.exp(lr * dt)
    ar, ai = mag * jnp.cos(li * dt), mag * jnp.sin(li * dt)
    den = lr * lr + li * li
    kr = ((ar - 1.0) * lr + ai * li) / den
    ki = (ai * lr - (ar - 1.0) * li) / den
    bur = jnp.einsum('blgc,gnc->blgn', u, b_re.astype(f32))
    bui = jnp.einsum('blgc,gnc->blgn', u, b_im.astype(f32))
    xr = kr * bur - ki * bui
    xi = kr * bui + ki * bur
    xr = xr.at[:, 0].add(ar * h0_re - ai * h0_im)
    xi = xi.at[:, 0].add(ar * h0_im + ai * h0_re)
    a_r = jnp.broadcast_to(ar, xr.shape)
    a_i = jnp.broadcast_to(ai, xi.shape)
    _, _, hr, hi = lax.associative_scan(_s5_combine, (a_r, a_i, xr, xi), axis=1)
    return hr, hi


def odd_mixer(h, ow, ctx):
    f32 = jnp.float32
    bsz, L, _ = h.shape
    flip = lambda t: jnp.flip(t, axis=1)
    proj = h @ ow['w_in']
    z, xbc, dt_raw, u = jnp.split(proj, ODD_SPLITS, axis=-1)
    xbc = jax.nn.silu(centred_dwconv(xbc, ow['conv_w'], ow['conv_b'])).astype(f32)
    xc, bm, cm = jnp.split(xbc, [C_INNER, C_INNER + C_GROUPS * C_STATE], axis=-1)
    xc = xc.reshape(bsz, L, C_GROUPS, C_HPG, C_HD)
    bm = bm.reshape(bsz, L, C_GROUPS, C_STATE)
    cm = cm.reshape(bsz, L, C_GROUPS, C_STATE)
    dt = jax.nn.softplus(dt_raw.astype(f32).reshape(bsz, L, 2, C_HEADS) + ow['dt_bias'].astype(f32))
    dt = dt.reshape(bsz, L, 2, C_GROUPS, C_HPG)
    a_head = -jnp.exp(ow['a_log'].astype(f32)).reshape(2, C_GROUPS, C_HPG)
    if ctx is None:
        h0c = jnp.zeros((bsz, 2, C_GROUPS, C_HPG, C_HD, C_STATE), f32)
        h0r = jnp.zeros((bsz, 2, D_GROUPS, D_STATE), f32)
        h0i = jnp.zeros((bsz, 2, D_GROUPS, D_STATE), f32)
    else:
        sc, sr, si = ctx
        h0c = sc.astype(f32).reshape(bsz, 2, C_GROUPS, C_HPG, C_HD, C_STATE)
        h0r, h0i = sr.astype(f32), si.astype(f32)
    y_f, hc_f = ssd_scan(xc, dt[:, :, 0], a_head[0], bm, cm, h0c[:, 0])
    y_b, hc_b = ssd_scan(flip(xc), flip(dt[:, :, 1]), a_head[1], flip(bm), flip(cm), h0c[:, 1])
    d_c = ow['c_d_skip'].astype(f32).reshape(C_GROUPS, C_HPG, 1)
    yc = (y_f + flip(y_b) + d_c * xc).reshape(bsz, L, C_INNER)
    yc = rms_norm(yc * jax.nn.silu(z.astype(f32)), ow['c_norm_g'])
    uu = u.astype(f32).reshape(bsz, L, D_GROUPS, D_GROUP_SIZE)
    hr_f, hi_f = s5_scan(uu, ow['lam_re'][0], ow['lam_im'][0], ow['log_dt'][0],
                         ow['b_re'], ow['b_im'], h0r[:, 0], h0i[:, 0])
    hr_b, hi_b = s5_scan(flip(uu), ow['lam_re'][1], ow['lam_im'][1], ow['log_dt'][1],
                         ow['b_re'], ow['b_im'], h0r[:, 1], h0i[:, 1])
    s_re = hr_f + flip(hr_b)
    s_im = hi_f + flip(hi_b)
    yd = (jnp.einsum('blgn,gcn->blgc', s_re, ow['c_re'].astype(f32))
          - jnp.einsum('blgn,gcn->blgc', s_im, ow['c_im'].astype(f32))
          + ow['d_skip'].astype(f32).reshape(D_GROUPS, D_GROUP_SIZE) * uu)
    yd = jax.nn.gelu(yd.reshape(bsz, L, D_WIDTH))
    yd = yd * jax.nn.sigmoid(yd @ ow['w_glu'].astype(f32))
    out = jnp.concatenate([yc, yd], axis=-1).astype(h.dtype) @ ow['w_out']
    new_state = None
    if ctx is None:
        new_state = (jnp.stack([hc_f, hc_b], axis=1).reshape(bsz, 2, C_HEADS, C_HD, C_STATE).astype(h.dtype),
                     jnp.stack([hr_f[:, -1], hr_b[:, -1]], axis=1).astype(h.dtype),
                     jnp.stack([hi_f[:, -1], hi_b[:, -1]], axis=1).astype(h.dtype))
    return out, new_state


def swiglu(h, w1, w3, w2):
    return (jax.nn.silu(h @ w1) * (h @ w3)) @ w2


def moe_swiglu(h, router_w, router_b, w1, w3, w2):
    logits = (h @ router_w).astype(jnp.float32) + router_b.astype(jnp.float32)
    top_v, top_i = lax.top_k(logits, TOP_K)
    gates = jax.nn.softmax(top_v, axis=-1)
    comb = jnp.sum(jax.nn.one_hot(top_i, N_EXPERTS, dtype=jnp.float32) * gates[..., None], axis=-2)
    out = jnp.zeros(h.shape, jnp.float32)
    for e in range(N_EXPERTS):
        out = out + comb[..., e:e + 1] * swiglu(h, w1[e], w3[e], w2[e])
    return out.astype(h.dtype)


def setup_inputs(seed: int = 0) -> dict:
    key = jax.random.key(seed)
    ks = list(jax.random.split(key, 64))
    f32 = jnp.float32

    def nrm(shape, scale):
        return scale * jax.random.normal(ks.pop(), shape, f32)

    def gain(shape):
        return 1.0 + nrm(shape, 0.1)

    def unif(shape, lo, hi):
        return jax.random.uniform(ks.pop(), shape, f32, lo, hi)

    NE, NO = N_EVEN, N_ODD
    dt0 = jnp.exp(unif((NO, 2, C_HEADS), math.log(1e-3), math.log(1e-1)))
    n_idx = jnp.arange(D_STATE, dtype=f32)
    inp = {}
    inp['x_prompt'] = nrm((BATCH, SEQ, D_MODEL), 1.0)
    inp['x_sample'] = nrm((DEC_BATCH, DEC_SEQ, D_MODEL), 1.0)
    inp['cache_a_ckv'] = nrm((DEC_BATCH, NE, PAST_LEN, A_KV_LORA), 1.0)
    inp['cache_a_krope'] = nrm((DEC_BATCH, NE, PAST_LEN, A_ROPE), 1.0)
    inp['cache_b_k'] = nrm((DEC_BATCH, NE, B_KV_HEADS, PAST_LEN, B_HD), 1.0)
    inp['cache_b_v'] = nrm((DEC_BATCH, NE, B_KV_HEADS, PAST_LEN, B_HD), 1.0)
    inp['state_c_ssm'] = nrm((DEC_BATCH, NO, 2, C_HEADS, C_HD, C_STATE), 0.1)
    inp['state_d_re'] = nrm((DEC_BATCH, NO, 2, D_GROUPS, D_STATE), 0.1)
    inp['state_d_im'] = nrm((DEC_BATCH, NO, 2, D_GROUPS, D_STATE), 0.1)
    inp['c'] = nrm((DEC_BATCH, D_MODEL), 1.0)
    inp['c_ctx'] = nrm((D_MODEL,), 1.0)
    inp['w_mod'] = nrm((DEPTH, D_MODEL, 6 * D_MODEL), 0.5 * D_MODEL ** -0.5)
    inp['b_mod'] = nrm((DEPTH, 6 * D_MODEL), 0.02)
    inp['norm1_g'] = gain((DEPTH, D_MODEL))
    inp['norm2_g'] = gain((DEPTH, D_MODEL))
    inp['ev_w_in'] = nrm((NE, D_MODEL, EVEN_IN), D_MODEL ** -0.5)
    inp['ev_a_q_lora_g'] = gain((NE, A_Q_LORA))
    inp['ev_a_w_uq'] = nrm((NE, A_Q_LORA, A_HEADS, A_QK), A_Q_LORA ** -0.5)
    inp['ev_a_kv_lora_g'] = gain((NE, A_KV_LORA))
    inp['ev_a_w_ukv'] = nrm((NE, A_KV_LORA, A_HEADS, A_NOPE + A_V), A_KV_LORA ** -0.5)
    inp['ev_a_q_norm_g'] = gain((NE, A_QK))
    inp['ev_a_k_norm_g'] = gain((NE, A_QK))
    inp['ev_b_q_norm_g'] = gain((NE, B_HD))
    inp['ev_b_k_norm_g'] = gain((NE, B_HD))
    inp['ev_w_out'] = nrm((NE, EVEN_MIX, D_MODEL), EVEN_MIX ** -0.5)
    inp['ffn_w1'] = nrm((NE, D_MODEL, FF_DENSE), D_MODEL ** -0.5)
    inp['ffn_w3'] = nrm((NE, D_MODEL, FF_DENSE), D_MODEL ** -0.5)
    inp['ffn_w2'] = nrm((NE, FF_DENSE, D_MODEL), FF_DENSE ** -0.5)
    inp['od_w_in'] = nrm((NO, D_MODEL, ODD_IN), D_MODEL ** -0.5)
    inp['od_conv_w'] = nrm((NO, C_CONV_K, C_CONV_DIM), C_CONV_K ** -0.5)
    inp['od_conv_b'] = nrm((NO, C_CONV_DIM), 0.02)
    inp['od_dt_bias'] = dt0 + jnp.log(-jnp.expm1(-dt0))
    inp['od_a_log'] = jnp.log(unif((NO, 2, C_HEADS), 1.0, 16.0))
    inp['od_c_d_skip'] = gain((NO, C_HEADS))
    inp['od_c_norm_g'] = gain((NO, C_INNER))
    inp['od_lam_re'] = -0.5 + nrm((NO, 2, D_GROUPS, D_STATE), 0.01)
    inp['od_lam_im'] = jnp.pi * n_idx + nrm((NO, 2, D_GROUPS, D_STATE), 0.01)
    inp['od_log_dt'] = unif((NO, 2, D_GROUPS), math.log(1e-3), math.log(1e-1))
    inp['od_b_re'] = nrm((NO, D_GROUPS, D_STATE, D_GROUP_SIZE), D_GROUP_SIZE ** -0.5)
    inp['od_b_im'] = nrm((NO, D_GROUPS, D_STATE, D_GROUP_SIZE), D_GROUP_SIZE ** -0.5)
    inp['od_c_re'] = nrm((NO, D_GROUPS, D_GROUP_SIZE, D_STATE), D_STATE ** -0.5)
    inp['od_c_im'] = nrm((NO, D_GROUPS, D_GROUP_SIZE, D_STATE), D_STATE ** -0.5)
    inp['od_d_skip'] = nrm((NO, D_WIDTH), 0.5)
    inp['od_w_glu'] = nrm((NO, D_WIDTH, D_WIDTH), D_WIDTH ** -0.5)
    inp['od_w_out'] = nrm((NO, ODD_MIX, D_MODEL), ODD_MIX ** -0.5)
    inp['moe_router_w'] = nrm((NO, D_MODEL, N_EXPERTS), D_MODEL ** -0.5)
    inp['moe_router_b'] = nrm((NO, N_EXPERTS), 0.01)
    inp['moe_w1'] = nrm((NO, N_EXPERTS, D_MODEL, FF_EXPERT), D_MODEL ** -0.5)
    inp['moe_w3'] = nrm((NO, N_EXPERTS, D_MODEL, FF_EXPERT), D_MODEL ** -0.5)
    inp['moe_w2'] = nrm((NO, N_EXPERTS, FF_EXPERT, D_MODEL), FF_EXPERT ** -0.5)
    return inp


def reference(x_prompt, x_sample, cache_a_ckv, cache_a_krope, cache_b_k, cache_b_v,
              state_c_ssm, state_d_re, state_d_im, c,
              c_ctx, w_mod, b_mod, norm1_g, norm2_g,
              ev_w_in, ev_a_q_lora_g, ev_a_w_uq, ev_a_kv_lora_g, ev_a_w_ukv,
              ev_a_q_norm_g, ev_a_k_norm_g, ev_b_q_norm_g, ev_b_k_norm_g, ev_w_out,
              ffn_w1, ffn_w3, ffn_w2,
              od_w_in, od_conv_w, od_conv_b, od_dt_bias, od_a_log, od_c_d_skip, od_c_norm_g,
              od_lam_re, od_lam_im, od_log_dt, od_b_re, od_b_im, od_c_re, od_c_im,
              od_d_skip, od_w_glu, od_w_out,
              moe_router_w, moe_router_b, moe_w1, moe_w3, moe_w2):
    rows = x_sample.shape[1] // GRID_W
    rope_a = axial_rope_tables(rows, A_ROPE)
    rope_b = axial_rope_tables(rows, B_HD)
    xp, xs = x_prompt, x_sample
    new_ckv, new_krope, new_bk, new_bv = [], [], [], []
    new_css, new_dre, new_dim = [], [], []
    for layer in range(DEPTH):
        mp_ = adaln(c_ctx[None, :], w_mod[layer], b_mod[layer])
        ms_ = adaln(c, w_mod[layer], b_mod[layer])
        hp = modulate(xp, norm1_g[layer], mp_[0], mp_[1])
        hs = modulate(xs, norm1_g[layer], ms_[0], ms_[1])
        if layer % 2 == 0:
            e = layer // 2
            ew = {'w_in': ev_w_in[e], 'a_q_lora_g': ev_a_q_lora_g[e], 'a_w_uq': ev_a_w_uq[e],
                  'a_kv_lora_g': ev_a_kv_lora_g[e], 'a_w_ukv': ev_a_w_ukv[e],
                  'a_q_norm_g': ev_a_q_norm_g[e], 'a_k_norm_g': ev_a_k_norm_g[e],
                  'b_q_norm_g': ev_b_q_norm_g[e], 'b_k_norm_g': ev_b_k_norm_g[e], 'w_out': ev_w_out[e]}
            mix_p, ctx_new = even_mixer(hp, ew, None, rope_a, rope_b)
            mix_s, _ = even_mixer(hs, ew, (cache_a_ckv[:, e], cache_a_krope[:, e], cache_b_k[:, e], cache_b_v[:, e]),
                                  rope_a, rope_b)
            new_ckv.append(ctx_new[0])
            new_krope.append(ctx_new[1])
            new_bk.append(ctx_new[2])
            new_bv.append(ctx_new[3])
        else:
            o = layer // 2
            ow = {'w_in': od_w_in[o], 'conv_w': od_conv_w[o], 'conv_b': od_conv_b[o],
                  'dt_bias': od_dt_bias[o], 'a_log': od_a_log[o], 'c_d_skip': od_c_d_skip[o],
                  'c_norm_g': od_c_norm_g[o], 'lam_re': od_lam_re[o], 'lam_im': od_lam_im[o],
                  'log_dt': od_log_dt[o], 'b_re': od_b_re[o], 'b_im': od_b_im[o],
                  'c_re': od_c_re[o], 'c_im': od_c_im[o], 'd_skip': od_d_skip[o],
                  'w_glu': od_w_glu[o], 'w_out': od_w_out[o]}
            mix_p, st_new = odd_mixer(hp, ow, None)
            mix_s, _ = odd_mixer(hs, ow, (state_c_ssm[:, o], state_d_re[:, o], state_d_im[:, o]))
            new_css.append(st_new[0])
            new_dre.append(st_new[1])
            new_dim.append(st_new[2])
        xp = (xp + mp_[2] * mix_p).astype(x_prompt.dtype)
        xs = (xs + ms_[2] * mix_s).astype(x_sample.dtype)
        hp = modulate(xp, norm2_g[layer], mp_[3], mp_[4])
        hs = modulate(xs, norm2_g[layer], ms_[3], ms_[4])
        if layer % 2 == 0:
            f_p = swiglu(hp, ffn_w1[e], ffn_w3[e], ffn_w2[e])
            f_s = swiglu(hs, ffn_w1[e], ffn_w3[e], ffn_w2[e])
        else:
            f_p = moe_swiglu(hp, moe_router_w[o], moe_router_b[o], moe_w1[o], moe_w3[o], moe_w2[o])
            f_s = moe_swiglu(hs, moe_router_w[o], moe_router_b[o], moe_w1[o], moe_w3[o], moe_w2[o])
        xp = (xp + mp_[5] * f_p).astype(x_prompt.dtype)
        xs = (xs + ms_[5] * f_s).astype(x_sample.dtype)
    return (xp, xs, jnp.stack(new_ckv, axis=1), jnp.stack(new_krope, axis=1),
            jnp.stack(new_bk, axis=1), jnp.stack(new_bv, axis=1),
            jnp.stack(new_css, axis=1), jnp.stack(new_dre, axis=1), jnp.stack(new_dim, axis=1))
```

```python
import functools
import math

import jax
import jax.numpy as jnp
from jax import lax
from jax.experimental import pallas as pl
from jax.experimental.pallas import tpu as pltpu

F32 = jnp.float32
BF16 = jnp.bfloat16

D_MODEL = 1024
BATCH = 32
SEQ = 256
DEPTH = 4
DEC_BATCH = 2
DEC_SEQ = 4096
PAST_LEN = 256
GRID_W = 64
ROPE_THETA = 10000.0
EPS = 1e-6
A_HEADS = 8
A_NOPE = 64
A_ROPE = 32
A_QK = A_NOPE + A_ROPE
A_V = 64
A_Q_LORA = 384
A_KV_LORA = 256
B_HEADS = 8
B_KV_HEADS = 2
B_HD = 64
C_HEADS = 16
C_HD = 64
C_GROUPS = 2
C_HPG = C_HEADS // C_GROUPS
C_STATE = 128
C_INNER = C_HEADS * C_HD
C_CONV_DIM = C_INNER + 2 * C_GROUPS * C_STATE
SSD_CHUNK = 128
D_GROUP_SIZE = 16
D_GROUPS = 32
D_WIDTH = D_GROUPS * D_GROUP_SIZE
D_STATE = 64
FF_DENSE = 2816
N_EXPERTS = 8
FF_EXPERT = 1408

TP = BATCH * SEQ
TS = DEC_BATCH * DEC_SEQ
T_ALL = TP + TS
LANE = 128
LOG2E = 1.4426950408889634
VMEM_LIMIT = 56 * 1024 * 1024

EV_CQ = 0
EV_CKV = 384
EV_KR = 640
EV_QB = 768
EV_KB = EV_QB + B_HEADS * LANE
EV_VB = EV_KB + B_KV_HEADS * LANE
EV_N = EV_VB + B_KV_HEADS * LANE
OD_Z = 0
OD_XBC = C_INNER
OD_U = OD_XBC + C_CONV_DIM
OD_DT = OD_U + D_WIDTH
OD_N = OD_DT + LANE
S5_STEPS = 256
S5_CHAINS = 32
S5_SEG = DEC_SEQ // S5_STEPS


def _cp(sem):
    return pltpu.CompilerParams(dimension_semantics=sem, vmem_limit_bytes=VMEM_LIMIT)


def _cond_idx(i, tm):
    t0 = i * tm
    return jnp.where(t0 < TP, 0, 1 + (t0 - TP) // DEC_SEQ)


def _mm(a, b):
    return jnp.dot(a, b, preferred_element_type=F32)


def _mm_nt(a, b):
    return lax.dot_general(a, b, (((1,), (1,)), ((), ())), preferred_element_type=F32)


def _split_bf16(v):
    hi = v.astype(BF16)
    lo = (v - hi.astype(F32)).astype(BF16)
    return hi, lo


def _silu(v):
    return v * jax.nn.sigmoid(v)


def _rms(v, n):
    return v * lax.rsqrt(jnp.sum(v * v, axis=-1, keepdims=True) * (1.0 / n) + EPS)


def _adaln_kernel(c_ref, w_ref, b_ref, o_ref):
    s = _silu(c_ref[...])
    o_ref[0] = _mm(s.astype(BF16), w_ref[0].astype(BF16)) + b_ref[0]


def _adaln(cond8, w_mod, b_mod):
    depth, d, n6 = w_mod.shape
    tn = 1536
    return pl.pallas_call(
        _adaln_kernel,
        grid=(depth, n6 // tn),
        in_specs=[pl.BlockSpec((8, d), lambda l, j: (0, 0)),
                  pl.BlockSpec((1, d, tn), lambda l, j: (l, 0, j)),
                  pl.BlockSpec((1, 1, tn), lambda l, j: (l, 0, j))],
        out_specs=pl.BlockSpec((1, 8, tn), lambda l, j: (l, 0, j)),
        out_shape=jax.ShapeDtypeStruct((depth, 8, n6), F32),
        compiler_params=_cp(("parallel", "parallel")),
        name="adaln",
    )(cond8, w_mod, b_mod.reshape(depth, 1, n6))


def _modulate(x, g, mod_ref, shift_row, scale_row):
    y = x * lax.rsqrt(jnp.mean(x * x, axis=-1, keepdims=True) + EPS) * g
    return y * (1.0 + mod_ref[0, scale_row:scale_row + 1, :]) + mod_ref[0, shift_row:shift_row + 1, :]


def _modmm_kernel(x_ref, mod_ref, g_ref, w_ref, o_ref):
    h = _modulate(x_ref[...], g_ref[...], mod_ref, 0, 1)
    o_ref[...] = _mm(h.astype(BF16), w_ref[...])


def _modmm(x, mod, g, w, tm=512):
    t, d = x.shape
    n = w.shape[1]
    return pl.pallas_call(
        _modmm_kernel,
        grid=(t // tm,),
        in_specs=[pl.BlockSpec((tm, d), lambda i: (i, 0)),
                  pl.BlockSpec((1, 8, d), lambda i: (_cond_idx(i, tm), 0, 0)),
                  pl.BlockSpec((1, d), lambda i: (0, 0)),
                  pl.BlockSpec((d, n), lambda i: (0, 0))],
        out_specs=pl.BlockSpec((tm, n), lambda i: (i, 0)),
        out_shape=jax.ShapeDtypeStruct((t, n), F32),
        compiler_params=_cp(("parallel",)),
        name="modmm",
    )(x, mod, g.reshape(1, d), w)


def _swap_pairs(v):
    lane = lax.broadcasted_iota(jnp.int32, v.shape, 1)
    nxt = pltpu.roll(v, LANE - 1, 1)
    prv = pltpu.roll(v, 1, 1)
    return jnp.where((lane & 1) == 0, nxt, prv)


def _rope(v, cos, sin):
    return v * cos + _swap_pairs(v) * sin


def _mla_kv(ckvn, kr_blk, wk_ref, wv_ref, gkn, rope_tabs, ka_o, va_o):
    cb = ckvn.astype(BF16)
    kn = _mm(cb, wk_ref[...])
    va_o[...] = _mm(cb, wv_ref[...]).astype(BF16)
    for h in range(A_HEADS):
        sl = slice(h * LANE, (h + 1) * LANE)
        kh = _rms(kn[:, sl] + kr_blk, A_QK) * gkn
        if rope_tabs is not None:
            kh = _rope(kh, rope_tabs[0], rope_tabs[1])
        ka_o[:, sl] = kh.astype(BF16)


def _evenprep_kernel(*refs, rope):
    it = iter(refs)
    proj = next(it)
    if rope:
        cos_a, sin_a, cos_b, sin_b = (next(it)[...] for _ in range(4))
    gql, wuq, gqn, gkvl, wk, wv, gkn, gbq, gbk = (next(it) for _ in range(9))
    qa_o, ka_o, va_o, qb_o, kb_o, vb_o = (next(it) for _ in range(6))
    if not rope:
        ckv_o, kr_o, kbn_o, vbf_o = (next(it) for _ in range(4))

    cqn = _rms(proj[:, EV_CQ:EV_CQ + A_Q_LORA], A_Q_LORA) * gql[...]
    q = _mm(cqn.astype(BF16), wuq[...])
    for h in range(A_HEADS):
        sl = slice(h * LANE, (h + 1) * LANE)
        qh = _rms(q[:, sl], A_QK) * gqn[...]
        if rope:
            qh = _rope(qh, cos_a, sin_a)
        qa_o[:, sl] = qh.astype(BF16)

    ckvn = _rms(proj[:, EV_CKV:EV_CKV + A_KV_LORA], A_KV_LORA) * gkvl[...]
    kr_blk = proj[:, EV_KR:EV_KR + LANE]
    _mla_kv(ckvn, kr_blk, wk, wv, gkn[...], (cos_a, sin_a) if rope else None, ka_o, va_o)

    for h in range(B_HEADS):
        sl = slice(h * LANE, (h + 1) * LANE)
        qh = _rms(proj[:, EV_QB + h * LANE:EV_QB + (h + 1) * LANE], B_HD) * gbq[...]
        if rope:
            qh = _rope(qh, cos_b, sin_b)
        qb_o[:, sl] = qh.astype(BF16)
    for j in range(B_KV_HEADS):
        sl = slice(j * LANE, (j + 1) * LANE)
        kh = _rms(proj[:, EV_KB + j * LANE:EV_KB + (j + 1) * LANE], B_HD) * gbk[...]
        if not rope:
            kbn_o[:, sl] = kh
        else:
            kh = _rope(kh, cos_b, sin_b)
        kb_o[:, sl] = kh.astype(BF16)
    vb = proj[:, EV_VB:EV_VB + B_KV_HEADS * LANE]
    vb_o[...] = vb.astype(BF16)
    if not rope:
        vbf_o[...] = vb
        ckv_o[...] = ckvn
        kr_o[...] = kr_blk


def _evenprep(proj, row_off, nrows, ew, rope_tabs):
    tm = 256
    rope = rope_tabs is not None
    off = row_off // tm
    full = lambda a: pl.BlockSpec(a.shape, lambda i: (0,) * a.ndim)
    rows = lambda w: pl.BlockSpec((tm, w), lambda i: (i, 0))
    ins = [proj]
    in_specs = [pl.BlockSpec((tm, EV_N), lambda i: (i + off, 0))]
    if rope:
        ins += list(rope_tabs)
        in_specs += [pl.BlockSpec((tm, LANE), lambda i: (i % (DEC_SEQ // tm), 0))] * 4
    ws = [ew["gql"], ew["wuq"], ew["gqn"], ew["gkvl"], ew["wk"], ew["wv"], ew["gkn"], ew["gbq"], ew["gbk"]]
    ins += ws
    in_specs += [full(a) for a in ws]
    hw = A_HEADS * LANE
    kvw = B_KV_HEADS * LANE
    out_shape = [jax.ShapeDtypeStruct((nrows, hw), BF16)] * 4 + [jax.ShapeDtypeStruct((nrows, kvw), BF16)] * 2
    out_specs = [rows(hw)] * 4 + [rows(kvw)] * 2
    if not rope:
        out_shape += [jax.ShapeDtypeStruct((nrows, A_KV_LORA), F32), jax.ShapeDtypeStruct((nrows, LANE), F32),
                      jax.ShapeDtypeStruct((nrows, kvw), F32), jax.ShapeDtypeStruct((nrows, kvw), F32)]
        out_specs += [rows(A_KV_LORA), rows(LANE), rows(kvw), rows(kvw)]
    return pl.pallas_call(
        functools.partial(_evenprep_kernel, rope=rope),
        grid=(nrows // tm,),
        in_specs=in_specs,
        out_specs=out_specs,
        out_shape=out_shape,
        compiler_params=_cp(("parallel",)),
        name="evenprep_rope" if rope else "evenprep",
    )(*ins)


def _ctxkv_kernel(ckv_ref, kr_ref, wk, wv, gkn, ka_o, va_o):
    _mla_kv(ckv_ref[...], kr_ref[...], wk, wv, gkn[...], None, ka_o, va_o)


def _ctxkv(ckvn, kr_blk, ew):
    n = ckvn.shape[0]
    hw = A_HEADS * LANE
    return pl.pallas_call(
        _ctxkv_kernel,
        out_shape=[jax.ShapeDtypeStruct((n, hw), BF16)] * 2,
        compiler_params=pltpu.CompilerParams(vmem_limit_bytes=VMEM_LIMIT),
        name="ctxkv",
    )(ckvn, kr_blk, ew["wk"], ew["wv"], ew["gkn"])


def _attn_kernel(*refs, hp, shared_kv, c, tk, nk, has_ctx):
    if has_ctx:
        q_ref, kn_ref, vn_ref, kc_ref, vc_ref, o_ref, m_sc, l_sc, acc_sc = refs
    else:
        q_ref, kn_ref, vn_ref, o_ref, m_sc, l_sc, acc_sc = refs
    outs = []
    for h in range(hp):
        kv = slice(0, LANE) if shared_kv else slice(h * LANE, (h + 1) * LANE)
        q = q_ref[:, h * LANE:(h + 1) * LANE]
        m_sc[...] = jnp.full(m_sc.shape, -jnp.inf, F32)
        l_sc[...] = jnp.zeros(l_sc.shape, F32)
        acc_sc[...] = jnp.zeros(acc_sc.shape, F32)

        def update(k, v, q=q):
            s = _mm_nt(q, k)
            m_prev = m_sc[...]
            m_new = jnp.maximum(m_prev, jnp.max(s, axis=-1, keepdims=True))
            alpha = jnp.exp2((m_prev - m_new) * c)
            p = jnp.exp2((s - m_new[:, 0:1]) * c)
            l_sc[...] = alpha * l_sc[...] + jnp.sum(p, axis=-1, keepdims=True)
            acc_sc[...] = alpha * acc_sc[...] + _mm(p.astype(BF16), v)
            m_sc[...] = m_new

        if has_ctx:
            update(kc_ref[:, kv], vc_ref[:, kv])
        if nk == 1:
            update(kn_ref[:, kv], vn_ref[:, kv])
        else:
            def body(j, carry, kv=kv, update=update):
                off = pl.multiple_of(j * tk, tk)
                update(kn_ref[pl.ds(off, tk), kv], vn_ref[pl.ds(off, tk), kv])
                return carry
            lax.fori_loop(0, nk, body, 0)
        outs.append(acc_sc[...] * (1.0 / l_sc[...]))
    for p2 in range(hp // 2):
        pair = jnp.concatenate([outs[2 * p2][:, :A_V], outs[2 * p2 + 1][:, :A_V]], axis=-1)
        o_ref[:, p2 * LANE:(p2 + 1) * LANE] = pair.astype(BF16)


def _attn(q, kn, vn, ctx, *, nb, lq, lk, nheads, hp, shared_kv, scale, tq, tk):
    ngroups = nheads // hp
    kvw = LANE if shared_kv else hp * LANE
    nq = lq // tq
    ins = [q, kn, vn]
    in_specs = [pl.BlockSpec((tq, hp * LANE), lambda b, g, i: (b * nq + i, g)),
                pl.BlockSpec((lk, kvw), lambda b, g, i: (b, g)),
                pl.BlockSpec((lk, kvw), lambda b, g, i: (b, g))]
    if ctx is not None:
        ins += list(ctx)
        in_specs += [pl.BlockSpec((PAST_LEN, kvw), lambda b, g, i: (b, g))] * 2
    return pl.pallas_call(
        functools.partial(_attn_kernel, hp=hp, shared_kv=shared_kv, c=scale * LOG2E, tk=tk, nk=lk // tk,
                          has_ctx=ctx is not None),
        grid=(nb, ngroups, nq),
        in_specs=in_specs,
        out_specs=pl.BlockSpec((tq, hp * A_V), lambda b, g, i: (b * nq + i, g)),
        out_shape=jax.ShapeDtypeStruct((nb * lq, nheads * A_V), BF16),
        scratch_shapes=[pltpu.VMEM((tq, LANE), F32)] * 3,
        compiler_params=_cp(("parallel", "parallel", "arbitrary")),
        name="attn_ctx" if ctx is not None else "attn",
    )(*ins)


def _outproj_kernel(*refs, glu):
    if glu:
        x_ref, mod_ref, a_ref, b_ref, wa_ref, wb_ref, wg_ref, o_ref = refs
        yd = jax.nn.gelu(b_ref[...])
        b = (yd * jax.nn.sigmoid(_mm(yd.astype(BF16), wg_ref[...]))).astype(BF16)
    else:
        x_ref, mod_ref, a_ref, b_ref, wa_ref, wb_ref, o_ref = refs
        b = b_ref[...]
    acc = _mm(a_ref[...], wa_ref[...]) + _mm(b, wb_ref[...])
    o_ref[...] = x_ref[...] + mod_ref[0, 2:3, :] * acc


def _outproj(x, mod, a, b, wa, wb, wg=None, tm=512):
    t, d = x.shape
    full = lambda v: pl.BlockSpec(v.shape, lambda i: (0,) * v.ndim)
    rows = lambda v: pl.BlockSpec((tm, v.shape[1]), lambda i: (i, 0))
    ins = [x, mod, a, b, wa, wb]
    in_specs = [rows(x), pl.BlockSpec((1, 8, d), lambda i: (_cond_idx(i, tm), 0, 0)), rows(a), rows(b),
                full(wa), full(wb)]
    if wg is not None:
        ins.append(wg)
        in_specs.append(full(wg))
    return pl.pallas_call(
        functools.partial(_outproj_kernel, glu=wg is not None),
        grid=(t // tm,),
        in_specs=in_specs,
        out_specs=rows(x),
        out_shape=jax.ShapeDtypeStruct((t, d), F32),
        compiler_params=_cp(("parallel",)),
        name="outproj_glu" if wg is not None else "outproj",
    )(*ins)


FF_SPLIT = (0, 768, FF_EXPERT)


def _ffn_kernel(*refs, routed, nj):
    if routed:
        x_ref, mod_ref, g_ref, rwh_ref, rwl_ref, rb_ref, w1_ref, w3_ref, w2_ref, o_ref, h_sc, acc_sc, comb_sc = refs
    else:
        x_ref, mod_ref, g_ref, w1_ref, w3_ref, w2_ref, o_ref, h_sc, acc_sc = refs
    j = pl.program_id(1)

    @pl.when(j == 0)
    def _():
        h = _modulate(x_ref[...], g_ref[...], mod_ref, 3, 4)
        h_hi = h.astype(BF16)
        h_sc[...] = h_hi
        acc_sc[...] = jnp.zeros(acc_sc.shape, F32)
        if routed:
            h_lo = (h - h_hi.astype(F32)).astype(BF16)
            logits = _mm(h_hi, rwh_ref[...]) + _mm(h_lo, rwh_ref[...]) + _mm(h_hi, rwl_ref[...]) + rb_ref[...]
            lane = lax.broadcasted_iota(jnp.int32, logits.shape, 1)
            lg = jnp.where(lane < N_EXPERTS, logits, -jnp.inf)
            m1 = jnp.max(lg, axis=-1, keepdims=True)
            i1 = jnp.min(jnp.where(lg == m1, lane, LANE), axis=-1, keepdims=True)
            lg2 = jnp.where(lane == i1, -jnp.inf, lg)
            m2 = jnp.max(lg2, axis=-1, keepdims=True)
            i2 = jnp.min(jnp.where(lg2 == m2, lane, LANE), axis=-1, keepdims=True)
            e2 = jnp.exp(m2 - m1)
            inv = 1.0 / (1.0 + e2)
            comb_sc[...] = jnp.where(lane == i1, inv, 0.0) + jnp.where(lane == i2, e2 * inv, 0.0)

    h = h_sc[...]
    w1 = w1_ref.at[0] if routed else w1_ref
    w3 = w3_ref.at[0] if routed else w3_ref
    w2 = w2_ref.at[0] if routed else w2_ref
    y = None
    for k in range(len(FF_SPLIT) - 1):
        sl = slice(FF_SPLIT[k], FF_SPLIT[k + 1])
        act = (_silu(_mm(h, w1[:, sl])) * _mm(h, w3[:, sl])).astype(BF16)
        part = _mm(act, w2[sl, :])
        y = part if y is None else y + part
    if routed:
        lane = lax.broadcasted_iota(jnp.int32, comb_sc.shape, 1)
        y = y * jnp.sum(jnp.where(lane == j, comb_sc[...], 0.0), axis=-1, keepdims=True)
    acc_sc[...] += y

    @pl.when(j == nj - 1)
    def _():
        o_ref[...] = x_ref[...] + mod_ref[0, 5:6, :] * acc_sc[...]


def _ffn(x, mod, g, w1, w3, w2, router=None, tm=1024):
    t, d = x.shape
    routed = router is not None
    tf = FF_EXPERT
    nj = w1.shape[0] if routed else w1.shape[1] // tf
    ins = [x, mod, g.reshape(1, d)]
    in_specs = [pl.BlockSpec((tm, d), lambda i, j: (i, 0)),
                pl.BlockSpec((1, 8, d), lambda i, j: (_cond_idx(i, tm), 0, 0)),
                pl.BlockSpec((1, d), lambda i, j: (0, 0))]
    scratch = [pltpu.VMEM((tm, d), BF16), pltpu.VMEM((tm, d), F32)]
    if routed:
        ins += list(router)
        in_specs += [pl.BlockSpec((d, LANE), lambda i, j: (0, 0))] * 2 + [pl.BlockSpec((1, LANE), lambda i, j: (0, 0))]
        in_specs += [pl.BlockSpec((1, d, tf), lambda i, j: (j, 0, 0))] * 2 + [pl.BlockSpec((1, tf, d), lambda i, j: (j, 0, 0))]
        scratch.append(pltpu.VMEM((tm, LANE), F32))
    else:
        in_specs += [pl.BlockSpec((d, tf), lambda i, j: (0, j))] * 2 + [pl.BlockSpec((tf, d), lambda i, j: (j, 0))]
    ins += [w1, w3, w2]
    return pl.pallas_call(
        functools.partial(_ffn_kernel, routed=routed, nj=nj),
        grid=(t // tm, nj),
        in_specs=in_specs,
        out_specs=pl.BlockSpec((tm, d), lambda i, j: (i, 0)),
        out_shape=jax.ShapeDtypeStruct((t, d), F32),
        scratch_shapes=scratch,
        compiler_params=_cp(("parallel", "arbitrary")),
        name="moe" if routed else "ffn",
    )(*ins)


CONV_ROWS = 4096


def _conv_kernel(x_ref, w_ref, b_ref, o_ref):
    i = pl.program_id(0)
    seq = jnp.where(i < TP // CONV_ROWS, SEQ, DEC_SEQ)
    x = x_ref[...]
    pos = lax.broadcasted_iota(jnp.int32, x.shape, 0) & (seq - 1)
    prev = jnp.where(pos == 0, 0.0, pltpu.roll(x, 1, 0))
    nxt = jnp.where(pos == seq - 1, 0.0, pltpu.roll(x, CONV_ROWS - 1, 0))
    y = prev * w_ref[0:1, :] + x * w_ref[1:2, :] + nxt * w_ref[2:3, :] + b_ref[...]
    o_ref[...] = _silu(y)


def _conv(proj, w, b):
    tc = 256
    t = proj.shape[0]
    return pl.pallas_call(
        _conv_kernel,
        grid=(t // CONV_ROWS, C_CONV_DIM // tc),
        in_specs=[pl.BlockSpec((CONV_ROWS, tc), lambda i, j: (i, j + OD_XBC // tc)),
                  pl.BlockSpec((3, tc), lambda i, j: (0, j)),
                  pl.BlockSpec((1, tc), lambda i, j: (0, j))],
        out_specs=pl.BlockSpec((CONV_ROWS, tc), lambda i, j: (i, j)),
        out_shape=jax.ShapeDtypeStruct((t, C_CONV_DIM), F32),
        compiler_params=_cp(("parallel", "parallel")),
        name="conv",
    )(proj, w, b.reshape(1, C_CONV_DIM))


Q = SSD_CHUNK
NCH = T_ALL // Q
NCH_P = TP // Q
NC_P = SEQ // Q
NC_S = DEC_SEQ // Q
NSEQ = BATCH + DEC_BATCH


def _chunk_seq(s):
    is_p = s < NCH_P
    nc = jnp.where(is_p, NC_P, NC_S)
    seq = jnp.where(is_p, s // NC_P, BATCH + (s - NCH_P) // NC_S)
    c = jnp.where(is_p, s % NC_P, (s - NCH_P) % NC_S)
    return seq, c, nc, s - c


def _chunk_mirror(s):
    _, c, nc, start = _chunk_seq(s)
    return start + nc - 1 - c


def _ssd_dt(dt_raw, dt_bias, a_row):
    z = dt_raw + dt_bias
    dt = jnp.maximum(z, 0.0) + jnp.log(1.0 + jnp.exp(-jnp.abs(z)))
    la = dt * a_row
    r = lax.broadcasted_iota(jnp.int32, (Q, Q), 0)
    cc = lax.broadcasted_iota(jnp.int32, (Q, Q), 1)
    tri = (cc <= r).astype(BF16)
    hi, lo = _split_bf16(la)
    return dt, la, _mm(tri, hi) + _mm(tri, lo)


def _expand(v, e32_ref):
    hi, lo = _split_bf16(v)
    return _mm(hi, e32_ref[...]) + _mm(lo, e32_ref[...])


def _ssd1_kernel(xf_ref, xb_ref, df_ref, db_ref, dtb_ref, a_ref, e32_ref, h0_ref, hf_o, hb_o, fin_o, hf_sc, hb_sc):
    s = pl.program_id(0)
    _, c, nc, _ = _chunk_seq(s)

    @pl.when(c == 0)
    def _():
        hf_sc[...] = h0_ref[0, 0]
        hb_sc[...] = h0_ref[0, 1]

    hf_o[0] = hf_sc[...]
    hb_o[0] = hb_sc[...]

    lane = lax.broadcasted_iota(jnp.int32, (Q, LANE), 1)
    is_f = lane < C_HEADS
    dt_f, _, ac_f = _ssd_dt(df_ref[...], dtb_ref[...], a_ref[...])
    tot_f = ac_f[Q - 1:Q, :]
    dt_b, la_b, ac_b = _ssd_dt(db_ref[...], dtb_ref[...], a_ref[...])
    tot_b = ac_b[Q - 1:Q, :]
    w = jnp.where(is_f, jnp.exp(tot_f - ac_f) * dt_f, jnp.exp(ac_b - la_b) * dt_b)
    is_f8 = lax.broadcasted_iota(jnp.int32, (8, LANE), 1) < C_HEADS
    dec = jnp.where(is_f8, jnp.exp(jnp.broadcast_to(tot_f, (8, LANE))), jnp.exp(jnp.broadcast_to(tot_b, (8, LANE))))
    w_x = _expand(w, e32_ref)
    dec_x = _expand(dec, e32_ref)[0:1, :]
    for d, (x_ref, st) in enumerate(((xf_ref, hf_sc), (xb_ref, hb_sc))):
        xw = (x_ref[:, 0:C_INNER] * w_x[:, d * C_INNER:(d + 1) * C_INNER]).astype(BF16)
        for g in range(C_GROUPS):
            bm_t = x_ref[:, C_INNER + g * C_STATE:C_INNER + (g + 1) * C_STATE].T.astype(BF16)
            cs = slice(g * C_HPG * C_HD, (g + 1) * C_HPG * C_HD)
            dcs = slice(d * C_INNER + g * C_HPG * C_HD, d * C_INNER + (g + 1) * C_HPG * C_HD)
            st[:, cs] = st[:, cs] * dec_x[:, dcs] + _mm(bm_t, xw[:, cs])

    @pl.when(c == nc - 1)
    def _():
        for d, st in enumerate((hf_sc, hb_sc)):
            for k in range(C_INNER // LANE):
                fin_o[0, d, k * LANE:(k + 1) * LANE, :] = st[:, k * LANE:(k + 1) * LANE].T


def _ssd1(xa, proj, dtb, a_row, e32, h0):
    dcol = OD_DT // LANE
    h0_idx = lambda s: (jnp.where(s < NCH_P, 0, 1 + (s - NCH_P) // NC_S), 0, 0, 0)
    return pl.pallas_call(
        _ssd1_kernel,
        grid=(NCH,),
        in_specs=[pl.BlockSpec((Q, C_CONV_DIM), lambda s: (s, 0)),
                  pl.BlockSpec((Q, C_CONV_DIM), lambda s: (_chunk_mirror(s), 0)),
                  pl.BlockSpec((Q, LANE), lambda s: (s, dcol)),
                  pl.BlockSpec((Q, LANE), lambda s: (_chunk_mirror(s), dcol)),
                  pl.BlockSpec((1, LANE), lambda s: (0, 0)),
                  pl.BlockSpec((1, LANE), lambda s: (0, 0)),
                  pl.BlockSpec((LANE, 2 * C_INNER), lambda s: (0, 0)),
                  pl.BlockSpec((1, 2, C_STATE, C_INNER), h0_idx)],
        out_specs=[pl.BlockSpec((1, C_STATE, C_INNER), lambda s: (s, 0, 0)),
                   pl.BlockSpec((1, C_STATE, C_INNER), lambda s: (_chunk_mirror(s), 0, 0)),
                   pl.BlockSpec((1, 2, C_INNER, C_STATE), lambda s: (_chunk_seq(s)[0], 0, 0, 0))],
        out_shape=[jax.ShapeDtypeStruct((NCH, C_STATE, C_INNER), F32),
                   jax.ShapeDtypeStruct((NCH, C_STATE, C_INNER), F32),
                   jax.ShapeDtypeStruct((NSEQ, 2, C_INNER, C_STATE), F32)],
        scratch_shapes=[pltpu.VMEM((C_STATE, C_INNER), F32)] * 2,
        compiler_params=_cp(("arbitrary",)),
        name="ssd_states",
    )(xa, xa, proj, proj, dtb, a_row, e32, h0)


def _ssd2_kernel(xa_ref, d_ref, z_ref, hf_ref, hb_ref, dtb_ref, a_ref, e32_ref, dsk_ref, gn_ref, o_ref, y_sc):
    dt, la, acum = _ssd_dt(d_ref[...], dtb_ref[...], a_ref[...])
    tot = acum[Q - 1:Q, :]
    excl = acum - la
    lane = lax.broadcasted_iota(jnp.int32, (Q, LANE), 1)
    is_f = lane < C_HEADS
    logdt = jnp.log(dt)
    col = jnp.where(is_f, acum, excl)
    row_t = jnp.where(is_f, acum - logdt, excl + logdt).T
    dec_x = _expand(jnp.where(is_f, jnp.exp(acum), jnp.exp(tot - excl)), e32_ref)
    x = xa_ref[:, 0:C_INNER]
    xb16 = x.astype(BF16)
    ti = lax.broadcasted_iota(jnp.int32, (Q, Q), 0)
    si = lax.broadcasted_iota(jnp.int32, (Q, Q), 1)
    low = si <= ti
    upp = si >= ti
    lane_q = lax.broadcasted_iota(jnp.int32, (Q, LANE), 1)
    gw = C_HPG * C_HD
    for g in range(C_GROUPS):
        bm = xa_ref[:, C_INNER + g * C_STATE:C_INNER + (g + 1) * C_STATE].astype(BF16)
        cm = xa_ref[:, C_INNER + (C_GROUPS + g) * C_STATE:C_INNER + (C_GROUPS + g + 1) * C_STATE].astype(BF16)
        gmat = _mm_nt(cm, bm)
        cs = slice(g * gw, (g + 1) * gw)
        ys = (_mm(cm, hf_ref[0, :, cs].astype(BF16)) * dec_x[:, cs]
              + _mm(cm, hb_ref[0, :, cs].astype(BF16)) * dec_x[:, C_INNER + g * gw:C_INNER + (g + 1) * gw])
        for pr in range(C_HPG // 2):
            xpair = xb16[:, g * gw + pr * LANE:g * gw + (pr + 1) * LANE]
            res = []
            for e in range(2):
                h = g * C_HPG + 2 * pr + e
                hb_ = C_HEADS + h
                m_f = jnp.where(low, jnp.exp(col[:, h:h + 1] - row_t[h:h + 1, :]), 0.0)
                m_b = jnp.where(upp, jnp.exp(row_t[hb_:hb_ + 1, :] - col[:, hb_:hb_ + 1]), 0.0)
                res.append(_mm((gmat * (m_f + m_b)).astype(BF16), xpair))
            y_sc[:, g * gw + pr * LANE:g * gw + (pr + 1) * LANE] = (
                jnp.where(lane_q < C_HD, res[0], res[1]) + ys[:, pr * LANE:(pr + 1) * LANE])
    y = y_sc[...] + dsk_ref[...] * x
    y = y * _silu(z_ref[...])
    y = y * lax.rsqrt(jnp.mean(y * y, axis=-1, keepdims=True) + EPS) * gn_ref[...]
    o_ref[...] = y.astype(BF16)


def _ssd2(xa, proj, hf, hb, dtb, a_row, e32, dsk, gn):
    dcol = OD_DT // LANE
    one = lambda w: pl.BlockSpec((1, w), lambda s: (0, 0))
    return pl.pallas_call(
        _ssd2_kernel,
        grid=(NCH,),
        in_specs=[pl.BlockSpec((Q, C_CONV_DIM), lambda s: (s, 0)),
                  pl.BlockSpec((Q, LANE), lambda s: (s, dcol)),
                  pl.BlockSpec((Q, C_INNER), lambda s: (s, 0)),
                  pl.BlockSpec((1, C_STATE, C_INNER), lambda s: (s, 0, 0)),
                  pl.BlockSpec((1, C_STATE, C_INNER), lambda s: (s, 0, 0)),
                  one(LANE), one(LANE),
                  pl.BlockSpec((LANE, 2 * C_INNER), lambda s: (0, 0)),
                  one(C_INNER), one(C_INNER)],
        out_specs=pl.BlockSpec((Q, C_INNER), lambda s: (s, 0)),
        out_shape=jax.ShapeDtypeStruct((T_ALL, C_INNER), BF16),
        scratch_shapes=[pltpu.VMEM((Q, C_INNER), F32)],
        compiler_params=_cp(("parallel",)),
        name="ssd_out",
    )(xa, proj, proj, hf, hb, dtb, a_row, e32, dsk, gn)


S5_MM_STEPS = 32


def _cmul(ar, ai, br, bi):
    return ar * br - ai * bi, ar * bi + ai * br


def _s5_kernel(u_ref, wx_ref, cw_ref, a_ref, dsk_ref, h0_ref, y_ref, st_ref, xs, *, nseg):
    ct = pl.program_id(0)
    quarter = ct % 4
    nslab = S5_STEPS // S5_MM_STEPS
    rows = S5_MM_STEPS * S5_CHAINS

    def fill(k, carry):
        r0 = pl.multiple_of(k * S5_MM_STEPS, S5_MM_STEPS)
        ub = u_ref[pl.ds(r0, S5_MM_STEPS)].reshape(rows, LANE).astype(BF16)
        xs[pl.ds(r0, S5_MM_STEPS)] = _mm(ub, wx_ref[0]).reshape(S5_MM_STEPS, S5_CHAINS, 4 * LANE)
        return carry
    lax.fori_loop(0, nslab, fill, 0)

    shape = (S5_CHAINS, LANE)
    a = a_ref[0]
    afr, afi, abr, abi = (jnp.broadcast_to(a[k:k + 1, :], shape) for k in range(4))

    def scan(init, store):
        def body(i, carry):
            hfr, hfi, hbr, hbi = carry
            j = S5_STEPS - 1 - i
            pr, pi = _cmul(afr, afi, hfr, hfi)
            nfr = pr + xs[i, :, 0:LANE]
            nfi = pi + xs[i, :, LANE:2 * LANE]
            pr, pi = _cmul(abr, abi, hbr, hbi)
            nbr = pr + xs[j, :, 2 * LANE:3 * LANE]
            nbi = pi + xs[j, :, 3 * LANE:4 * LANE]
            if store:
                xs[i, :, 0:LANE] = nfr
                xs[i, :, LANE:2 * LANE] = nfi
                xs[j, :, 2 * LANE:3 * LANE] = nbr
                xs[j, :, 3 * LANE:4 * LANE] = nbi
            return nfr, nfi, nbr, nbi
        return lax.fori_loop(0, S5_STEPS, body, init)

    zero = jnp.zeros(shape, F32)
    if nseg == 1:
        init = (zero, zero, zero, zero)
    else:
        efr, efi, ebr, ebi = scan((zero, zero, zero, zero), store=False)
        pfr, pfi, pbr, pbi = afr, afi, abr, abi
        for _ in range(int(math.log2(S5_STEPS))):
            pfr, pfi = _cmul(pfr, pfi, pfr, pfi)
            pbr, pbi = _cmul(pbr, pbi, pbr, pbi)
        row = lax.broadcasted_iota(jnp.int32, shape, 0) % nseg
        first = row == 0
        last = row == nseg - 1
        h0 = h0_ref[...]
        cfr, cfi, cbr, cbi = zero, zero, zero, zero
        dn = lambda v: pltpu.roll(v, 1, 0)
        up = lambda v: pltpu.roll(v, S5_CHAINS - 1, 0)
        for _ in range(nseg):
            pr, pi = _cmul(pfr, pfi, dn(cfr), dn(cfi))
            cfr = jnp.where(first, h0[0], pr + dn(efr))
            cfi = jnp.where(first, h0[1], pi + dn(efi))
            pr, pi = _cmul(pbr, pbi, up(cbr), up(cbi))
            cbr = jnp.where(last, h0[2], pr + up(ebr))
            cbi = jnp.where(last, h0[3], pi + up(ebi))
        init = (cfr, cfi, cbr, cbi)
    fin = scan(init, store=True)
    for k in range(4):
        st_ref[k] = fin[k]

    def emit(k, carry):
        r0 = pl.multiple_of(k * S5_MM_STEPS, S5_MM_STEPS)
        blk = xs[pl.ds(r0, S5_MM_STEPS)]
        s_re = blk[:, :, 0:LANE] + blk[:, :, 2 * LANE:3 * LANE]
        s_im = blk[:, :, LANE:2 * LANE] + blk[:, :, 3 * LANE:4 * LANE]
        s2 = jnp.concatenate([s_re, s_im], axis=-1).reshape(rows, 2 * LANE).astype(BF16)
        yv = _mm(s2, cw_ref[0]).reshape(S5_MM_STEPS, S5_CHAINS, LANE)

        @pl.when(quarter == 0)
        def _():
            y_ref[pl.ds(r0, S5_MM_STEPS)] = yv + dsk_ref[...] * u_ref[pl.ds(r0, S5_MM_STEPS)]

        @pl.when(quarter != 0)
        def _():
            y_ref[pl.ds(r0, S5_MM_STEPS)] += yv
        return carry
    lax.fori_loop(0, nslab, emit, 0)


def _s5(u_perm, sw, h0, nseg):
    nct = D_GROUPS // 2
    return pl.pallas_call(
        functools.partial(_s5_kernel, nseg=nseg),
        grid=(nct,),
        in_specs=[pl.BlockSpec((S5_STEPS, S5_CHAINS, LANE), lambda ct: (0, 0, ct // 4)),
                  pl.BlockSpec((1, LANE, 4 * LANE), lambda ct: (ct, 0, 0)),
                  pl.BlockSpec((1, 2 * LANE, LANE), lambda ct: (ct, 0, 0)),
                  pl.BlockSpec((1, 8, LANE), lambda ct: (ct, 0, 0)),
                  pl.BlockSpec((1, LANE), lambda ct: (0, ct // 4)),
                  pl.BlockSpec((4, S5_CHAINS, LANE), lambda ct: (0, 0, ct))],
        out_specs=[pl.BlockSpec((S5_STEPS, S5_CHAINS, LANE), lambda ct: (0, 0, ct // 4)),
                   pl.BlockSpec((4, S5_CHAINS, LANE), lambda ct: (0, 0, ct))],
        out_shape=[jax.ShapeDtypeStruct((S5_STEPS, S5_CHAINS, D_WIDTH), F32),
                   jax.ShapeDtypeStruct((4, S5_CHAINS, nct * LANE), F32)],
        scratch_shapes=[pltpu.VMEM((S5_STEPS, S5_CHAINS, 4 * LANE), F32)],
        compiler_params=_cp(("arbitrary",)),
        name="s5_seg" if nseg > 1 else "s5",
    )(u_perm, sw["wx"], sw["cw"], sw["a"], sw["dsk"], h0)


def _pad_last(a, n):
    return jnp.pad(a, [(0, 0)] * (a.ndim - 1) + [(0, n - a.shape[-1])])


def _even_weights(w_in, q_lora_g, w_uq, kv_lora_g, w_ukv, q_norm_g, k_norm_g, bq_g, bk_g):
    d = w_in.shape[0]
    s0, s1, s2, s3, s4 = 384, 640, 672, 1184, 1312
    kr = jnp.pad(w_in[:, s1:s2], ((0, 0), (A_NOPE, LANE - A_QK)))
    heads = lambda w, n: _pad_last(w.reshape(d, n, B_HD), LANE).reshape(d, n * LANE)
    w_perm = jnp.concatenate([w_in[:, :s0], w_in[:, s0:s1], kr, heads(w_in[:, s2:s3], B_HEADS),
                              heads(w_in[:, s3:s4], B_KV_HEADS), heads(w_in[:, s4:], B_KV_HEADS)], axis=1)
    row = lambda g: _pad_last(g.reshape(1, -1), LANE)
    return {
        "w_in": w_perm.astype(BF16),
        "gql": q_lora_g.reshape(1, -1),
        "wuq": _pad_last(w_uq, LANE).reshape(A_Q_LORA, A_HEADS * LANE).astype(BF16),
        "gqn": row(q_norm_g),
        "gkvl": kv_lora_g.reshape(1, -1),
        "wk": _pad_last(w_ukv[:, :, :A_NOPE], LANE).reshape(A_KV_LORA, A_HEADS * LANE).astype(BF16),
        "wv": _pad_last(w_ukv[:, :, A_NOPE:], LANE).reshape(A_KV_LORA, A_HEADS * LANE).astype(BF16),
        "gkn": row(k_norm_g),
        "gbq": row(bq_g),
        "gbk": row(bk_g),
    }


def _rope_tables():
    rows = DEC_SEQ // GRID_W

    def angles(dim):
        half = dim // 2
        inv = ROPE_THETA ** (-jnp.arange(0, half, 2, dtype=F32) / half)
        r = jnp.repeat(jnp.arange(rows, dtype=F32), GRID_W)
        col = jnp.tile(jnp.arange(GRID_W, dtype=F32), rows)
        return jnp.concatenate([r[:, None] * inv, col[:, None] * inv], axis=-1)

    def lanes(dim, start):
        ang = angles(dim)
        cos = jnp.repeat(jnp.cos(ang), 2, axis=-1)
        sin = jnp.repeat(jnp.sin(ang), 2, axis=-1) * jnp.tile(jnp.array([-1.0, 1.0], F32), dim // 2)
        pad = ((0, 0), (start, LANE - start - dim))
        return jnp.pad(cos, pad, constant_values=1.0), jnp.pad(sin, pad)

    cos_a, sin_a = lanes(A_ROPE, A_NOPE)
    cos_b, sin_b = lanes(B_HD, 0)
    return cos_a, sin_a, cos_b, sin_b


def _odd_weights(w_in, dt_bias, a_log, c_d_skip):
    w_perm = jnp.concatenate([w_in[:, :C_INNER], w_in[:, C_INNER:C_INNER + C_CONV_DIM],
                              w_in[:, C_INNER + C_CONV_DIM + 2 * C_HEADS:],
                              _pad_last(w_in[:, C_INNER + C_CONV_DIM:C_INNER + C_CONV_DIM + 2 * C_HEADS], LANE)], axis=1)
    lane = jnp.arange(LANE)
    col = jnp.arange(2 * C_INNER)
    e32 = ((lane[:, None] < 2 * C_HEADS) & (lane[:, None] == col[None, :] // C_HD)).astype(BF16)
    return {
        "w_in": w_perm.astype(BF16),
        "dtb": _pad_last(dt_bias.reshape(1, -1), LANE),
        "a_row": _pad_last(-jnp.exp(a_log).reshape(1, -1), LANE),
        "e32": e32,
        "dsk": jnp.repeat(c_d_skip, C_HD).reshape(1, C_INNER),
    }


def _s5_weights(lam_re, lam_im, log_dt, b_re, b_im, c_re, c_im, d_skip):
    nct = D_GROUPS // 2
    dt = jnp.exp(log_dt)[..., None]
    mag = jnp.exp(lam_re * dt)
    ar, ai = mag * jnp.cos(lam_im * dt), mag * jnp.sin(lam_im * dt)
    den = lam_re * lam_re + lam_im * lam_im
    kr = ((ar - 1.0) * lam_re + ai * lam_im) / den
    ki = (ai * lam_re - (ar - 1.0) * lam_im) / den
    xr = kr[..., None] * b_re[None] - ki[..., None] * b_im[None]
    xi = kr[..., None] * b_im[None] + ki[..., None] * b_re[None]
    xc = jnp.stack([xr[0], xi[0], xr[1], xi[1]]).reshape(4, nct, 2, D_STATE, D_GROUP_SIZE)
    quarter = (jnp.arange(nct)[:, None] % 4 == jnp.arange(4)[None, :]).astype(F32)
    eye2 = jnp.eye(2, dtype=F32)
    wx = jnp.einsum("tq,gh,ktgnj->tqgjkhn", quarter, eye2, xc).reshape(nct, LANE, 4 * LANE)
    cc = jnp.stack([c_re, -c_im]).reshape(2, nct, 2, D_GROUP_SIZE, D_STATE)
    cw = jnp.einsum("tq,gh,rtgcn->trhnqgc", quarter, eye2, cc).reshape(nct, 2 * LANE, LANE)
    planes = jnp.stack([ar[0], ai[0], ar[1], ai[1]]).reshape(4, nct, LANE).transpose(1, 0, 2)
    return {"wx": wx.astype(BF16), "cw": cw.astype(BF16), "a": jnp.pad(planes, ((0, 0), (0, 4), (0, 0))),
            "dsk": d_skip.reshape(1, D_WIDTH)}


def _even_layer(x, mod, norm1_g, norm2_g, ew, w_out, ffn_w, caches, rope_tabs):
    c_ckv, c_krope, c_kb, c_vb = caches
    proj = _modmm(x, mod, norm1_g, ew["w_in"])
    qa_p, ka_p, va_p, qb_p, kb_p, vb_p, ckv_c, kr_c, kbn_c, vb_c = _evenprep(proj, 0, TP, ew, None)
    qa_s, ka_s, va_s, qb_s, kb_s, vb_s = _evenprep(proj, TP, TS, ew, rope_tabs)
    n_ctx = DEC_BATCH * PAST_LEN
    kr_blk = jnp.pad(c_krope.reshape(n_ctx, A_ROPE), ((0, 0), (A_NOPE, LANE - A_QK)))
    ka_c, va_c = _ctxkv(c_ckv.reshape(n_ctx, A_KV_LORA), kr_blk, ew)
    heads_b = lambda a: _pad_last(a.transpose(0, 2, 1, 3), LANE).reshape(n_ctx, B_KV_HEADS * LANE).astype(BF16)
    sa, sb = A_QK ** -0.5, B_HD ** -0.5
    oa_p = _attn(qa_p, ka_p, va_p, None, nb=BATCH, lq=SEQ, lk=SEQ, nheads=A_HEADS, hp=2, shared_kv=False,
                 scale=sa, tq=SEQ, tk=SEQ)
    ob_p = _attn(qb_p, kb_p, vb_p, None, nb=BATCH, lq=SEQ, lk=SEQ, nheads=B_HEADS, hp=4, shared_kv=True,
                 scale=sb, tq=SEQ, tk=SEQ)
    oa_s = _attn(qa_s, ka_s, va_s, (ka_c, va_c), nb=DEC_BATCH, lq=DEC_SEQ, lk=DEC_SEQ, nheads=A_HEADS, hp=2,
                 shared_kv=False, scale=sa, tq=256, tk=512)
    ob_s = _attn(qb_s, kb_s, vb_s, (heads_b(c_kb), heads_b(c_vb)), nb=DEC_BATCH, lq=DEC_SEQ, lk=DEC_SEQ,
                 nheads=B_HEADS, hp=4, shared_kv=True, scale=sb, tq=256, tk=512)
    oa = jnp.concatenate([oa_p, oa_s], axis=0)
    ob = jnp.concatenate([ob_p, ob_s], axis=0)
    x = _outproj(x, mod, oa, ob, w_out[:A_HEADS * A_V], w_out[A_HEADS * A_V:])
    x = _ffn(x, mod, norm2_g, *ffn_w)
    unhead = lambda a: a.reshape(BATCH, SEQ, B_KV_HEADS, LANE)[..., :B_HD].transpose(0, 2, 1, 3)
    new_cache = (ckv_c.reshape(BATCH, SEQ, A_KV_LORA), kr_c[:, A_NOPE:A_QK].reshape(BATCH, SEQ, A_ROPE),
                 unhead(kbn_c), unhead(vb_c))
    return x, new_cache


def _odd_layer(x, mod, norm1_g, norm2_g, ow, sw, conv_w, conv_b, c_norm_g, w_glu, w_out, moe_w, states):
    st_c, st_re, st_im = states
    proj = _modmm(x, mod, norm1_g, ow["w_in"])
    xa = _conv(proj, conv_w, conv_b)
    h0 = st_c.transpose(0, 1, 4, 2, 3).reshape(DEC_BATCH, 2, C_STATE, C_INNER)
    h0 = jnp.concatenate([jnp.zeros((1,) + h0.shape[1:], F32), h0], axis=0)
    hf, hb, fin = _ssd1(xa, proj, ow["dtb"], ow["a_row"], ow["e32"], h0)
    yc = _ssd2(xa, proj, hf, hb, ow["dtb"], ow["a_row"], ow["e32"], ow["dsk"], c_norm_g.reshape(1, C_INNER))
    u = proj[:, OD_U:OD_U + D_WIDTH]
    u_p = u[:TP].reshape(BATCH, SEQ, D_WIDTH).transpose(1, 0, 2)
    u_s = u[TP:].reshape(DEC_BATCH * S5_SEG, S5_STEPS, D_WIDTH).transpose(1, 0, 2)
    zeros_h0 = jnp.zeros((4, S5_CHAINS, D_GROUPS * D_STATE), F32)
    h0_s = jnp.stack([st_re[:, 0], st_im[:, 0], st_re[:, 1], st_im[:, 1]]).reshape(4, DEC_BATCH, D_GROUPS * D_STATE)
    h0_s = jnp.repeat(h0_s, S5_SEG, axis=1)
    yd_p, fin_d = _s5(u_p, sw, zeros_h0, 1)
    yd_s, _ = _s5(u_s, sw, h0_s, S5_SEG)
    yd = jnp.concatenate([yd_p.transpose(1, 0, 2).reshape(TP, D_WIDTH),
                          yd_s.transpose(1, 0, 2).reshape(TS, D_WIDTH)], axis=0)
    x = _outproj(x, mod, yc, yd, w_out[:C_INNER], w_out[C_INNER:], wg=w_glu)
    x = _ffn(x, mod, norm2_g, *moe_w[:3], router=moe_w[3])
    new_c = fin[:BATCH].reshape(BATCH, 2, C_HEADS, C_HD, C_STATE)
    fin_d = fin_d.reshape(4, BATCH, D_GROUPS, D_STATE)
    new_re = jnp.stack([fin_d[0], fin_d[2]], axis=1)
    new_im = jnp.stack([fin_d[1], fin_d[3]], axis=1)
    return x, (new_c, new_re, new_im)


def kernel(x_prompt, x_sample, cache_a_ckv, cache_a_krope, cache_b_k, cache_b_v, state_c_ssm, state_d_re, state_d_im, c, c_ctx, w_mod, b_mod, norm1_g, norm2_g, ev_w_in, ev_a_q_lora_g, ev_a_w_uq, ev_a_kv_lora_g, ev_a_w_ukv, ev_a_q_norm_g, ev_a_k_norm_g, ev_b_q_norm_g, ev_b_k_norm_g, ev_w_out, ffn_w1, ffn_w3, ffn_w2, od_w_in, od_conv_w, od_conv_b, od_dt_bias, od_a_log, od_c_d_skip, od_c_norm_g, od_lam_re, od_lam_im, od_log_dt, od_b_re, od_b_im, od_c_re, od_c_im, od_d_skip, od_w_glu, od_w_out, moe_router_w, moe_router_b, moe_w1, moe_w3, moe_w2):
    x = jnp.concatenate([x_prompt.reshape(TP, D_MODEL), x_sample.reshape(TS, D_MODEL)], axis=0)
    cond8 = jnp.zeros((8, D_MODEL), F32).at[0].set(c_ctx).at[1:1 + DEC_BATCH].set(c)
    mods = _adaln(cond8, w_mod, b_mod)[:, :1 + DEC_BATCH].reshape(DEPTH, 1 + DEC_BATCH, 6, D_MODEL)
    mods = jnp.pad(mods, ((0, 0), (0, 0), (0, 2), (0, 0)))
    rope_tabs = _rope_tables()
    ev_new, od_new = [], []
    for layer in range(DEPTH):
        mod = mods[layer]
        if layer % 2 == 0:
            e = layer // 2
            ew = _even_weights(ev_w_in[e], ev_a_q_lora_g[e], ev_a_w_uq[e], ev_a_kv_lora_g[e], ev_a_w_ukv[e],
                               ev_a_q_norm_g[e], ev_a_k_norm_g[e], ev_b_q_norm_g[e], ev_b_k_norm_g[e])
            ffn_w = (ffn_w1[e].astype(BF16), ffn_w3[e].astype(BF16), ffn_w2[e].astype(BF16))
            x, new = _even_layer(x, mod, norm1_g[layer], norm2_g[layer], ew, ev_w_out[e].astype(BF16), ffn_w,
                                 (cache_a_ckv[:, e], cache_a_krope[:, e], cache_b_k[:, e], cache_b_v[:, e]),
                                 rope_tabs)
            ev_new.append(new)
        else:
            o = layer // 2
            ow = _odd_weights(od_w_in[o], od_dt_bias[o], od_a_log[o], od_c_d_skip[o])
            sw = _s5_weights(od_lam_re[o], od_lam_im[o], od_log_dt[o], od_b_re[o], od_b_im[o], od_c_re[o],
                             od_c_im[o], od_d_skip[o])
            rw_hi = _pad_last(moe_router_w[o], LANE).astype(BF16)
            rw_lo = (_pad_last(moe_router_w[o], LANE) - rw_hi.astype(F32)).astype(BF16)
            moe_w = (moe_w1[o].astype(BF16), moe_w3[o].astype(BF16), moe_w2[o].astype(BF16),
                     (rw_hi, rw_lo, _pad_last(moe_router_b[o].reshape(1, -1), LANE)))
            x, new = _odd_layer(x, mod, norm1_g[layer], norm2_g[layer], ow, sw, od_conv_w[o], od_conv_b[o],
                                od_c_norm_g[o], od_w_glu[o].astype(BF16), od_w_out[o].astype(BF16), moe_w,
                                (state_c_ssm[:, o], state_d_re[:, o], state_d_im[:, o]))
            od_new.append(new)
    stack = lambda items, k: jnp.stack([it[k] for it in items], axis=1)
    return (x[:TP].reshape(BATCH, SEQ, D_MODEL), x[TP:].reshape(DEC_BATCH, DEC_SEQ, D_MODEL),
            stack(ev_new, 0), stack(ev_new, 1), stack(ev_new, 2), stack(ev_new, 3),
            stack(od_new, 0), stack(od_new, 1), stack(od_new, 2))
```

```python
import functools
import math

import jax
import jax.numpy as jnp
from jax import lax
from jax.experimental import pallas as pl
from jax.experimental.pallas import tpu as pltpu

F32 = jnp.float32
BF16 = jnp.bfloat16

D_MODEL = 1024
BATCH = 32
SEQ = 256
DEPTH = 4
DEC_BATCH = 2
DEC_SEQ = 4096
PAST_LEN = 256
GRID_W = 64
ROPE_THETA = 10000.0
EPS = 1e-6
A_HEADS = 8
A_NOPE = 64
A_ROPE = 32
A_QK = A_NOPE + A_ROPE
A_V = 64
A_Q_LORA = 384
A_KV_LORA = 256
B_HEADS = 8
B_KV_HEADS = 2
B_HD = 64
C_HEADS = 16
C_HD = 64
C_GROUPS = 2
C_HPG = C_HEADS // C_GROUPS
C_STATE = 128
C_INNER = C_HEADS * C_HD
C_CONV_DIM = C_INNER + 2 * C_GROUPS * C_STATE
SSD_CHUNK = 128
D_GROUP_SIZE = 16
D_GROUPS = 32
D_WIDTH = D_GROUPS * D_GROUP_SIZE
D_STATE = 64
FF_DENSE = 2816
N_EXPERTS = 8
FF_EXPERT = 1408

TP = BATCH * SEQ
TS = DEC_BATCH * DEC_SEQ
T_ALL = TP + TS
LANE = 128
LOG2E = 1.4426950408889634
VMEM_LIMIT = 56 * 1024 * 1024

EV_CQ = 0
EV_CKV = 384
EV_KR = 640
EV_QB = 768
EV_KB = EV_QB + B_HEADS * LANE
EV_VB = EV_KB + B_KV_HEADS * LANE
EV_N = EV_VB + B_KV_HEADS * LANE
OD_Z = 0
OD_XBC = C_INNER
OD_U = OD_XBC + C_CONV_DIM
OD_DT = OD_U + D_WIDTH
OD_N = OD_DT + LANE
S5_STEPS = 256
S5_CHAINS = 32
S5_SEG = DEC_SEQ // S5_STEPS
ATT_TQ = 256


def _cp(sem):
    return pltpu.CompilerParams(dimension_semantics=sem, vmem_limit_bytes=VMEM_LIMIT)


def _cond_idx(i, tm):
    t0 = i * tm
    return jnp.where(t0 < TP, 0, 1 + (t0 - TP) // DEC_SEQ)


def _mm(a, b):
    return jnp.dot(a, b, preferred_element_type=F32)


def _mm_nt(a, b):
    return lax.dot_general(a, b, (((1,), (1,)), ((), ())), preferred_element_type=F32)


def _split_bf16(v):
    hi = v.astype(BF16)
    lo = (v - hi.astype(F32)).astype(BF16)
    return hi, lo


def _silu(v):
    return v * jax.nn.sigmoid(v)


def _rms(v, n):
    return v * lax.rsqrt(jnp.sum(v * v, axis=-1, keepdims=True) * (1.0 / n) + EPS)


def _adaln_kernel(c_ref, w_ref, b_ref, o_ref):
    s = _silu(c_ref[...])
    o_ref[0] = _mm(s.astype(BF16), w_ref[0].astype(BF16)) + b_ref[0]


def _adaln(cond8, w_mod, b_mod):
    depth, d, n6 = w_mod.shape
    tn = 1536
    return pl.pallas_call(
        _adaln_kernel,
        grid=(depth, n6 // tn),
        in_specs=[pl.BlockSpec((8, d), lambda l, j: (0, 0)),
                  pl.BlockSpec((1, d, tn), lambda l, j: (l, 0, j)),
                  pl.BlockSpec((1, 1, tn), lambda l, j: (l, 0, j))],
        out_specs=pl.BlockSpec((1, 8, tn), lambda l, j: (l, 0, j)),
        out_shape=jax.ShapeDtypeStruct((depth, 8, n6), F32),
        compiler_params=_cp(("parallel", "parallel")),
        name="adaln",
    )(cond8, w_mod, b_mod.reshape(depth, 1, n6))


def _modulate(x, g, mod_ref, shift_row, scale_row):
    y = x * lax.rsqrt(jnp.mean(x * x, axis=-1, keepdims=True) + EPS) * g
    return y * (1.0 + mod_ref[0, scale_row:scale_row + 1, :]) + mod_ref[0, shift_row:shift_row + 1, :]


def _modmm_kernel(x_ref, mod_ref, g_ref, w_ref, o_ref):
    h = _modulate(x_ref[...], g_ref[...], mod_ref, 0, 1)
    o_ref[...] = _mm(h.astype(BF16), w_ref[...])


def _modmm(x, mod, g, w, tm=512):
    t, d = x.shape
    n = w.shape[1]
    return pl.pallas_call(
        _modmm_kernel,
        grid=(t // tm,),
        in_specs=[pl.BlockSpec((tm, d), lambda i: (i, 0)),
                  pl.BlockSpec((1, 8, d), lambda i: (_cond_idx(i, tm), 0, 0)),
                  pl.BlockSpec((1, d), lambda i: (0, 0)),
                  pl.BlockSpec((d, n), lambda i: (0, 0))],
        out_specs=pl.BlockSpec((tm, n), lambda i: (i, 0)),
        out_shape=jax.ShapeDtypeStruct((t, n), F32),
        compiler_params=_cp(("parallel",)),
        name="modmm",
    )(x, mod, g.reshape(1, d), w)


def _swap_pairs(v):
    lane = lax.broadcasted_iota(jnp.int32, v.shape, 1)
    nxt = pltpu.roll(v, LANE - 1, 1)
    prv = pltpu.roll(v, 1, 1)
    return jnp.where((lane & 1) == 0, nxt, prv)


def _rope(v, cos, sin):
    return v * cos + _swap_pairs(v) * sin


def _mla_kv(ckvn, kr_blk, wk_ref, wv_ref, gkn, rope_tabs, ka_o, va_o):
    cb = ckvn.astype(BF16)
    kn = _mm(cb, wk_ref[...])
    va_o[...] = _mm(cb, wv_ref[...]).astype(BF16)
    for h in range(A_HEADS):
        sl = slice(h * LANE, (h + 1) * LANE)
        kh = _rms(kn[:, sl] + kr_blk, A_QK) * gkn
        if rope_tabs is not None:
            kh = _rope(kh, rope_tabs[0], rope_tabs[1])
        ka_o[:, sl] = kh.astype(BF16)


def _evenprep_kernel(*refs, rope):
    it = iter(refs)
    proj = next(it)
    if rope:
        cos_a, sin_a, cos_b, sin_b = (next(it)[...] for _ in range(4))
    gql, wuq, gqn, gkvl, wk, wv, gkn, gbq, gbk = (next(it) for _ in range(9))
    qa_o, ka_o, va_o, qb_o, kb_o, vb_o = (next(it) for _ in range(6))
    if not rope:
        ckv_o, kr_o, kbn_o, vbf_o = (next(it) for _ in range(4))

    cqn = _rms(proj[:, EV_CQ:EV_CQ + A_Q_LORA], A_Q_LORA) * gql[...]
    q = _mm(cqn.astype(BF16), wuq[...])
    for h in range(A_HEADS):
        sl = slice(h * LANE, (h + 1) * LANE)
        qh = _rms(q[:, sl], A_QK) * gqn[...]
        if rope:
            qh = _rope(qh, cos_a, sin_a)
        qa_o[:, sl] = qh.astype(BF16)

    ckvn = _rms(proj[:, EV_CKV:EV_CKV + A_KV_LORA], A_KV_LORA) * gkvl[...]
    kr_blk = proj[:, EV_KR:EV_KR + LANE]
    _mla_kv(ckvn, kr_blk, wk, wv, gkn[...], (cos_a, sin_a) if rope else None, ka_o, va_o)

    for h in range(B_HEADS):
        sl = slice(h * LANE, (h + 1) * LANE)
        qh = _rms(proj[:, EV_QB + h * LANE:EV_QB + (h + 1) * LANE], B_HD) * gbq[...]
        if rope:
            qh = _rope(qh, cos_b, sin_b)
        qb_o[:, sl] = qh.astype(BF16)
    for j in range(B_KV_HEADS):
        sl = slice(j * LANE, (j + 1) * LANE)
        kh = _rms(proj[:, EV_KB + j * LANE:EV_KB + (j + 1) * LANE], B_HD) * gbk[...]
        if not rope:
            kbn_o[:, sl] = kh
        else:
            kh = _rope(kh, cos_b, sin_b)
        kb_o[:, sl] = kh.astype(BF16)
    vb = proj[:, EV_VB:EV_VB + B_KV_HEADS * LANE]
    vb_o[...] = vb.astype(BF16)
    if not rope:
        vbf_o[...] = vb
        ckv_o[...] = ckvn
        kr_o[...] = kr_blk


def _evenprep(proj, row_off, nrows, ew, rope_tabs):
    tm = 256
    rope = rope_tabs is not None
    off = row_off // tm
    full = lambda a: pl.BlockSpec(a.shape, lambda i: (0,) * a.ndim)
    rows = lambda w: pl.BlockSpec((tm, w), lambda i: (i, 0))
    ins = [proj]
    in_specs = [pl.BlockSpec((tm, EV_N), lambda i: (i + off, 0))]
    if rope:
        ins += list(rope_tabs)
        in_specs += [pl.BlockSpec((tm, LANE), lambda i: (i % (DEC_SEQ // tm), 0))] * 4
    ws = [ew["gql"], ew["wuq"], ew["gqn"], ew["gkvl"], ew["wk"], ew["wv"], ew["gkn"], ew["gbq"], ew["gbk"]]
    ins += ws
    in_specs += [full(a) for a in ws]
    hw = A_HEADS * LANE
    kvw = B_KV_HEADS * LANE
    out_shape = [jax.ShapeDtypeStruct((nrows, hw), BF16)] * 4 + [jax.ShapeDtypeStruct((nrows, kvw), BF16)] * 2
    out_specs = [rows(hw)] * 4 + [rows(kvw)] * 2
    if not rope:
        out_shape += [jax.ShapeDtypeStruct((nrows, A_KV_LORA), F32), jax.ShapeDtypeStruct((nrows, LANE), F32),
                      jax.ShapeDtypeStruct((nrows, kvw), F32), jax.ShapeDtypeStruct((nrows, kvw), F32)]
        out_specs += [rows(A_KV_LORA), rows(LANE), rows(kvw), rows(kvw)]
    return pl.pallas_call(
        functools.partial(_evenprep_kernel, rope=rope),
        grid=(nrows // tm,),
        in_specs=in_specs,
        out_specs=out_specs,
        out_shape=out_shape,
        compiler_params=_cp(("parallel",)),
        name="evenprep_rope" if rope else "evenprep",
    )(*ins)


def _ctxkv_kernel(ckv_ref, kr_ref, wk, wv, gkn, ka_o, va_o):
    _mla_kv(ckv_ref[...], kr_ref[...], wk, wv, gkn[...], None, ka_o, va_o)


def _ctxkv(ckvn, kr_blk, ew):
    n = ckvn.shape[0]
    hw = A_HEADS * LANE
    return pl.pallas_call(
        _ctxkv_kernel,
        out_shape=[jax.ShapeDtypeStruct((n, hw), BF16)] * 2,
        compiler_params=pltpu.CompilerParams(vmem_limit_bytes=VMEM_LIMIT),
        name="ctxkv",
    )(ckvn, kr_blk, ew["wk"], ew["wv"], ew["gkn"])


def _attn_kernel(*refs, hp, shared_kv, c, has_ctx):
    if has_ctx:
        q_ref, kn_ref, vn_ref, kc_ref, vc_ref, o_ref = refs
    else:
        q_ref, kn_ref, vn_ref, o_ref = refs
    outs = []
    for h in range(hp):
        kv = slice(0, LANE) if shared_kv else slice(h * LANE, (h + 1) * LANE)
        q = q_ref[:, h * LANE:(h + 1) * LANE]
        s_n = _mm_nt(q, kn_ref[:, kv])
        m = jnp.max(s_n, axis=-1, keepdims=True)
        if has_ctx:
            s_c = _mm_nt(q, kc_ref[:, kv])
            m = jnp.maximum(m, jnp.max(s_c, axis=-1, keepdims=True))
        p_n = jnp.exp2((s_n - m) * c)
        l = jnp.sum(p_n, axis=-1, keepdims=True)
        acc = _mm(p_n.astype(BF16), vn_ref[:, kv])
        if has_ctx:
            p_c = jnp.exp2((s_c - m) * c)
            l = l + jnp.sum(p_c, axis=-1, keepdims=True)
            acc = acc + _mm(p_c.astype(BF16), vc_ref[:, kv])
        outs.append(acc * (1.0 / l))
    for p2 in range(hp // 2):
        pair = jnp.concatenate([outs[2 * p2][:, :A_V], outs[2 * p2 + 1][:, :A_V]], axis=-1)
        o_ref[:, p2 * LANE:(p2 + 1) * LANE] = pair.astype(BF16)


def _attn(q, kn, vn, ctx, *, nb, lq, lk, nheads, hp, shared_kv, scale, tq):
    ngroups = nheads // hp
    kvw = LANE if shared_kv else hp * LANE
    nq = lq // tq
    ins = [q, kn, vn]
    in_specs = [pl.BlockSpec((tq, hp * LANE), lambda b, g, i: (b * nq + i, g)),
                pl.BlockSpec((lk, kvw), lambda b, g, i: (b, g)),
                pl.BlockSpec((lk, kvw), lambda b, g, i: (b, g))]
    if ctx is not None:
        ins += list(ctx)
        in_specs += [pl.BlockSpec((PAST_LEN, kvw), lambda b, g, i: (b, g))] * 2
    return pl.pallas_call(
        functools.partial(_attn_kernel, hp=hp, shared_kv=shared_kv, c=scale * LOG2E, has_ctx=ctx is not None),
        grid=(nb, ngroups, nq),
        in_specs=in_specs,
        out_specs=pl.BlockSpec((tq, hp * A_V), lambda b, g, i: (b * nq + i, g)),
        out_shape=jax.ShapeDtypeStruct((nb * lq, nheads * A_V), BF16),
        compiler_params=_cp(("parallel", "parallel", "arbitrary")),
        name="attn_ctx" if ctx is not None else "attn",
    )(*ins)


def _outproj_kernel(*refs, glu):
    if glu:
        x_ref, mod_ref, a_ref, b_ref, wa_ref, wb_ref, wg_ref, o_ref = refs
        yd = jax.nn.gelu(b_ref[...])
        b = (yd * jax.nn.sigmoid(_mm(yd.astype(BF16), wg_ref[...]))).astype(BF16)
    else:
        x_ref, mod_ref, a_ref, b_ref, wa_ref, wb_ref, o_ref = refs
        b = b_ref[...]
    acc = _mm(a_ref[...], wa_ref[...]) + _mm(b, wb_ref[...])
    o_ref[...] = x_ref[...] + mod_ref[0, 2:3, :] * acc


def _outproj(x, mod, a, b, wa, wb, wg=None, tm=512):
    t, d = x.shape
    full = lambda v: pl.BlockSpec(v.shape, lambda i: (0,) * v.ndim)
    rows = lambda v: pl.BlockSpec((tm, v.shape[1]), lambda i: (i, 0))
    ins = [x, mod, a, b, wa, wb]
    in_specs = [rows(x), pl.BlockSpec((1, 8, d), lambda i: (_cond_idx(i, tm), 0, 0)), rows(a), rows(b),
                full(wa), full(wb)]
    if wg is not None:
        ins.append(wg)
        in_specs.append(full(wg))
    return pl.pallas_call(
        functools.partial(_outproj_kernel, glu=wg is not None),
        grid=(t // tm,),
        in_specs=in_specs,
        out_specs=rows(x),
        out_shape=jax.ShapeDtypeStruct((t, d), F32),
        compiler_params=_cp(("parallel",)),
        name="outproj_glu" if wg is not None else "outproj",
    )(*ins)


FF_SPLIT = (0, 768, FF_EXPERT)


def _ffn_kernel(*refs, routed, nj):
    if routed:
        x_ref, mod_ref, g_ref, rwh_ref, rwl_ref, rb_ref, w1_ref, w3_ref, w2_ref, o_ref, h_sc, acc_sc, comb_sc = refs
    else:
        x_ref, mod_ref, g_ref, w1_ref, w3_ref, w2_ref, o_ref, h_sc, acc_sc = refs
    j = pl.program_id(1)

    @pl.when(j == 0)
    def _():
        h = _modulate(x_ref[...], g_ref[...], mod_ref, 3, 4)
        h_hi = h.astype(BF16)
        h_sc[...] = h_hi
        acc_sc[...] = jnp.zeros(acc_sc.shape, F32)
        if routed:
            h_lo = (h - h_hi.astype(F32)).astype(BF16)
            logits = _mm(h_hi, rwh_ref[...]) + _mm(h_lo, rwh_ref[...]) + _mm(h_hi, rwl_ref[...]) + rb_ref[...]
            lane = lax.broadcasted_iota(jnp.int32, logits.shape, 1)
            lg = jnp.where(lane < N_EXPERTS, logits, -jnp.inf)
            m1 = jnp.max(lg, axis=-1, keepdims=True)
            i1 = jnp.min(jnp.where(lg == m1, lane, LANE), axis=-1, keepdims=True)
            lg2 = jnp.where(lane == i1, -jnp.inf, lg)
            m2 = jnp.max(lg2, axis=-1, keepdims=True)
            i2 = jnp.min(jnp.where(lg2 == m2, lane, LANE), axis=-1, keepdims=True)
            e2 = jnp.exp(m2 - m1)
            inv = 1.0 / (1.0 + e2)
            comb_sc[...] = jnp.where(lane == i1, inv, 0.0) + jnp.where(lane == i2, e2 * inv, 0.0)

    h = h_sc[...]
    w1 = w1_ref.at[0] if routed else w1_ref
    w3 = w3_ref.at[0] if routed else w3_ref
    w2 = w2_ref.at[0] if routed else w2_ref
    y = None
    for k in range(len(FF_SPLIT) - 1):
        sl = slice(FF_SPLIT[k], FF_SPLIT[k + 1])
        act = (_silu(_mm(h, w1[:, sl])) * _mm(h, w3[:, sl])).astype(BF16)
        part = _mm(act, w2[sl, :])
        y = part if y is None else y + part
    if routed:
        lane = lax.broadcasted_iota(jnp.int32, comb_sc.shape, 1)
        y = y * jnp.sum(jnp.where(lane == j, comb_sc[...], 0.0), axis=-1, keepdims=True)
    acc_sc[...] += y

    @pl.when(j == nj - 1)
    def _():
        o_ref[...] = x_ref[...] + mod_ref[0, 5:6, :] * acc_sc[...]


def _ffn(x, mod, g, w1, w3, w2, router=None, tm=1024):
    t, d = x.shape
    routed = router is not None
    tf = FF_EXPERT
    nj = w1.shape[0] if routed else w1.shape[1] // tf
    ins = [x, mod, g.reshape(1, d)]
    in_specs = [pl.BlockSpec((tm, d), lambda i, j: (i, 0)),
                pl.BlockSpec((1, 8, d), lambda i, j: (_cond_idx(i, tm), 0, 0)),
                pl.BlockSpec((1, d), lambda i, j: (0, 0))]
    scratch = [pltpu.VMEM((tm, d), BF16), pltpu.VMEM((tm, d), F32)]
    if routed:
        ins += list(router)
        in_specs += [pl.BlockSpec((d, LANE), lambda i, j: (0, 0))] * 2 + [pl.BlockSpec((1, LANE), lambda i, j: (0, 0))]
        in_specs += [pl.BlockSpec((1, d, tf), lambda i, j: (j, 0, 0))] * 2 + [pl.BlockSpec((1, tf, d), lambda i, j: (j, 0, 0))]
        scratch.append(pltpu.VMEM((tm, LANE), F32))
    else:
        in_specs += [pl.BlockSpec((d, tf), lambda i, j: (0, j))] * 2 + [pl.BlockSpec((tf, d), lambda i, j: (j, 0))]
    ins += [w1, w3, w2]
    return pl.pallas_call(
        functools.partial(_ffn_kernel, routed=routed, nj=nj),
        grid=(t // tm, nj),
        in_specs=in_specs,
        out_specs=pl.BlockSpec((tm, d), lambda i, j: (i, 0)),
        out_shape=jax.ShapeDtypeStruct((t, d), F32),
        scratch_shapes=scratch,
        compiler_params=_cp(("parallel", "arbitrary")),
        name="moe" if routed else "ffn",
    )(*ins)


CONV_ROWS = 4096


def _conv_kernel(x_ref, w_ref, b_ref, o_ref):
    i = pl.program_id(0)
    seq = jnp.where(i < TP // CONV_ROWS, SEQ, DEC_SEQ)
    x = x_ref[...]
    pos = lax.broadcasted_iota(jnp.int32, x.shape, 0) & (seq - 1)
    prev = jnp.where(pos == 0, 0.0, pltpu.roll(x, 1, 0))
    nxt = jnp.where(pos == seq - 1, 0.0, pltpu.roll(x, CONV_ROWS - 1, 0))
    y = prev * w_ref[0:1, :] + x * w_ref[1:2, :] + nxt * w_ref[2:3, :] + b_ref[...]
    o_ref[...] = _silu(y)


def _conv(proj, w, b):
    tc = 256
    t = proj.shape[0]
    return pl.pallas_call(
        _conv_kernel,
        grid=(t // CONV_ROWS, C_CONV_DIM // tc),
        in_specs=[pl.BlockSpec((CONV_ROWS, tc), lambda i, j: (i, j + OD_XBC // tc)),
                  pl.BlockSpec((3, tc), lambda i, j: (0, j)),
                  pl.BlockSpec((1, tc), lambda i, j: (0, j))],
        out_specs=pl.BlockSpec((CONV_ROWS, tc), lambda i, j: (i, j)),
        out_shape=jax.ShapeDtypeStruct((t, C_CONV_DIM), F32),
        compiler_params=_cp(("parallel", "parallel")),
        name="conv",
    )(proj, w, b.reshape(1, C_CONV_DIM))


Q = SSD_CHUNK
NCH = T_ALL // Q
NCH_P = TP // Q
NC_P = SEQ // Q
NC_S = DEC_SEQ // Q
NSEQ = BATCH + DEC_BATCH


def _chunk_seq(s):
    is_p = s < NCH_P
    nc = jnp.where(is_p, NC_P, NC_S)
    seq = jnp.where(is_p, s // NC_P, BATCH + (s - NCH_P) // NC_S)
    c = jnp.where(is_p, s % NC_P, (s - NCH_P) % NC_S)
    return seq, c, nc, s - c


def _chunk_mirror(s):
    _, c, nc, start = _chunk_seq(s)
    return start + nc - 1 - c


def _ssd_dt(dt_raw, dt_bias, a_row):
    z = dt_raw + dt_bias
    dt = jnp.maximum(z, 0.0) + jnp.log(1.0 + jnp.exp(-jnp.abs(z)))
    la = dt * a_row
    r = lax.broadcasted_iota(jnp.int32, (Q, Q), 0)
    cc = lax.broadcasted_iota(jnp.int32, (Q, Q), 1)
    tri = (cc <= r).astype(BF16)
    hi, lo = _split_bf16(la)
    return dt, la, _mm(tri, hi) + _mm(tri, lo)


def _expand(v, e32_ref):
    hi, lo = _split_bf16(v)
    return _mm(hi, e32_ref[...]) + _mm(lo, e32_ref[...])


def _ssd1_kernel(xf_ref, xb_ref, df_ref, db_ref, dtb_ref, a_ref, e32_ref, h0_ref, hf_o, hb_o, fin_o, hf_sc, hb_sc):
    s = pl.program_id(0)
    _, c, nc, _ = _chunk_seq(s)

    @pl.when(c == 0)
    def _():
        hf_sc[...] = h0_ref[0, 0]
        hb_sc[...] = h0_ref[0, 1]

    hf_o[0] = hf_sc[...]
    hb_o[0] = hb_sc[...]

    lane = lax.broadcasted_iota(jnp.int32, (Q, LANE), 1)
    is_f = lane < C_HEADS
    dt_f, _, ac_f = _ssd_dt(df_ref[...], dtb_ref[...], a_ref[...])
    tot_f = ac_f[Q - 1:Q, :]
    dt_b, la_b, ac_b = _ssd_dt(db_ref[...], dtb_ref[...], a_ref[...])
    tot_b = ac_b[Q - 1:Q, :]
    w = jnp.where(is_f, jnp.exp(tot_f - ac_f) * dt_f, jnp.exp(ac_b - la_b) * dt_b)
    is_f8 = lax.broadcasted_iota(jnp.int32, (8, LANE), 1) < C_HEADS
    dec = jnp.where(is_f8, jnp.exp(jnp.broadcast_to(tot_f, (8, LANE))), jnp.exp(jnp.broadcast_to(tot_b, (8, LANE))))
    w_x = _expand(w, e32_ref)
    dec_x = _expand(dec, e32_ref)[0:1, :]
    for d, (x_ref, st) in enumerate(((xf_ref, hf_sc), (xb_ref, hb_sc))):
        xw = (x_ref[:, 0:C_INNER] * w_x[:, d * C_INNER:(d + 1) * C_INNER]).astype(BF16)
        for g in range(C_GROUPS):
            bm_t = x_ref[:, C_INNER + g * C_STATE:C_INNER + (g + 1) * C_STATE].T.astype(BF16)
            cs = slice(g * C_HPG * C_HD, (g + 1) * C_HPG * C_HD)
            dcs = slice(d * C_INNER + g * C_HPG * C_HD, d * C_INNER + (g + 1) * C_HPG * C_HD)
            st[:, cs] = st[:, cs] * dec_x[:, dcs] + _mm(bm_t, xw[:, cs])

    @pl.when(c == nc - 1)
    def _():
        for d, st in enumerate((hf_sc, hb_sc)):
            for k in range(C_INNER // LANE):
                fin_o[0, d, k * LANE:(k + 1) * LANE, :] = st[:, k * LANE:(k + 1) * LANE].T


def _ssd1(xa, proj, dtb, a_row, e32, h0):
    dcol = OD_DT // LANE
    h0_idx = lambda s: (jnp.where(s < NCH_P, 0, 1 + (s - NCH_P) // NC_S), 0, 0, 0)
    return pl.pallas_call(
        _ssd1_kernel,
        grid=(NCH,),
        in_specs=[pl.BlockSpec((Q, C_CONV_DIM), lambda s: (s, 0)),
                  pl.BlockSpec((Q, C_CONV_DIM), lambda s: (_chunk_mirror(s), 0)),
                  pl.BlockSpec((Q, LANE), lambda s: (s, dcol)),
                  pl.BlockSpec((Q, LANE), lambda s: (_chunk_mirror(s), dcol)),
                  pl.BlockSpec((1, LANE), lambda s: (0, 0)),
                  pl.BlockSpec((1, LANE), lambda s: (0, 0)),
                  pl.BlockSpec((LANE, 2 * C_INNER), lambda s: (0, 0)),
                  pl.BlockSpec((1, 2, C_STATE, C_INNER), h0_idx)],
        out_specs=[pl.BlockSpec((1, C_STATE, C_INNER), lambda s: (s, 0, 0)),
                   pl.BlockSpec((1, C_STATE, C_INNER), lambda s: (_chunk_mirror(s), 0, 0)),
                   pl.BlockSpec((1, 2, C_INNER, C_STATE), lambda s: (_chunk_seq(s)[0], 0, 0, 0))],
        out_shape=[jax.ShapeDtypeStruct((NCH, C_STATE, C_INNER), F32),
                   jax.ShapeDtypeStruct((NCH, C_STATE, C_INNER), F32),
                   jax.ShapeDtypeStruct((NSEQ, 2, C_INNER, C_STATE), F32)],
        scratch_shapes=[pltpu.VMEM((C_STATE, C_INNER), F32)] * 2,
        compiler_params=_cp(("arbitrary",)),
        name="ssd_states",
    )(xa, xa, proj, proj, dtb, a_row, e32, h0)


def _ssd2_kernel(xa_ref, d_ref, z_ref, hf_ref, hb_ref, dtb_ref, a_ref, e32_ref, dsk_ref, gn_ref, o_ref, y_sc):
    dt, la, acum = _ssd_dt(d_ref[...], dtb_ref[...], a_ref[...])
    tot = acum[Q - 1:Q, :]
    excl = acum - la
    lane = lax.broadcasted_iota(jnp.int32, (Q, LANE), 1)
    is_f = lane < C_HEADS
    logdt = jnp.log(dt)
    col = jnp.where(is_f, acum, excl)
    row_t = jnp.where(is_f, acum - logdt, excl + logdt).T
    dec_x = _expand(jnp.where(is_f, jnp.exp(acum), jnp.exp(tot - excl)), e32_ref)
    x = xa_ref[:, 0:C_INNER]
    xb16 = x.astype(BF16)
    ti = lax.broadcasted_iota(jnp.int32, (Q, Q), 0)
    si = lax.broadcasted_iota(jnp.int32, (Q, Q), 1)
    low = si <= ti
    upp = si >= ti
    lane_q = lax.broadcasted_iota(jnp.int32, (Q, LANE), 1)
    gw = C_HPG * C_HD
    for g in range(C_GROUPS):
        bm = xa_ref[:, C_INNER + g * C_STATE:C_INNER + (g + 1) * C_STATE].astype(BF16)
        cm = xa_ref[:, C_INNER + (C_GROUPS + g) * C_STATE:C_INNER + (C_GROUPS + g + 1) * C_STATE].astype(BF16)
        gmat = _mm_nt(cm, bm)
        cs = slice(g * gw, (g + 1) * gw)
        ys = (_mm(cm, hf_ref[0, :, cs].astype(BF16)) * dec_x[:, cs]
              + _mm(cm, hb_ref[0, :, cs].astype(BF16)) * dec_x[:, C_INNER + g * gw:C_INNER + (g + 1) * gw])
        for pr in range(C_HPG // 2):
            xpair = xb16[:, g * gw + pr * LANE:g * gw + (pr + 1) * LANE]
            res = []
            for e in range(2):
                h = g * C_HPG + 2 * pr + e
                hb_ = C_HEADS + h
                m_f = jnp.where(low, jnp.exp(col[:, h:h + 1] - row_t[h:h + 1, :]), 0.0)
                m_b = jnp.where(upp, jnp.exp(row_t[hb_:hb_ + 1, :] - col[:, hb_:hb_ + 1]), 0.0)
                res.append(_mm((gmat * (m_f + m_b)).astype(BF16), xpair))
            y_sc[:, g * gw + pr * LANE:g * gw + (pr + 1) * LANE] = (
                jnp.where(lane_q < C_HD, res[0], res[1]) + ys[:, pr * LANE:(pr + 1) * LANE])
    y = y_sc[...] + dsk_ref[...] * x
    y = y * _silu(z_ref[...])
    y = y * lax.rsqrt(jnp.mean(y * y, axis=-1, keepdims=True) + EPS) * gn_ref[...]
    o_ref[...] = y.astype(BF16)


def _ssd2(xa, proj, hf, hb, dtb, a_row, e32, dsk, gn):
    dcol = OD_DT // LANE
    one = lambda w: pl.BlockSpec((1, w), lambda s: (0, 0))
    return pl.pallas_call(
        _ssd2_kernel,
        grid=(NCH,),
        in_specs=[pl.BlockSpec((Q, C_CONV_DIM), lambda s: (s, 0)),
                  pl.BlockSpec((Q, LANE), lambda s: (s, dcol)),
                  pl.BlockSpec((Q, C_INNER), lambda s: (s, 0)),
                  pl.BlockSpec((1, C_STATE, C_INNER), lambda s: (s, 0, 0)),
                  pl.BlockSpec((1, C_STATE, C_INNER), lambda s: (s, 0, 0)),
                  one(LANE), one(LANE),
                  pl.BlockSpec((LANE, 2 * C_INNER), lambda s: (0, 0)),
                  one(C_INNER), one(C_INNER)],
        out_specs=pl.BlockSpec((Q, C_INNER), lambda s: (s, 0)),
        out_shape=jax.ShapeDtypeStruct((T_ALL, C_INNER), BF16),
        scratch_shapes=[pltpu.VMEM((Q, C_INNER), F32)],
        compiler_params=_cp(("parallel",)),
        name="ssd_out",
    )(xa, proj, proj, hf, hb, dtb, a_row, e32, dsk, gn)


S5_MM_STEPS = 32


def _cmul(ar, ai, br, bi):
    return ar * br - ai * bi, ar * bi + ai * br


def _s5_kernel(u_ref, wx_ref, cw_ref, a_ref, dsk_ref, h0_ref, y_ref, st_ref, xs, *, nseg):
    ct = pl.program_id(0)
    quarter = ct % 4
    nslab = S5_STEPS // S5_MM_STEPS
    rows = S5_MM_STEPS * S5_CHAINS

    def fill(k, carry):
        r0 = pl.multiple_of(k * S5_MM_STEPS, S5_MM_STEPS)
        ub = u_ref[pl.ds(r0, S5_MM_STEPS)].reshape(rows, LANE).astype(BF16)
        xs[pl.ds(r0, S5_MM_STEPS)] = _mm(ub, wx_ref[0]).reshape(S5_MM_STEPS, S5_CHAINS, 4 * LANE)
        return carry
    lax.fori_loop(0, nslab, fill, 0)

    shape = (S5_CHAINS, LANE)
    a = a_ref[0]
    afr, afi, abr, abi = (jnp.broadcast_to(a[k:k + 1, :], shape) for k in range(4))

    def scan(init, store):
        def body(i, carry):
            hfr, hfi, hbr, hbi = carry
            j = S5_STEPS - 1 - i
            pr, pi = _cmul(afr, afi, hfr, hfi)
            nfr = pr + xs[i, :, 0:LANE]
            nfi = pi + xs[i, :, LANE:2 * LANE]
            pr, pi = _cmul(abr, abi, hbr, hbi)
            nbr = pr + xs[j, :, 2 * LANE:3 * LANE]
            nbi = pi + xs[j, :, 3 * LANE:4 * LANE]
            if store:
                xs[i, :, 0:LANE] = nfr
                xs[i, :, LANE:2 * LANE] = nfi
                xs[j, :, 2 * LANE:3 * LANE] = nbr
                xs[j, :, 3 * LANE:4 * LANE] = nbi
            return nfr, nfi, nbr, nbi
        return lax.fori_loop(0, S5_STEPS, body, init)

    zero = jnp.zeros(shape, F32)
    if nseg == 1:
        init = (zero, zero, zero, zero)
    else:
        efr, efi, ebr, ebi = scan((zero, zero, zero, zero), store=False)
        pfr, pfi, pbr, pbi = afr, afi, abr, abi
        for _ in range(int(math.log2(S5_STEPS))):
            pfr, pfi = _cmul(pfr, pfi, pfr, pfi)
            pbr, pbi = _cmul(pbr, pbi, pbr, pbi)
        row = lax.broadcasted_iota(jnp.int32, shape, 0) % nseg
        first = row == 0
        last = row == nseg - 1
        h0 = h0_ref[...]
        cfr, cfi, cbr, cbi = zero, zero, zero, zero
        dn = lambda v: pltpu.roll(v, 1, 0)
        up = lambda v: pltpu.roll(v, S5_CHAINS - 1, 0)
        for _ in range(nseg):
            pr, pi = _cmul(pfr, pfi, dn(cfr), dn(cfi))
            cfr = jnp.where(first, h0[0], pr + dn(efr))
            cfi = jnp.where(first, h0[1], pi + dn(efi))
            pr, pi = _cmul(pbr, pbi, up(cbr), up(cbi))
            cbr = jnp.where(last, h0[2], pr + up(ebr))
            cbi = jnp.where(last, h0[3], pi + up(ebi))
        init = (cfr, cfi, cbr, cbi)
    fin = scan(init, store=True)
    for k in range(4):
        st_ref[k] = fin[k]

    def emit(k, carry):
        r0 = pl.multiple_of(k * S5_MM_STEPS, S5_MM_STEPS)
        blk = xs[pl.ds(r0, S5_MM_STEPS)]
        s_re = blk[:, :, 0:LANE] + blk[:, :, 2 * LANE:3 * LANE]
        s_im = blk[:, :, LANE:2 * LANE] + blk[:, :, 3 * LANE:4 * LANE]
        s2 = jnp.concatenate([s_re, s_im], axis=-1).reshape(rows, 2 * LANE).astype(BF16)
        yv = _mm(s2, cw_ref[0]).reshape(S5_MM_STEPS, S5_CHAINS, LANE)

        @pl.when(quarter == 0)
        def _():
            y_ref[pl.ds(r0, S5_MM_STEPS)] = yv + dsk_ref[...] * u_ref[pl.ds(r0, S5_MM_STEPS)]

        @pl.when(quarter != 0)
        def _():
            y_ref[pl.ds(r0, S5_MM_STEPS)] += yv
        return carry
    lax.fori_loop(0, nslab, emit, 0)


def _s5(u_perm, sw, h0, nseg):
    nct = D_GROUPS // 2
    return pl.pallas_call(
        functools.partial(_s5_kernel, nseg=nseg),
        grid=(nct,),
        in_specs=[pl.BlockSpec((S5_STEPS, S5_CHAINS, LANE), lambda ct: (0, 0, ct // 4)),
                  pl.BlockSpec((1, LANE, 4 * LANE), lambda ct: (ct, 0, 0)),
                  pl.BlockSpec((1, 2 * LANE, LANE), lambda ct: (ct, 0, 0)),
                  pl.BlockSpec((1, 8, LANE), lambda ct: (ct, 0, 0)),
                  pl.BlockSpec((1, LANE), lambda ct: (0, ct // 4)),
                  pl.BlockSpec((4, S5_CHAINS, LANE), lambda ct: (0, 0, ct))],
        out_specs=[pl.BlockSpec((S5_STEPS, S5_CHAINS, LANE), lambda ct: (0, 0, ct // 4)),
                   pl.BlockSpec((4, S5_CHAINS, LANE), lambda ct: (0, 0, ct))],
        out_shape=[jax.ShapeDtypeStruct((S5_STEPS, S5_CHAINS, D_WIDTH), F32),
                   jax.ShapeDtypeStruct((4, S5_CHAINS, nct * LANE), F32)],
        scratch_shapes=[pltpu.VMEM((S5_STEPS, S5_CHAINS, 4 * LANE), F32)],
        compiler_params=_cp(("arbitrary",)),
        name="s5_seg" if nseg > 1 else "s5",
    )(u_perm, sw["wx"], sw["cw"], sw["a"], sw["dsk"], h0)


def _pad_last(a, n):
    return jnp.pad(a, [(0, 0)] * (a.ndim - 1) + [(0, n - a.shape[-1])])


def _even_weights(w_in, q_lora_g, w_uq, kv_lora_g, w_ukv, q_norm_g, k_norm_g, bq_g, bk_g):
    d = w_in.shape[0]
    s0, s1, s2, s3, s4 = 384, 640, 672, 1184, 1312
    kr = jnp.pad(w_in[:, s1:s2], ((0, 0), (A_NOPE, LANE - A_QK)))
    heads = lambda w, n: _pad_last(w.reshape(d, n, B_HD), LANE).reshape(d, n * LANE)
    w_perm = jnp.concatenate([w_in[:, :s0], w_in[:, s0:s1], kr, heads(w_in[:, s2:s3], B_HEADS),
                              heads(w_in[:, s3:s4], B_KV_HEADS), heads(w_in[:, s4:], B_KV_HEADS)], axis=1)
    row = lambda g: _pad_last(g.reshape(1, -1), LANE)
    return {
        "w_in": w_perm.astype(BF16),
        "gql": q_lora_g.reshape(1, -1),
        "wuq": _pad_last(w_uq, LANE).reshape(A_Q_LORA, A_HEADS * LANE).astype(BF16),
        "gqn": row(q_norm_g),
        "gkvl": kv_lora_g.reshape(1, -1),
        "wk": _pad_last(w_ukv[:, :, :A_NOPE], LANE).reshape(A_KV_LORA, A_HEADS * LANE).astype(BF16),
        "wv": _pad_last(w_ukv[:, :, A_NOPE:], LANE).reshape(A_KV_LORA, A_HEADS * LANE).astype(BF16),
        "gkn": row(k_norm_g),
        "gbq": row(bq_g),
        "gbk": row(bk_g),
    }


def _rope_tables():
    rows = DEC_SEQ // GRID_W

    def angles(dim):
        half = dim // 2
        inv = ROPE_THETA ** (-jnp.arange(0, half, 2, dtype=F32) / half)
        r = jnp.repeat(jnp.arange(rows, dtype=F32), GRID_W)
        col = jnp.tile(jnp.arange(GRID_W, dtype=F32), rows)
        return jnp.concatenate([r[:, None] * inv, col[:, None] * inv], axis=-1)

    def lanes(dim, start):
        ang = angles(dim)
        cos = jnp.repeat(jnp.cos(ang), 2, axis=-1)
        sin = jnp.repeat(jnp.sin(ang), 2, axis=-1) * jnp.tile(jnp.array([-1.0, 1.0], F32), dim // 2)
        pad = ((0, 0), (start, LANE - start - dim))
        return jnp.pad(cos, pad, constant_values=1.0), jnp.pad(sin, pad)

    cos_a, sin_a = lanes(A_ROPE, A_NOPE)
    cos_b, sin_b = lanes(B_HD, 0)
    return cos_a, sin_a, cos_b, sin_b


def _odd_weights(w_in, dt_bias, a_log, c_d_skip):
    w_perm = jnp.concatenate([w_in[:, :C_INNER], w_in[:, C_INNER:C_INNER + C_CONV_DIM],
                              w_in[:, C_INNER + C_CONV_DIM + 2 * C_HEADS:],
                              _pad_last(w_in[:, C_INNER + C_CONV_DIM:C_INNER + C_CONV_DIM + 2 * C_HEADS], LANE)], axis=1)
    lane = jnp.arange(LANE)
    col = jnp.arange(2 * C_INNER)
    e32 = ((lane[:, None] < 2 * C_HEADS) & (lane[:, None] == col[None, :] // C_HD)).astype(BF16)
    return {
        "w_in": w_perm.astype(BF16),
        "dtb": _pad_last(dt_bias.reshape(1, -1), LANE),
        "a_row": _pad_last(-jnp.exp(a_log).reshape(1, -1), LANE),
        "e32": e32,
        "dsk": jnp.repeat(c_d_skip, C_HD).reshape(1, C_INNER),
    }


def _s5_weights(lam_re, lam_im, log_dt, b_re, b_im, c_re, c_im, d_skip):
    nct = D_GROUPS // 2
    dt = jnp.exp(log_dt)[..., None]
    mag = jnp.exp(lam_re * dt)
    ar, ai = mag * jnp.cos(lam_im * dt), mag * jnp.sin(lam_im * dt)
    den = lam_re * lam_re + lam_im * lam_im
    kr = ((ar - 1.0) * lam_re + ai * lam_im) / den
    ki = (ai * lam_re - (ar - 1.0) * lam_im) / den
    xr = kr[..., None] * b_re[None] - ki[..., None] * b_im[None]
    xi = kr[..., None] * b_im[None] + ki[..., None] * b_re[None]
    xc = jnp.stack([xr[0], xi[0], xr[1], xi[1]]).reshape(4, nct, 2, D_STATE, D_GROUP_SIZE)
    quarter = (jnp.arange(nct)[:, None] % 4 == jnp.arange(4)[None, :]).astype(F32)
    eye2 = jnp.eye(2, dtype=F32)
    wx = jnp.einsum("tq,gh,ktgnj->tqgjkhn", quarter, eye2, xc).reshape(nct, LANE, 4 * LANE)
    cc = jnp.stack([c_re, -c_im]).reshape(2, nct, 2, D_GROUP_SIZE, D_STATE)
    cw = jnp.einsum("tq,gh,rtgcn->trhnqgc", quarter, eye2, cc).reshape(nct, 2 * LANE, LANE)
    planes = jnp.stack([ar[0], ai[0], ar[1], ai[1]]).reshape(4, nct, LANE).transpose(1, 0, 2)
    return {"wx": wx.astype(BF16), "cw": cw.astype(BF16), "a": jnp.pad(planes, ((0, 0), (0, 4), (0, 0))),
            "dsk": d_skip.reshape(1, D_WIDTH)}


def _even_layer(x, mod, norm1_g, norm2_g, ew, w_out, ffn_w, caches, rope_tabs):
    c_ckv, c_krope, c_kb, c_vb = caches
    proj = _modmm(x, mod, norm1_g, ew["w_in"])
    qa_p, ka_p, va_p, qb_p, kb_p, vb_p, ckv_c, kr_c, kbn_c, vb_c = _evenprep(proj, 0, TP, ew, None)
    qa_s, ka_s, va_s, qb_s, kb_s, vb_s = _evenprep(proj, TP, TS, ew, rope_tabs)
    n_ctx = DEC_BATCH * PAST_LEN
    kr_blk = jnp.pad(c_krope.reshape(n_ctx, A_ROPE), ((0, 0), (A_NOPE, LANE - A_QK)))
    ka_c, va_c = _ctxkv(c_ckv.reshape(n_ctx, A_KV_LORA), kr_blk, ew)
    heads_b = lambda a: _pad_last(a.transpose(0, 2, 1, 3), LANE).reshape(n_ctx, B_KV_HEADS * LANE).astype(BF16)
    sa, sb = A_QK ** -0.5, B_HD ** -0.5
    oa_p = _attn(qa_p, ka_p, va_p, None, nb=BATCH, lq=SEQ, lk=SEQ, nheads=A_HEADS, hp=2, shared_kv=False,
                 scale=sa, tq=SEQ)
    ob_p = _attn(qb_p, kb_p, vb_p, None, nb=BATCH, lq=SEQ, lk=SEQ, nheads=B_HEADS, hp=4, shared_kv=True,
                 scale=sb, tq=SEQ)
    oa_s = _attn(qa_s, ka_s, va_s, (ka_c, va_c), nb=DEC_BATCH, lq=DEC_SEQ, lk=DEC_SEQ, nheads=A_HEADS, hp=2,
                 shared_kv=False, scale=sa, tq=ATT_TQ)
    ob_s = _attn(qb_s, kb_s, vb_s, (heads_b(c_kb), heads_b(c_vb)), nb=DEC_BATCH, lq=DEC_SEQ, lk=DEC_SEQ,
                 nheads=B_HEADS, hp=4, shared_kv=True, scale=sb, tq=ATT_TQ)
    oa = jnp.concatenate([oa_p, oa_s], axis=0)
    ob = jnp.concatenate([ob_p, ob_s], axis=0)
    x = _outproj(x, mod, oa, ob, w_out[:A_HEADS * A_V], w_out[A_HEADS * A_V:])
    x = _ffn(x, mod, norm2_g, *ffn_w)
    unhead = lambda a: a.reshape(BATCH, SEQ, B_KV_HEADS, LANE)[..., :B_HD].transpose(0, 2, 1, 3)
    new_cache = (ckv_c.reshape(BATCH, SEQ, A_KV_LORA), kr_c[:, A_NOPE:A_QK].reshape(BATCH, SEQ, A_ROPE),
                 unhead(kbn_c), unhead(vb_c))
    return x, new_cache


def _odd_layer(x, mod, norm1_g, norm2_g, ow, sw, conv_w, conv_b, c_norm_g, w_glu, w_out, moe_w, states):
    st_c, st_re, st_im = states
    proj = _modmm(x, mod, norm1_g, ow["w_in"])
    xa = _conv(proj, conv_w, conv_b)
    h0 = st_c.transpose(0, 1, 4, 2, 3).reshape(DEC_BATCH, 2, C_STATE, C_INNER)
    h0 = jnp.concatenate([jnp.zeros((1,) + h0.shape[1:], F32), h0], axis=0)
    hf, hb, fin = _ssd1(xa, proj, ow["dtb"], ow["a_row"], ow["e32"], h0)
    yc = _ssd2(xa, proj, hf, hb, ow["dtb"], ow["a_row"], ow["e32"], ow["dsk"], c_norm_g.reshape(1, C_INNER))
    u = proj[:, OD_U:OD_U + D_WIDTH]
    u_p = u[:TP].reshape(BATCH, SEQ, D_WIDTH).transpose(1, 0, 2)
    u_s = u[TP:].reshape(DEC_BATCH * S5_SEG, S5_STEPS, D_WIDTH).transpose(1, 0, 2)
    zeros_h0 = jnp.zeros((4, S5_CHAINS, D_GROUPS * D_STATE), F32)
    h0_s = jnp.stack([st_re[:, 0], st_im[:, 0], st_re[:, 1], st_im[:, 1]]).reshape(4, DEC_BATCH, D_GROUPS * D_STATE)
    h0_s = jnp.repeat(h0_s, S5_SEG, axis=1)
    yd_p, fin_d = _s5(u_p, sw, zeros_h0, 1)
    yd_s, _ = _s5(u_s, sw, h0_s, S5_SEG)
    yd = jnp.concatenate([yd_p.transpose(1, 0, 2).reshape(TP, D_WIDTH),
                          yd_s.transpose(1, 0, 2).reshape(TS, D_WIDTH)], axis=0)
    x = _outproj(x, mod, yc, yd, w_out[:C_INNER], w_out[C_INNER:], wg=w_glu)
    x = _ffn(x, mod, norm2_g, *moe_w[:3], router=moe_w[3])
    new_c = fin[:BATCH].reshape(BATCH, 2, C_HEADS, C_HD, C_STATE)
    fin_d = fin_d.reshape(4, BATCH, D_GROUPS, D_STATE)
    new_re = jnp.stack([fin_d[0], fin_d[2]], axis=1)
    new_im = jnp.stack([fin_d[1], fin_d[3]], axis=1)
    return x, (new_c, new_re, new_im)


def kernel(x_prompt, x_sample, cache_a_ckv, cache_a_krope, cache_b_k, cache_b_v, state_c_ssm, state_d_re, state_d_im, c, c_ctx, w_mod, b_mod, norm1_g, norm2_g, ev_w_in, ev_a_q_lora_g, ev_a_w_uq, ev_a_kv_lora_g, ev_a_w_ukv, ev_a_q_norm_g, ev_a_k_norm_g, ev_b_q_norm_g, ev_b_k_norm_g, ev_w_out, ffn_w1, ffn_w3, ffn_w2, od_w_in, od_conv_w, od_conv_b, od_dt_bias, od_a_log, od_c_d_skip, od_c_norm_g, od_lam_re, od_lam_im, od_log_dt, od_b_re, od_b_im, od_c_re, od_c_im, od_d_skip, od_w_glu, od_w_out, moe_router_w, moe_router_b, moe_w1, moe_w3, moe_w2):
    x = jnp.concatenate([x_prompt.reshape(TP, D_MODEL), x_sample.reshape(TS, D_MODEL)], axis=0)
    cond8 = jnp.zeros((8, D_MODEL), F32).at[0].set(c_ctx).at[1:1 + DEC_BATCH].set(c)
    mods = _adaln(cond8, w_mod, b_mod)[:, :1 + DEC_BATCH].reshape(DEPTH, 1 + DEC_BATCH, 6, D_MODEL)
    mods = jnp.pad(mods, ((0, 0), (0, 0), (0, 2), (0, 0)))
    rope_tabs = _rope_tables()
    ev_new, od_new = [], []
    for layer in range(DEPTH):
        mod = mods[layer]
        if layer % 2 == 0:
            e = layer // 2
            ew = _even_weights(ev_w_in[e], ev_a_q_lora_g[e], ev_a_w_uq[e], ev_a_kv_lora_g[e], ev_a_w_ukv[e],
                               ev_a_q_norm_g[e], ev_a_k_norm_g[e], ev_b_q_norm_g[e], ev_b_k_norm_g[e])
            ffn_w = (ffn_w1[e].astype(BF16), ffn_w3[e].astype(BF16), ffn_w2[e].astype(BF16))
            x, new = _even_layer(x, mod, norm1_g[layer], norm2_g[layer], ew, ev_w_out[e].astype(BF16), ffn_w,
                                 (cache_a_ckv[:, e], cache_a_krope[:, e], cache_b_k[:, e], cache_b_v[:, e]),
                                 rope_tabs)
            ev_new.append(new)
        else:
            o = layer // 2
            ow = _odd_weights(od_w_in[o], od_dt_bias[o], od_a_log[o], od_c_d_skip[o])
            sw = _s5_weights(od_lam_re[o], od_lam_im[o], od_log_dt[o], od_b_re[o], od_b_im[o], od_c_re[o],
                             od_c_im[o], od_d_skip[o])
            rw_hi = _pad_last(moe_router_w[o], LANE).astype(BF16)
            rw_lo = (_pad_last(moe_router_w[o], LANE) - rw_hi.astype(F32)).astype(BF16)
            moe_w = (moe_w1[o].astype(BF16), moe_w3[o].astype(BF16), moe_w2[o].astype(BF16),
                     (rw_hi, rw_lo, _pad_last(moe_router_b[o].reshape(1, -1), LANE)))
            x, new = _odd_layer(x, mod, norm1_g[layer], norm2_g[layer], ow, sw, od_conv_w[o], od_conv_b[o],
                                od_c_norm_g[o], od_w_glu[o].astype(BF16), od_w_out[o].astype(BF16), moe_w,
                                (state_c_ssm[:, o], state_d_re[:, o], state_d_im[:, o]))
            od_new.append(new)
    stack = lambda items, k: jnp.stack([it[k] for it in items], axis=1)
    return (x[:TP].reshape(BATCH, SEQ, D_MODEL), x[TP:].reshape(DEC_BATCH, DEC_SEQ, D_MODEL),
            stack(ev_new, 0), stack(ev_new, 1), stack(ev_new, 2), stack(ev_new, 3),
            stack(od_new, 0), stack(od_new, 1), stack(od_new, 2))
```

```python
import functools
import math

import jax
import jax.numpy as jnp
from jax import lax
from jax.experimental import pallas as pl
from jax.experimental.pallas import tpu as pltpu

F32 = jnp.float32
BF16 = jnp.bfloat16

D_MODEL = 1024
BATCH = 32
SEQ = 256
DEPTH = 4
DEC_BATCH = 2
DEC_SEQ = 4096
PAST_LEN = 256
GRID_W = 64
ROPE_THETA = 10000.0
EPS = 1e-6
A_HEADS = 8
A_NOPE = 64
A_ROPE = 32
A_QK = A_NOPE + A_ROPE
A_V = 64
A_Q_LORA = 384
A_KV_LORA = 256
B_HEADS = 8
B_KV_HEADS = 2
B_HD = 64
C_HEADS = 16
C_HD = 64
C_GROUPS = 2
C_HPG = C_HEADS // C_GROUPS
C_STATE = 128
C_INNER = C_HEADS * C_HD
C_CONV_DIM = C_INNER + 2 * C_GROUPS * C_STATE
SSD_CHUNK = 128
D_GROUP_SIZE = 16
D_GROUPS = 32
D_WIDTH = D_GROUPS * D_GROUP_SIZE
D_STATE = 64
FF_DENSE = 2816
N_EXPERTS = 8
FF_EXPERT = 1408

TP = BATCH * SEQ
TS = DEC_BATCH * DEC_SEQ
T_ALL = TP + TS
LANE = 128
LOG2E = 1.4426950408889634
VMEM_LIMIT = 56 * 1024 * 1024

EV_CQ = 0
EV_CKV = 384
EV_KR = 640
EV_QB = 768
EV_KB = EV_QB + B_HEADS * LANE
EV_VB = EV_KB + B_KV_HEADS * LANE
EV_N = EV_VB + B_KV_HEADS * LANE
OD_Z = 0
OD_XBC = C_INNER
OD_U = OD_XBC + C_CONV_DIM
OD_DT = OD_U + D_WIDTH
OD_N = OD_DT + LANE
S5_STEPS = 256
S5_CHAINS = 32
S5_SEG = DEC_SEQ // S5_STEPS
ATT_TQ = 256


def _cp(sem):
    return pltpu.CompilerParams(dimension_semantics=sem, vmem_limit_bytes=VMEM_LIMIT)


def _cond_idx(i, tm):
    t0 = i * tm
    return jnp.where(t0 < TP, 0, 1 + (t0 - TP) // DEC_SEQ)


def _mm(a, b):
    return jnp.dot(a, b, preferred_element_type=F32)


def _mm_nt(a, b):
    return lax.dot_general(a, b, (((1,), (1,)), ((), ())), preferred_element_type=F32)


def _split_bf16(v):
    hi = v.astype(BF16)
    lo = (v - hi.astype(F32)).astype(BF16)
    return hi, lo


def _silu(v):
    return v * jax.nn.sigmoid(v)


def _rms(v, n):
    return v * lax.rsqrt(jnp.sum(v * v, axis=-1, keepdims=True) * (1.0 / n) + EPS)


def _adaln_kernel(c_ref, w_ref, b_ref, o_ref):
    s = _silu(c_ref[...])
    o_ref[0] = _mm(s.astype(BF16), w_ref[0].astype(BF16)) + b_ref[0]


def _adaln(cond8, w_mod, b_mod):
    depth, d, n6 = w_mod.shape
    tn = 1536
    return pl.pallas_call(
        _adaln_kernel,
        grid=(depth, n6 // tn),
        in_specs=[pl.BlockSpec((8, d), lambda l, j: (0, 0)),
                  pl.BlockSpec((1, d, tn), lambda l, j: (l, 0, j)),
                  pl.BlockSpec((1, 1, tn), lambda l, j: (l, 0, j))],
        out_specs=pl.BlockSpec((1, 8, tn), lambda l, j: (l, 0, j)),
        out_shape=jax.ShapeDtypeStruct((depth, 8, n6), F32),
        compiler_params=_cp(("parallel", "parallel")),
        name="adaln",
    )(cond8, w_mod, b_mod.reshape(depth, 1, n6))


def _modulate(x, g, mod_ref, shift_row, scale_row):
    y = x * lax.rsqrt(jnp.mean(x * x, axis=-1, keepdims=True) + EPS) * g
    return y * (1.0 + mod_ref[0, scale_row:scale_row + 1, :]) + mod_ref[0, shift_row:shift_row + 1, :]


def _modmm_kernel(x_ref, mod_ref, g_ref, w_ref, o_ref):
    h = _modulate(x_ref[...], g_ref[...], mod_ref, 0, 1)
    o_ref[...] = _mm(h.astype(BF16), w_ref[...])


def _modmm(x, mod, g, w, tm=512):
    t, d = x.shape
    n = w.shape[1]
    return pl.pallas_call(
        _modmm_kernel,
        grid=(t // tm,),
        in_specs=[pl.BlockSpec((tm, d), lambda i: (i, 0)),
                  pl.BlockSpec((1, 8, d), lambda i: (_cond_idx(i, tm), 0, 0)),
                  pl.BlockSpec((1, d), lambda i: (0, 0)),
                  pl.BlockSpec((d, n), lambda i: (0, 0))],
        out_specs=pl.BlockSpec((tm, n), lambda i: (i, 0)),
        out_shape=jax.ShapeDtypeStruct((t, n), F32),
        compiler_params=_cp(("parallel",)),
        name="modmm",
    )(x, mod, g.reshape(1, d), w)


def _swap_pairs(v):
    lane = lax.broadcasted_iota(jnp.int32, v.shape, 1)
    nxt = pltpu.roll(v, LANE - 1, 1)
    prv = pltpu.roll(v, 1, 1)
    return jnp.where((lane & 1) == 0, nxt, prv)


def _rope(v, cos, sin):
    return v * cos + _swap_pairs(v) * sin


def _mla_kv(ckvn, kr_blk, wk_ref, wv_ref, gkn, rope_tabs, ka_o, va_o):
    cb = ckvn.astype(BF16)
    kn = _mm(cb, wk_ref[...])
    va_o[...] = _mm(cb, wv_ref[...]).astype(BF16)
    for h in range(A_HEADS):
        sl = slice(h * LANE, (h + 1) * LANE)
        kh = _rms(kn[:, sl] + kr_blk, A_QK) * gkn
        if rope_tabs is not None:
            kh = _rope(kh, rope_tabs[0], rope_tabs[1])
        ka_o[:, sl] = kh.astype(BF16)


def _evenprep_kernel(*refs, rope):
    it = iter(refs)
    proj = next(it)
    if rope:
        cos_a, sin_a, cos_b, sin_b = (next(it)[...] for _ in range(4))
    gql, wuq, gqn, gkvl, wk, wv, gkn, gbq, gbk = (next(it) for _ in range(9))
    qa_o, ka_o, va_o, qb_o, kb_o, vb_o = (next(it) for _ in range(6))
    if not rope:
        ckv_o, kr_o, kbn_o, vbf_o = (next(it) for _ in range(4))

    cqn = _rms(proj[:, EV_CQ:EV_CQ + A_Q_LORA], A_Q_LORA) * gql[...]
    q = _mm(cqn.astype(BF16), wuq[...])
    for h in range(A_HEADS):
        sl = slice(h * LANE, (h + 1) * LANE)
        qh = _rms(q[:, sl], A_QK) * gqn[...]
        if rope:
            qh = _rope(qh, cos_a, sin_a)
        qa_o[:, sl] = qh.astype(BF16)

    ckvn = _rms(proj[:, EV_CKV:EV_CKV + A_KV_LORA], A_KV_LORA) * gkvl[...]
    kr_blk = proj[:, EV_KR:EV_KR + LANE]
    _mla_kv(ckvn, kr_blk, wk, wv, gkn[...], (cos_a, sin_a) if rope else None, ka_o, va_o)

    for h in range(B_HEADS):
        sl = slice(h * LANE, (h + 1) * LANE)
        qh = _rms(proj[:, EV_QB + h * LANE:EV_QB + (h + 1) * LANE], B_HD) * gbq[...]
        if rope:
            qh = _rope(qh, cos_b, sin_b)
        qb_o[:, sl] = qh.astype(BF16)
    for j in range(B_KV_HEADS):
        sl = slice(j * LANE, (j + 1) * LANE)
        kh = _rms(proj[:, EV_KB + j * LANE:EV_KB + (j + 1) * LANE], B_HD) * gbk[...]
        if not rope:
            kbn_o[0, j] = kh[:, :B_HD]
        else:
            kh = _rope(kh, cos_b, sin_b)
        kb_o[:, sl] = kh.astype(BF16)
    vb = proj[:, EV_VB:EV_VB + B_KV_HEADS * LANE]
    vb_o[...] = vb.astype(BF16)
    if not rope:
        for j in range(B_KV_HEADS):
            vbf_o[0, j] = vb[:, j * LANE:j * LANE + B_HD]
        ckv_o[...] = ckvn
        kr_o[...] = kr_blk


def _evenprep(proj, row_off, nrows, ew, rope_tabs):
    tm = 256
    rope = rope_tabs is not None
    off = row_off // tm
    full = lambda a: pl.BlockSpec(a.shape, lambda i: (0,) * a.ndim)
    rows = lambda w: pl.BlockSpec((tm, w), lambda i: (i, 0))
    ins = [proj]
    in_specs = [pl.BlockSpec((tm, EV_N), lambda i: (i + off, 0))]
    if rope:
        ins += list(rope_tabs)
        in_specs += [pl.BlockSpec((tm, LANE), lambda i: (i % (DEC_SEQ // tm), 0))] * 4
    ws = [ew["gql"], ew["wuq"], ew["gqn"], ew["gkvl"], ew["wk"], ew["wv"], ew["gkn"], ew["gbq"], ew["gbk"]]
    ins += ws
    in_specs += [full(a) for a in ws]
    hw = A_HEADS * LANE
    kvw = B_KV_HEADS * LANE
    out_shape = [jax.ShapeDtypeStruct((nrows, hw), BF16)] * 4 + [jax.ShapeDtypeStruct((nrows, kvw), BF16)] * 2
    out_specs = [rows(hw)] * 4 + [rows(kvw)] * 2
    if not rope:
        assert tm == SEQ
        cache_b = jax.ShapeDtypeStruct((nrows // SEQ, B_KV_HEADS, SEQ, B_HD), F32)
        cache_b_spec = pl.BlockSpec((1, B_KV_HEADS, SEQ, B_HD), lambda i: (i, 0, 0, 0))
        out_shape += [jax.ShapeDtypeStruct((nrows, A_KV_LORA), F32), jax.ShapeDtypeStruct((nrows, LANE), F32),
                      cache_b, cache_b]
        out_specs += [rows(A_KV_LORA), rows(LANE), cache_b_spec, cache_b_spec]
    return pl.pallas_call(
        functools.partial(_evenprep_kernel, rope=rope),
        grid=(nrows // tm,),
        in_specs=in_specs,
        out_specs=out_specs,
        out_shape=out_shape,
        compiler_params=_cp(("parallel",)),
        name="evenprep_rope" if rope else "evenprep",
    )(*ins)


def _ctxkv_kernel(ckv_ref, kr_ref, wk, wv, gkn, ka_o, va_o):
    _mla_kv(ckv_ref[...], kr_ref[...], wk, wv, gkn[...], None, ka_o, va_o)


def _ctxkv(ckvn, kr_blk, ew):
    n = ckvn.shape[0]
    hw = A_HEADS * LANE
    return pl.pallas_call(
        _ctxkv_kernel,
        out_shape=[jax.ShapeDtypeStruct((n, hw), BF16)] * 2,
        compiler_params=pltpu.CompilerParams(vmem_limit_bytes=VMEM_LIMIT),
        name="ctxkv",
    )(ckvn, kr_blk, ew["wk"], ew["wv"], ew["gkn"])


def _attn_kernel(*refs, hp, shared_kv, c, has_ctx):
    if has_ctx:
        q_ref, kn_ref, vn_ref, kc_ref, vc_ref, _, o_ref = refs
    else:
        q_ref, kn_ref, vn_ref, o_ref = refs
    outs = []
    for h in range(hp):
        kv = slice(0, LANE) if shared_kv else slice(h * LANE, (h + 1) * LANE)
        q = q_ref[:, h * LANE:(h + 1) * LANE]
        s_n = _mm_nt(q, kn_ref[:, kv])
        m = jnp.max(s_n, axis=-1, keepdims=True)
        if has_ctx:
            s_c = _mm_nt(q, kc_ref[:, kv])
            m = jnp.maximum(m, jnp.max(s_c, axis=-1, keepdims=True))
        p_n = jnp.exp2((s_n - m) * c)
        l = jnp.sum(p_n, axis=-1, keepdims=True)
        acc = _mm(p_n.astype(BF16), vn_ref[:, kv])
        if has_ctx:
            p_c = jnp.exp2((s_c - m) * c)
            l = l + jnp.sum(p_c, axis=-1, keepdims=True)
            acc = acc + _mm(p_c.astype(BF16), vc_ref[:, kv])
        outs.append(acc * (1.0 / l))
    for p2 in range(hp // 2):
        pair = jnp.concatenate([outs[2 * p2][:, :A_V], outs[2 * p2 + 1][:, :A_V]], axis=-1)
        o_ref[:, p2 * LANE:(p2 + 1) * LANE] = pair.astype(BF16)


def _attn(q, kn, vn, ctx, *, nb, lq, lk, nheads, hp, shared_kv, scale, tq):
    ngroups = nheads // hp
    kvw = LANE if shared_kv else hp * LANE
    nq = lq // tq
    ins = [q, kn, vn]
    in_specs = [pl.BlockSpec((tq, hp * LANE), lambda b, g, i: (b * nq + i, g)),
                pl.BlockSpec((lk, kvw), lambda b, g, i: (b, g)),
                pl.BlockSpec((lk, kvw), lambda b, g, i: (b, g))]
    row0 = 0
    aliases = {}
    if ctx is not None:
        ins += list(ctx)
        in_specs += [pl.BlockSpec((PAST_LEN, kvw), lambda b, g, i: (b, g))] * 2 + [pl.BlockSpec(memory_space=pl.ANY)]
        row0 = TP // tq
        aliases = {len(ins) - 1: 0}
    return pl.pallas_call(
        functools.partial(_attn_kernel, hp=hp, shared_kv=shared_kv, c=scale * LOG2E, has_ctx=ctx is not None),
        grid=(nb, ngroups, nq),
        in_specs=in_specs,
        out_specs=pl.BlockSpec((tq, hp * A_V), lambda b, g, i: (row0 + b * nq + i, g)),
        out_shape=jax.ShapeDtypeStruct((T_ALL, nheads * A_V), BF16),
        input_output_aliases=aliases,
        compiler_params=_cp(("parallel", "parallel", "arbitrary")),
        name="attn_ctx" if ctx is not None else "attn",
    )(*ins)


def _outproj_kernel(*refs, glu):
    if glu:
        x_ref, mod_ref, a_ref, b_ref, wa_ref, wb_ref, wg_ref, o_ref = refs
        yd = jax.nn.gelu(b_ref[...])
        b = (yd * jax.nn.sigmoid(_mm(yd.astype(BF16), wg_ref[...]))).astype(BF16)
    else:
        x_ref, mod_ref, a_ref, b_ref, wa_ref, wb_ref, o_ref = refs
        b = b_ref[...]
    acc = _mm(a_ref[...], wa_ref[...]) + _mm(b, wb_ref[...])
    o_ref[...] = x_ref[...] + mod_ref[0, 2:3, :] * acc


def _outproj(x, mod, a, b, wa, wb, wg=None, tm=512):
    t, d = x.shape
    full = lambda v: pl.BlockSpec(v.shape, lambda i: (0,) * v.ndim)
    rows = lambda v: pl.BlockSpec((tm, v.shape[1]), lambda i: (i, 0))
    ins = [x, mod, a, b, wa, wb]
    in_specs = [rows(x), pl.BlockSpec((1, 8, d), lambda i: (_cond_idx(i, tm), 0, 0)), rows(a), rows(b),
                full(wa), full(wb)]
    if wg is not None:
        ins.append(wg)
        in_specs.append(full(wg))
    return pl.pallas_call(
        functools.partial(_outproj_kernel, glu=wg is not None),
        grid=(t // tm,),
        in_specs=in_specs,
        out_specs=rows(x),
        out_shape=jax.ShapeDtypeStruct((t, d), F32),
        compiler_params=_cp(("parallel",)),
        name="outproj_glu" if wg is not None else "outproj",
    )(*ins)


FF_SPLIT = (0, 768, FF_EXPERT)


MOE_BLK = 256


def _swiglu(h, w1, w3, w2):
    y = None
    for k in range(len(FF_SPLIT) - 1):
        sl = slice(FF_SPLIT[k], FF_SPLIT[k + 1])
        act = (_silu(_mm(h, w1[:, sl])) * _mm(h, w3[:, sl])).astype(BF16)
        part = _mm(act, w2[sl, :])
        y = part if y is None else y + part
    return y


def _ffn_kernel(*refs, routed, nj):
    if routed:
        (x_ref, mod_ref, g_ref, rwh_ref, rwl_ref, rb_ref, w1_ref, w3_ref, w2_ref, o_ref,
         h_sc, acc_sc, comb_sc, rank_sc, rank_t_sc, cnt_sc) = refs
    else:
        x_ref, mod_ref, g_ref, w1_ref, w3_ref, w2_ref, o_ref, h_sc, acc_sc = refs
    j = pl.program_id(1)
    tm = x_ref.shape[0]

    @pl.when(j == 0)
    def _():
        h = _modulate(x_ref[...], g_ref[...], mod_ref, 3, 4)
        h_hi = h.astype(BF16)
        h_sc[...] = h_hi
        acc_sc[...] = jnp.zeros(acc_sc.shape, F32)
        if routed:
            h_lo = (h - h_hi.astype(F32)).astype(BF16)
            logits = _mm(h_hi, rwh_ref[...]) + _mm(h_lo, rwh_ref[...]) + _mm(h_hi, rwl_ref[...]) + rb_ref[...]
            lane = lax.broadcasted_iota(jnp.int32, logits.shape, 1)
            lg = jnp.where(lane < N_EXPERTS, logits, -jnp.inf)
            m1 = jnp.max(lg, axis=-1, keepdims=True)
            i1 = jnp.min(jnp.where(lg == m1, lane, LANE), axis=-1, keepdims=True)
            lg2 = jnp.where(lane == i1, -jnp.inf, lg)
            m2 = jnp.max(lg2, axis=-1, keepdims=True)
            i2 = jnp.min(jnp.where(lg2 == m2, lane, LANE), axis=-1, keepdims=True)
            e2 = jnp.exp(m2 - m1)
            inv = 1.0 / (1.0 + e2)
            comb_sc[...] = jnp.where(lane == i1, inv, 0.0) + jnp.where(lane == i2, e2 * inv, 0.0)
            sel = jnp.where(lane == i1, 1.0, 0.0) + jnp.where(lane == i2, 1.0, 0.0)
            ti = lax.broadcasted_iota(jnp.int32, (tm, tm), 0)
            si = lax.broadcasted_iota(jnp.int32, (tm, tm), 1)
            before = _mm((si < ti).astype(BF16), sel.astype(BF16))
            rank = jnp.where(sel > 0.0, before, -1.0)
            rank_sc[...] = rank
            cnt_sc[...] = jnp.broadcast_to(before[tm - 1:tm, :] + sel[tm - 1:tm, :], cnt_sc.shape)
            for k in range(tm // LANE):
                rank_t_sc[:, k * LANE:(k + 1) * LANE] = rank[k * LANE:(k + 1) * LANE, :].T

    if routed:
        lane = lax.broadcasted_iota(jnp.int32, comb_sc.shape, 1)
        lane8 = lax.broadcasted_iota(jnp.int32, cnt_sc.shape, 1)
        n = jnp.sum(jnp.where(lane8 == j, cnt_sc[...], 0.0)[0:1, :]).astype(jnp.int32)
        nblk = lax.shift_right_logical(n + (MOE_BLK - 1), int(math.log2(MOE_BLK)))
        gate = jnp.sum(jnp.where(lane == j, comb_sc[...], 0.0), axis=-1, keepdims=True)
        rank_col = jnp.sum(jnp.where(lane == j, rank_sc[...], 0.0), axis=-1, keepdims=True)
        rank_row = rank_t_sc[pl.ds(j, 1), :]
        r_iota = lax.broadcasted_iota(jnp.int32, (MOE_BLK, tm), 0).astype(F32)
        c_iota = lax.broadcasted_iota(jnp.int32, (tm, MOE_BLK), 1).astype(F32)

        def block(b, carry):
            r0 = (b * MOE_BLK).astype(F32)
            pick = jnp.where(rank_row - r0 == r_iota, 1.0, 0.0).astype(BF16)
            hb = _mm(pick, h_sc[...]).astype(BF16)
            yb = _swiglu(hb, w1_ref.at[0, 0], w3_ref.at[0, 0], w2_ref.at[0, 0]).astype(BF16)
            place = jnp.where(rank_col - r0 == c_iota, 1.0, 0.0).astype(BF16)
            acc_sc[...] += gate * _mm(place, yb)
            return carry
        lax.fori_loop(0, nblk, block, 0)
    else:
        acc_sc[...] += _swiglu(h_sc[...], w1_ref.at[0], w3_ref.at[0], w2_ref.at[0])

    @pl.when(j == nj - 1)
    def _():
        o_ref[...] = x_ref[...] + mod_ref[0, 5:6, :] * acc_sc[...]


def _ffn(x, mod, g, w1, w3, w2, li, router=None, tm=1024):
    t, d = x.shape
    routed = router is not None
    tf = FF_EXPERT
    nj = w1.shape[1] if routed else w1.shape[2] // tf
    ins = [x, mod, g.reshape(1, d)]
    in_specs = [pl.BlockSpec((tm, d), lambda i, j: (i, 0)),
                pl.BlockSpec((1, 8, d), lambda i, j: (_cond_idx(i, tm), 0, 0)),
                pl.BlockSpec((1, d), lambda i, j: (0, 0))]
    scratch = [pltpu.VMEM((tm, d), BF16), pltpu.VMEM((tm, d), F32)]
    if routed:
        ins += list(router)
        in_specs += [pl.BlockSpec((d, LANE), lambda i, j: (0, 0))] * 2 + [pl.BlockSpec((1, LANE), lambda i, j: (0, 0))]
        in_specs += ([pl.BlockSpec((1, 1, d, tf), lambda i, j: (li, j, 0, 0))] * 2
                     + [pl.BlockSpec((1, 1, tf, d), lambda i, j: (li, j, 0, 0))])
        scratch += [pltpu.VMEM((tm, LANE), F32), pltpu.VMEM((tm, LANE), F32), pltpu.VMEM((LANE, tm), F32),
                    pltpu.VMEM((8, LANE), F32)]
    else:
        in_specs += ([pl.BlockSpec((1, d, tf), lambda i, j: (li, 0, j))] * 2
                     + [pl.BlockSpec((1, tf, d), lambda i, j: (li, j, 0))])
    ins += [w1, w3, w2]
    return pl.pallas_call(
        functools.partial(_ffn_kernel, routed=routed, nj=nj),
        grid=(t // tm, nj),
        in_specs=in_specs,
        out_specs=pl.BlockSpec((tm, d), lambda i, j: (i, 0)),
        out_shape=jax.ShapeDtypeStruct((t, d), F32),
        scratch_shapes=scratch,
        compiler_params=_cp(("parallel", "arbitrary")),
        name="moe" if routed else "ffn",
    )(*ins)


CONV_ROWS = 4096


def _conv_kernel(x_ref, w_ref, b_ref, o_ref):
    i = pl.program_id(0)
    seq = jnp.where(i < TP // CONV_ROWS, SEQ, DEC_SEQ)
    x = x_ref[...]
    pos = lax.broadcasted_iota(jnp.int32, x.shape, 0) & (seq - 1)
    prev = jnp.where(pos == 0, 0.0, pltpu.roll(x, 1, 0))
    nxt = jnp.where(pos == seq - 1, 0.0, pltpu.roll(x, CONV_ROWS - 1, 0))
    y = prev * w_ref[0:1, :] + x * w_ref[1:2, :] + nxt * w_ref[2:3, :] + b_ref[...]
    o_ref[...] = _silu(y)


def _conv(proj, w, b):
    tc = 256
    t = proj.shape[0]
    return pl.pallas_call(
        _conv_kernel,
        grid=(t // CONV_ROWS, C_CONV_DIM // tc),
        in_specs=[pl.BlockSpec((CONV_ROWS, tc), lambda i, j: (i, j + OD_XBC // tc)),
                  pl.BlockSpec((3, tc), lambda i, j: (0, j)),
                  pl.BlockSpec((1, tc), lambda i, j: (0, j))],
        out_specs=pl.BlockSpec((CONV_ROWS, tc), lambda i, j: (i, j)),
        out_shape=jax.ShapeDtypeStruct((t, C_CONV_DIM), F32),
        compiler_params=_cp(("parallel", "parallel")),
        name="conv",
    )(proj, w, b.reshape(1, C_CONV_DIM))


Q = SSD_CHUNK
NCH = T_ALL // Q
NCH_P = TP // Q
NC_P = SEQ // Q
NC_S = DEC_SEQ // Q
NSEQ = BATCH + DEC_BATCH


def _chunk_seq(s):
    is_p = s < NCH_P
    nc = jnp.where(is_p, NC_P, NC_S)
    seq = jnp.where(is_p, s // NC_P, BATCH + (s - NCH_P) // NC_S)
    c = jnp.where(is_p, s % NC_P, (s - NCH_P) % NC_S)
    return seq, c, nc, s - c


def _chunk_mirror(s):
    _, c, nc, start = _chunk_seq(s)
    return start + nc - 1 - c


def _ssd_dt(dt_raw, dt_bias, a_row):
    z = dt_raw + dt_bias
    dt = jnp.maximum(z, 0.0) + jnp.log(1.0 + jnp.exp(-jnp.abs(z)))
    la = dt * a_row
    r = lax.broadcasted_iota(jnp.int32, (Q, Q), 0)
    cc = lax.broadcasted_iota(jnp.int32, (Q, Q), 1)
    tri = (cc <= r).astype(BF16)
    hi, lo = _split_bf16(la)
    return dt, la, _mm(tri, hi) + _mm(tri, lo)


def _expand(v, e32_ref):
    hi, lo = _split_bf16(v)
    return _mm(hi, e32_ref[...]) + _mm(lo, e32_ref[...])


def _ssd1_kernel(xf_ref, xb_ref, df_ref, db_ref, dtb_ref, a_ref, e32_ref, h0_ref, hf_o, hb_o, fin_o, hf_sc, hb_sc):
    s = pl.program_id(0)
    _, c, nc, _ = _chunk_seq(s)

    @pl.when(c == 0)
    def _():
        hf_sc[...] = h0_ref[0, 0]
        hb_sc[...] = h0_ref[0, 1]

    hf_o[0] = hf_sc[...].astype(BF16)
    hb_o[0] = hb_sc[...].astype(BF16)

    lane = lax.broadcasted_iota(jnp.int32, (Q, LANE), 1)
    is_f = lane < C_HEADS
    dt_f, _, ac_f = _ssd_dt(df_ref[...], dtb_ref[...], a_ref[...])
    tot_f = ac_f[Q - 1:Q, :]
    dt_b, la_b, ac_b = _ssd_dt(db_ref[...], dtb_ref[...], a_ref[...])
    tot_b = ac_b[Q - 1:Q, :]
    w = jnp.where(is_f, jnp.exp(tot_f - ac_f) * dt_f, jnp.exp(ac_b - la_b) * dt_b)
    is_f8 = lax.broadcasted_iota(jnp.int32, (8, LANE), 1) < C_HEADS
    dec = jnp.where(is_f8, jnp.exp(jnp.broadcast_to(tot_f, (8, LANE))), jnp.exp(jnp.broadcast_to(tot_b, (8, LANE))))
    w_x = _expand(w, e32_ref)
    dec_x = _expand(dec, e32_ref)[0:1, :]
    for d, (x_ref, st) in enumerate(((xf_ref, hf_sc), (xb_ref, hb_sc))):
        xw = (x_ref[:, 0:C_INNER] * w_x[:, d * C_INNER:(d + 1) * C_INNER]).astype(BF16)
        for g in range(C_GROUPS):
            bm_t = x_ref[:, C_INNER + g * C_STATE:C_INNER + (g + 1) * C_STATE].T.astype(BF16)
            cs = slice(g * C_HPG * C_HD, (g + 1) * C_HPG * C_HD)
            dcs = slice(d * C_INNER + g * C_HPG * C_HD, d * C_INNER + (g + 1) * C_HPG * C_HD)
            st[:, cs] = st[:, cs] * dec_x[:, dcs] + _mm(bm_t, xw[:, cs])

    @pl.when(c == nc - 1)
    def _():
        for d, st in enumerate((hf_sc, hb_sc)):
            for k in range(C_INNER // LANE):
                fin_o[0, d, k * LANE:(k + 1) * LANE, :] = st[:, k * LANE:(k + 1) * LANE].T


def _ssd1(xa, proj, dtb, a_row, e32, h0):
    dcol = OD_DT // LANE
    h0_idx = lambda s: (jnp.where(s < NCH_P, 0, 1 + (s - NCH_P) // NC_S), 0, 0, 0)
    return pl.pallas_call(
        _ssd1_kernel,
        grid=(NCH,),
        in_specs=[pl.BlockSpec((Q, C_CONV_DIM), lambda s: (s, 0)),
                  pl.BlockSpec((Q, C_CONV_DIM), lambda s: (_chunk_mirror(s), 0)),
                  pl.BlockSpec((Q, LANE), lambda s: (s, dcol)),
                  pl.BlockSpec((Q, LANE), lambda s: (_chunk_mirror(s), dcol)),
                  pl.BlockSpec((1, LANE), lambda s: (0, 0)),
                  pl.BlockSpec((1, LANE), lambda s: (0, 0)),
                  pl.BlockSpec((LANE, 2 * C_INNER), lambda s: (0, 0)),
                  pl.BlockSpec((1, 2, C_STATE, C_INNER), h0_idx)],
        out_specs=[pl.BlockSpec((1, C_STATE, C_INNER), lambda s: (s, 0, 0)),
                   pl.BlockSpec((1, C_STATE, C_INNER), lambda s: (_chunk_mirror(s), 0, 0)),
                   pl.BlockSpec((1, 2, C_INNER, C_STATE), lambda s: (_chunk_seq(s)[0], 0, 0, 0))],
        out_shape=[jax.ShapeDtypeStruct((NCH, C_STATE, C_INNER), BF16),
                   jax.ShapeDtypeStruct((NCH, C_STATE, C_INNER), BF16),
                   jax.ShapeDtypeStruct((NSEQ, 2, C_INNER, C_STATE), F32)],
        scratch_shapes=[pltpu.VMEM((C_STATE, C_INNER), F32)] * 2,
        compiler_params=_cp(("arbitrary",)),
        name="ssd_states",
    )(xa, xa, proj, proj, dtb, a_row, e32, h0)


def _ssd2_kernel(xa_ref, d_ref, z_ref, hf_ref, hb_ref, dtb_ref, a_ref, e32_ref, dsk_ref, gn_ref, o_ref, y_sc):
    dt, la, acum = _ssd_dt(d_ref[...], dtb_ref[...], a_ref[...])
    tot = acum[Q - 1:Q, :]
    excl = acum - la
    lane = lax.broadcasted_iota(jnp.int32, (Q, LANE), 1)
    is_f = lane < C_HEADS
    logdt = jnp.log(dt)
    col = jnp.where(is_f, acum, excl)
    row_t = jnp.where(is_f, acum - logdt, excl + logdt).T
    dec_x = _expand(jnp.where(is_f, jnp.exp(acum), jnp.exp(tot - excl)), e32_ref)
    x = xa_ref[:, 0:C_INNER]
    xb16 = x.astype(BF16)
    ti = lax.broadcasted_iota(jnp.int32, (Q, Q), 0)
    si = lax.broadcasted_iota(jnp.int32, (Q, Q), 1)
    low = si <= ti
    upp = si >= ti
    lane_q = lax.broadcasted_iota(jnp.int32, (Q, LANE), 1)
    gw = C_HPG * C_HD
    for g in range(C_GROUPS):
        bm = xa_ref[:, C_INNER + g * C_STATE:C_INNER + (g + 1) * C_STATE].astype(BF16)
        cm = xa_ref[:, C_INNER + (C_GROUPS + g) * C_STATE:C_INNER + (C_GROUPS + g + 1) * C_STATE].astype(BF16)
        gmat = _mm_nt(cm, bm)
        cs = slice(g * gw, (g + 1) * gw)
        ys = (_mm(cm, hf_ref[0, :, cs]) * dec_x[:, cs]
              + _mm(cm, hb_ref[0, :, cs]) * dec_x[:, C_INNER + g * gw:C_INNER + (g + 1) * gw])
        for pr in range(C_HPG // 2):
            xpair = xb16[:, g * gw + pr * LANE:g * gw + (pr + 1) * LANE]
            res = []
            for e in range(2):
                h = g * C_HPG + 2 * pr + e
                hb_ = C_HEADS + h
                m_f = jnp.where(low, jnp.exp(col[:, h:h + 1] - row_t[h:h + 1, :]), 0.0)
                m_b = jnp.where(upp, jnp.exp(row_t[hb_:hb_ + 1, :] - col[:, hb_:hb_ + 1]), 0.0)
                res.append(_mm((gmat * (m_f + m_b)).astype(BF16), xpair))
            y_sc[:, g * gw + pr * LANE:g * gw + (pr + 1) * LANE] = (
                jnp.where(lane_q < C_HD, res[0], res[1]) + ys[:, pr * LANE:(pr + 1) * LANE])
    y = y_sc[...] + dsk_ref[...] * x
    y = y * _silu(z_ref[...])
    y = y * lax.rsqrt(jnp.mean(y * y, axis=-1, keepdims=True) + EPS) * gn_ref[...]
    o_ref[...] = y.astype(BF16)


def _ssd2(xa, proj, hf, hb, dtb, a_row, e32, dsk, gn):
    dcol = OD_DT // LANE
    one = lambda w: pl.BlockSpec((1, w), lambda s: (0, 0))
    return pl.pallas_call(
        _ssd2_kernel,
        grid=(NCH,),
        in_specs=[pl.BlockSpec((Q, C_CONV_DIM), lambda s: (s, 0)),
                  pl.BlockSpec((Q, LANE), lambda s: (s, dcol)),
                  pl.BlockSpec((Q, C_INNER), lambda s: (s, 0)),
                  pl.BlockSpec((1, C_STATE, C_INNER), lambda s: (s, 0, 0)),
                  pl.BlockSpec((1, C_STATE, C_INNER), lambda s: (s, 0, 0)),
                  one(LANE), one(LANE),
                  pl.BlockSpec((LANE, 2 * C_INNER), lambda s: (0, 0)),
                  one(C_INNER), one(C_INNER)],
        out_specs=pl.BlockSpec((Q, C_INNER), lambda s: (s, 0)),
        out_shape=jax.ShapeDtypeStruct((T_ALL, C_INNER), BF16),
        scratch_shapes=[pltpu.VMEM((Q, C_INNER), F32)],
        compiler_params=_cp(("parallel",)),
        name="ssd_out",
    )(xa, proj, proj, hf, hb, dtb, a_row, e32, dsk, gn)


S5_MM_STEPS = 32


def _cmul(ar, ai, br, bi):
    return ar * br - ai * bi, ar * bi + ai * br


def _s5_kernel(u_ref, wx_ref, cw_ref, a_ref, dsk_ref, h0_ref, y_ref, st_ref, xs, *, nseg):
    ct = pl.program_id(0)
    quarter = ct % 4
    nslab = S5_STEPS // S5_MM_STEPS
    rows = S5_MM_STEPS * S5_CHAINS

    def fill(k, carry):
        r0 = pl.multiple_of(k * S5_MM_STEPS, S5_MM_STEPS)
        ub = u_ref[pl.ds(r0, S5_MM_STEPS)].reshape(rows, LANE).astype(BF16)
        xs[pl.ds(r0, S5_MM_STEPS)] = _mm(ub, wx_ref[0]).reshape(S5_MM_STEPS, S5_CHAINS, 4 * LANE)
        return carry
    lax.fori_loop(0, nslab, fill, 0)

    shape = (S5_CHAINS, LANE)
    a = a_ref[0]
    afr, afi, abr, abi = (jnp.broadcast_to(a[k:k + 1, :], shape) for k in range(4))

    def scan(init, store):
        def body(i, carry):
            hfr, hfi, hbr, hbi = carry
            j = S5_STEPS - 1 - i
            pr, pi = _cmul(afr, afi, hfr, hfi)
            nfr = pr + xs[i, :, 0:LANE]
            nfi = pi + xs[i, :, LANE:2 * LANE]
            pr, pi = _cmul(abr, abi, hbr, hbi)
            nbr = pr + xs[j, :, 2 * LANE:3 * LANE]
            nbi = pi + xs[j, :, 3 * LANE:4 * LANE]
            if store:
                xs[i, :, 0:LANE] = nfr
                xs[i, :, LANE:2 * LANE] = nfi
                xs[j, :, 2 * LANE:3 * LANE] = nbr
                xs[j, :, 3 * LANE:4 * LANE] = nbi
            return nfr, nfi, nbr, nbi
        return lax.fori_loop(0, S5_STEPS, body, init)

    zero = jnp.zeros(shape, F32)
    if nseg == 1:
        init = (zero, zero, zero, zero)
    else:
        efr, efi, ebr, ebi = scan((zero, zero, zero, zero), store=False)
        pfr, pfi, pbr, pbi = afr, afi, abr, abi
        for _ in range(int(math.log2(S5_STEPS))):
            pfr, pfi = _cmul(pfr, pfi, pfr, pfi)
            pbr, pbi = _cmul(pbr, pbi, pbr, pbi)
        row = lax.broadcasted_iota(jnp.int32, shape, 0) % nseg
        first = row == 0
        last = row == nseg - 1
        h0 = h0_ref[...]
        cfr, cfi, cbr, cbi = zero, zero, zero, zero
        dn = lambda v: pltpu.roll(v, 1, 0)
        up = lambda v: pltpu.roll(v, S5_CHAINS - 1, 0)
        for _ in range(nseg):
            pr, pi = _cmul(pfr, pfi, dn(cfr), dn(cfi))
            cfr = jnp.where(first, h0[0], pr + dn(efr))
            cfi = jnp.where(first, h0[1], pi + dn(efi))
            pr, pi = _cmul(pbr, pbi, up(cbr), up(cbi))
            cbr = jnp.where(last, h0[2], pr + up(ebr))
            cbi = jnp.where(last, h0[3], pi + up(ebi))
        init = (cfr, cfi, cbr, cbi)
    fin = scan(init, store=True)
    for k in range(4):
        st_ref[k] = fin[k]

    def emit(k, carry):
        r0 = pl.multiple_of(k * S5_MM_STEPS, S5_MM_STEPS)
        blk = xs[pl.ds(r0, S5_MM_STEPS)]
        s_re = blk[:, :, 0:LANE] + blk[:, :, 2 * LANE:3 * LANE]
        s_im = blk[:, :, LANE:2 * LANE] + blk[:, :, 3 * LANE:4 * LANE]
        s2 = jnp.concatenate([s_re, s_im], axis=-1).reshape(rows, 2 * LANE).astype(BF16)
        yv = _mm(s2, cw_ref[0]).reshape(S5_MM_STEPS, S5_CHAINS, LANE)

        @pl.when(quarter == 0)
        def _():
            y_ref[pl.ds(r0, S5_MM_STEPS)] = yv + dsk_ref[...] * u_ref[pl.ds(r0, S5_MM_STEPS)]

        @pl.when(quarter != 0)
        def _():
            y_ref[pl.ds(r0, S5_MM_STEPS)] += yv
        return carry
    lax.fori_loop(0, nslab, emit, 0)


def _s5(u_perm, sw, h0, nseg):
    nct = D_GROUPS // 2
    return pl.pallas_call(
        functools.partial(_s5_kernel, nseg=nseg),
        grid=(nct,),
        in_specs=[pl.BlockSpec((S5_STEPS, S5_CHAINS, LANE), lambda ct: (0, 0, ct // 4)),
                  pl.BlockSpec((1, LANE, 4 * LANE), lambda ct: (ct, 0, 0)),
                  pl.BlockSpec((1, 2 * LANE, LANE), lambda ct: (ct, 0, 0)),
                  pl.BlockSpec((1, 8, LANE), lambda ct: (ct, 0, 0)),
                  pl.BlockSpec((1, LANE), lambda ct: (0, ct // 4)),
                  pl.BlockSpec((4, S5_CHAINS, LANE), lambda ct: (0, 0, ct))],
        out_specs=[pl.BlockSpec((S5_STEPS, S5_CHAINS, LANE), lambda ct: (0, 0, ct // 4)),
                   pl.BlockSpec((4, S5_CHAINS, LANE), lambda ct: (0, 0, ct))],
        out_shape=[jax.ShapeDtypeStruct((S5_STEPS, S5_CHAINS, D_WIDTH), F32),
                   jax.ShapeDtypeStruct((4, S5_CHAINS, nct * LANE), F32)],
        scratch_shapes=[pltpu.VMEM((S5_STEPS, S5_CHAINS, 4 * LANE), F32)],
        compiler_params=_cp(("arbitrary",)),
        name="s5_seg" if nseg > 1 else "s5",
    )(u_perm, sw["wx"], sw["cw"], sw["a"], sw["dsk"], h0)


def _pad_last(a, n):
    return jnp.pad(a, [(0, 0)] * (a.ndim - 1) + [(0, n - a.shape[-1])])


def _even_weights(w_in, q_lora_g, w_uq, kv_lora_g, w_ukv, q_norm_g, k_norm_g, bq_g, bk_g):
    d = w_in.shape[0]
    s0, s1, s2, s3, s4 = 384, 640, 672, 1184, 1312
    kr = jnp.pad(w_in[:, s1:s2], ((0, 0), (A_NOPE, LANE - A_QK)))
    heads = lambda w, n: _pad_last(w.reshape(d, n, B_HD), LANE).reshape(d, n * LANE)
    w_perm = jnp.concatenate([w_in[:, :s0], w_in[:, s0:s1], kr, heads(w_in[:, s2:s3], B_HEADS),
                              heads(w_in[:, s3:s4], B_KV_HEADS), heads(w_in[:, s4:], B_KV_HEADS)], axis=1)
    row = lambda g: _pad_last(g.reshape(1, -1), LANE)
    return {
        "w_in": w_perm.astype(BF16),
        "gql": q_lora_g.reshape(1, -1),
        "wuq": _pad_last(w_uq, LANE).reshape(A_Q_LORA, A_HEADS * LANE).astype(BF16),
        "gqn": row(q_norm_g),
        "gkvl": kv_lora_g.reshape(1, -1),
        "wk": _pad_last(w_ukv[:, :, :A_NOPE], LANE).reshape(A_KV_LORA, A_HEADS * LANE).astype(BF16),
        "wv": _pad_last(w_ukv[:, :, A_NOPE:], LANE).reshape(A_KV_LORA, A_HEADS * LANE).astype(BF16),
        "gkn": row(k_norm_g),
        "gbq": row(bq_g),
        "gbk": row(bk_g),
    }


def _rope_tables():
    rows = DEC_SEQ // GRID_W

    def angles(dim):
        half = dim // 2
        inv = ROPE_THETA ** (-jnp.arange(0, half, 2, dtype=F32) / half)
        r = jnp.repeat(jnp.arange(rows, dtype=F32), GRID_W)
        col = jnp.tile(jnp.arange(GRID_W, dtype=F32), rows)
        return jnp.concatenate([r[:, None] * inv, col[:, None] * inv], axis=-1)

    def lanes(dim, start):
        ang = angles(dim)
        cos = jnp.repeat(jnp.cos(ang), 2, axis=-1)
        sin = jnp.repeat(jnp.sin(ang), 2, axis=-1) * jnp.tile(jnp.array([-1.0, 1.0], F32), dim // 2)
        pad = ((0, 0), (start, LANE - start - dim))
        return jnp.pad(cos, pad, constant_values=1.0), jnp.pad(sin, pad)

    cos_a, sin_a = lanes(A_ROPE, A_NOPE)
    cos_b, sin_b = lanes(B_HD, 0)
    return cos_a, sin_a, cos_b, sin_b


def _odd_weights(w_in, dt_bias, a_log, c_d_skip):
    w_perm = jnp.concatenate([w_in[:, :C_INNER], w_in[:, C_INNER:C_INNER + C_CONV_DIM],
                              w_in[:, C_INNER + C_CONV_DIM + 2 * C_HEADS:],
                              _pad_last(w_in[:, C_INNER + C_CONV_DIM:C_INNER + C_CONV_DIM + 2 * C_HEADS], LANE)], axis=1)
    lane = jnp.arange(LANE)
    col = jnp.arange(2 * C_INNER)
    e32 = ((lane[:, None] < 2 * C_HEADS) & (lane[:, None] == col[None, :] // C_HD)).astype(BF16)
    return {
        "w_in": w_perm.astype(BF16),
        "dtb": _pad_last(dt_bias.reshape(1, -1), LANE),
        "a_row": _pad_last(-jnp.exp(a_log).reshape(1, -1), LANE),
        "e32": e32,
        "dsk": jnp.repeat(c_d_skip, C_HD).reshape(1, C_INNER),
    }


def _s5_weights(lam_re, lam_im, log_dt, b_re, b_im, c_re, c_im, d_skip):
    nct = D_GROUPS // 2
    dt = jnp.exp(log_dt)[..., None]
    mag = jnp.exp(lam_re * dt)
    ar, ai = mag * jnp.cos(lam_im * dt), mag * jnp.sin(lam_im * dt)
    den = lam_re * lam_re + lam_im * lam_im
    kr = ((ar - 1.0) * lam_re + ai * lam_im) / den
    ki = (ai * lam_re - (ar - 1.0) * lam_im) / den
    xr = kr[..., None] * b_re[None] - ki[..., None] * b_im[None]
    xi = kr[..., None] * b_im[None] + ki[..., None] * b_re[None]
    xc = jnp.stack([xr[0], xi[0], xr[1], xi[1]]).reshape(4, nct, 2, D_STATE, D_GROUP_SIZE)
    quarter = (jnp.arange(nct)[:, None] % 4 == jnp.arange(4)[None, :]).astype(F32)
    eye2 = jnp.eye(2, dtype=F32)
    wx = jnp.einsum("tq,gh,ktgnj->tqgjkhn", quarter, eye2, xc).reshape(nct, LANE, 4 * LANE)
    cc = jnp.stack([c_re, -c_im]).reshape(2, nct, 2, D_GROUP_SIZE, D_STATE)
    cw = jnp.einsum("tq,gh,rtgcn->trhnqgc", quarter, eye2, cc).reshape(nct, 2 * LANE, LANE)
    planes = jnp.stack([ar[0], ai[0], ar[1], ai[1]]).reshape(4, nct, LANE).transpose(1, 0, 2)
    return {"wx": wx.astype(BF16), "cw": cw.astype(BF16), "a": jnp.pad(planes, ((0, 0), (0, 4), (0, 0))),
            "dsk": d_skip.reshape(1, D_WIDTH)}


def _even_layer(x, mod, norm1_g, norm2_g, ew, w_out, ffn_w, caches, rope_tabs):
    c_ckv, c_krope, c_kb, c_vb = caches
    proj = _modmm(x, mod, norm1_g, ew["w_in"])
    qa_p, ka_p, va_p, qb_p, kb_p, vb_p, ckv_c, kr_c, kbn_c, vb_c = _evenprep(proj, 0, TP, ew, None)
    qa_s, ka_s, va_s, qb_s, kb_s, vb_s = _evenprep(proj, TP, TS, ew, rope_tabs)
    n_ctx = DEC_BATCH * PAST_LEN
    kr_blk = jnp.pad(c_krope.reshape(n_ctx, A_ROPE), ((0, 0), (A_NOPE, LANE - A_QK)))
    ka_c, va_c = _ctxkv(c_ckv.reshape(n_ctx, A_KV_LORA), kr_blk, ew)
    heads_b = lambda a: _pad_last(a.transpose(0, 2, 1, 3), LANE).reshape(n_ctx, B_KV_HEADS * LANE).astype(BF16)
    sa, sb = A_QK ** -0.5, B_HD ** -0.5
    oa_p = _attn(qa_p, ka_p, va_p, None, nb=BATCH, lq=SEQ, lk=SEQ, nheads=A_HEADS, hp=2, shared_kv=False,
                 scale=sa, tq=SEQ)
    ob_p = _attn(qb_p, kb_p, vb_p, None, nb=BATCH, lq=SEQ, lk=SEQ, nheads=B_HEADS, hp=4, shared_kv=True,
                 scale=sb, tq=SEQ)
    oa = _attn(qa_s, ka_s, va_s, (ka_c, va_c, oa_p), nb=DEC_BATCH, lq=DEC_SEQ, lk=DEC_SEQ, nheads=A_HEADS, hp=2,
               shared_kv=False, scale=sa, tq=ATT_TQ)
    ob = _attn(qb_s, kb_s, vb_s, (heads_b(c_kb), heads_b(c_vb), ob_p), nb=DEC_BATCH, lq=DEC_SEQ, lk=DEC_SEQ,
               nheads=B_HEADS, hp=4, shared_kv=True, scale=sb, tq=ATT_TQ)
    x = _outproj(x, mod, oa, ob, w_out[:A_HEADS * A_V], w_out[A_HEADS * A_V:])
    x = _ffn(x, mod, norm2_g, *ffn_w)
    new_cache = (ckv_c.reshape(BATCH, SEQ, A_KV_LORA), kr_c[:, A_NOPE:A_QK].reshape(BATCH, SEQ, A_ROPE),
                 kbn_c, vb_c)
    return x, new_cache


def _odd_layer(x, mod, norm1_g, norm2_g, ow, sw, conv_w, conv_b, c_norm_g, w_glu, w_out, moe_w, states):
    st_c, st_re, st_im = states
    proj = _modmm(x, mod, norm1_g, ow["w_in"])
    xa = _conv(proj, conv_w, conv_b)
    h0 = st_c.transpose(0, 1, 4, 2, 3).reshape(DEC_BATCH, 2, C_STATE, C_INNER)
    h0 = jnp.concatenate([jnp.zeros((1,) + h0.shape[1:], F32), h0], axis=0)
    hf, hb, fin = _ssd1(xa, proj, ow["dtb"], ow["a_row"], ow["e32"], h0)
    yc = _ssd2(xa, proj, hf, hb, ow["dtb"], ow["a_row"], ow["e32"], ow["dsk"], c_norm_g.reshape(1, C_INNER))
    u = proj[:, OD_U:OD_U + D_WIDTH]
    u_p = u[:TP].reshape(BATCH, SEQ, D_WIDTH).transpose(1, 0, 2)
    u_s = u[TP:].reshape(DEC_BATCH * S5_SEG, S5_STEPS, D_WIDTH).transpose(1, 0, 2)
    zeros_h0 = jnp.zeros((4, S5_CHAINS, D_GROUPS * D_STATE), F32)
    h0_s = jnp.stack([st_re[:, 0], st_im[:, 0], st_re[:, 1], st_im[:, 1]]).reshape(4, DEC_BATCH, D_GROUPS * D_STATE)
    h0_s = jnp.repeat(h0_s, S5_SEG, axis=1)
    yd_p, fin_d = _s5(u_p, sw, zeros_h0, 1)
    yd_s, _ = _s5(u_s, sw, h0_s, S5_SEG)
    yd = jnp.concatenate([yd_p.transpose(1, 0, 2).reshape(TP, D_WIDTH),
                          yd_s.transpose(1, 0, 2).reshape(TS, D_WIDTH)], axis=0)
    x = _outproj(x, mod, yc, yd, w_out[:C_INNER], w_out[C_INNER:], wg=w_glu)
    x = _ffn(x, mod, norm2_g, *moe_w[:4], router=moe_w[4])
    new_c = fin[:BATCH].reshape(BATCH, 2, C_HEADS, C_HD, C_STATE)
    fin_d = fin_d.reshape(4, BATCH, D_GROUPS, D_STATE)
    new_re = jnp.stack([fin_d[0], fin_d[2]], axis=1)
    new_im = jnp.stack([fin_d[1], fin_d[3]], axis=1)
    return x, (new_c, new_re, new_im)


def kernel(x_prompt, x_sample, cache_a_ckv, cache_a_krope, cache_b_k, cache_b_v, state_c_ssm, state_d_re, state_d_im, c, c_ctx, w_mod, b_mod, norm1_g, norm2_g, ev_w_in, ev_a_q_lora_g, ev_a_w_uq, ev_a_kv_lora_g, ev_a_w_ukv, ev_a_q_norm_g, ev_a_k_norm_g, ev_b_q_norm_g, ev_b_k_norm_g, ev_w_out, ffn_w1, ffn_w3, ffn_w2, od_w_in, od_conv_w, od_conv_b, od_dt_bias, od_a_log, od_c_d_skip, od_c_norm_g, od_lam_re, od_lam_im, od_log_dt, od_b_re, od_b_im, od_c_re, od_c_im, od_d_skip, od_w_glu, od_w_out, moe_router_w, moe_router_b, moe_w1, moe_w3, moe_w2):
    x = jnp.concatenate([x_prompt.reshape(TP, D_MODEL), x_sample.reshape(TS, D_MODEL)], axis=0)
    cond8 = jnp.zeros((8, D_MODEL), F32).at[0].set(c_ctx).at[1:1 + DEC_BATCH].set(c)
    mods = _adaln(cond8, w_mod, b_mod)[:, :1 + DEC_BATCH].reshape(DEPTH, 1 + DEC_BATCH, 6, D_MODEL)
    mods = jnp.pad(mods, ((0, 0), (0, 0), (0, 2), (0, 0)))
    rope_tabs = _rope_tables()
    ffn_bf = (ffn_w1.astype(BF16), ffn_w3.astype(BF16), ffn_w2.astype(BF16))
    moe_bf = (moe_w1.astype(BF16), moe_w3.astype(BF16), moe_w2.astype(BF16))
    ev_new, od_new = [], []
    for layer in range(DEPTH):
        mod = mods[layer]
        if layer % 2 == 0:
            e = layer // 2
            ew = _even_weights(ev_w_in[e], ev_a_q_lora_g[e], ev_a_w_uq[e], ev_a_kv_lora_g[e], ev_a_w_ukv[e],
                               ev_a_q_norm_g[e], ev_a_k_norm_g[e], ev_b_q_norm_g[e], ev_b_k_norm_g[e])
            ffn_w = ffn_bf + (e,)
            x, new = _even_layer(x, mod, norm1_g[layer], norm2_g[layer], ew, ev_w_out[e].astype(BF16), ffn_w,
                                 (cache_a_ckv[:, e], cache_a_krope[:, e], cache_b_k[:, e], cache_b_v[:, e]),
                                 rope_tabs)
            ev_new.append(new)
        else:
            o = layer // 2
            ow = _odd_weights(od_w_in[o], od_dt_bias[o], od_a_log[o], od_c_d_skip[o])
            sw = _s5_weights(od_lam_re[o], od_lam_im[o], od_log_dt[o], od_b_re[o], od_b_im[o], od_c_re[o],
                             od_c_im[o], od_d_skip[o])
            rw_hi = _pad_last(moe_router_w[o], LANE).astype(BF16)
            rw_lo = (_pad_last(moe_router_w[o], LANE) - rw_hi.astype(F32)).astype(BF16)
            moe_w = moe_bf + (o, (rw_hi, rw_lo, _pad_last(moe_router_b[o].reshape(1, -1), LANE)))
            x, new = _odd_layer(x, mod, norm1_g[layer], norm2_g[layer], ow, sw, od_conv_w[o], od_conv_b[o],
                                od_c_norm_g[o], od_w_glu[o].astype(BF16), od_w_out[o].astype(BF16), moe_w,
                                (state_c_ssm[:, o], state_d_re[:, o], state_d_im[:, o]))
            od_new.append(new)
    stack = lambda items, k: jnp.stack([it[k] for it in items], axis=1)
    return (x[:TP].reshape(BATCH, SEQ, D_MODEL), x[TP:].reshape(DEC_BATCH, DEC_SEQ, D_MODEL),
            stack(ev_new, 0), stack(ev_new, 1), stack(ev_new, 2), stack(ev_new, 3),
            stack(od_new, 0), stack(od_new, 1), stack(od_new, 2))
```

```python
import functools
import math

import jax
import jax.numpy as jnp
from jax import lax
from jax.experimental import pallas as pl
from jax.experimental.pallas import tpu as pltpu

F32 = jnp.float32
BF16 = jnp.bfloat16

D_MODEL = 1024
BATCH = 32
SEQ = 256
DEPTH = 4
DEC_BATCH = 2
DEC_SEQ = 4096
PAST_LEN = 256
GRID_W = 64
ROPE_THETA = 10000.0
EPS = 1e-6
A_HEADS = 8
A_NOPE = 64
A_ROPE = 32
A_QK = A_NOPE + A_ROPE
A_V = 64
A_Q_LORA = 384
A_KV_LORA = 256
B_HEADS = 8
B_KV_HEADS = 2
B_HD = 64
C_HEADS = 16
C_HD = 64
C_GROUPS = 2
C_HPG = C_HEADS // C_GROUPS
C_STATE = 128
C_INNER = C_HEADS * C_HD
C_CONV_DIM = C_INNER + 2 * C_GROUPS * C_STATE
SSD_CHUNK = 128
D_GROUP_SIZE = 16
D_GROUPS = 32
D_WIDTH = D_GROUPS * D_GROUP_SIZE
D_STATE = 64
FF_DENSE = 2816
N_EXPERTS = 8
FF_EXPERT = 1408

TP = BATCH * SEQ
TS = DEC_BATCH * DEC_SEQ
T_ALL = TP + TS
LANE = 128
LOG2E = 1.4426950408889634
VMEM_LIMIT = 56 * 1024 * 1024

EV_CQ = 0
EV_CKV = 384
EV_KR = 640
EV_QB = 768
EV_KB = EV_QB + B_HEADS * LANE
EV_VB = EV_KB + B_KV_HEADS * LANE
EV_N = EV_VB + B_KV_HEADS * LANE
OD_Z = 0
OD_XBC = C_INNER
OD_U = OD_XBC + C_CONV_DIM
OD_DT = OD_U + D_WIDTH
OD_N = OD_DT + LANE
S5_STEPS = 256
S5_CHAINS = 32
S5_SEG = DEC_SEQ // S5_STEPS
ATT_TQ = 256
ATT_CHUNKS = 2


def _cp(sem):
    return pltpu.CompilerParams(dimension_semantics=sem, vmem_limit_bytes=VMEM_LIMIT)


def _cond_idx(i, tm):
    t0 = i * tm
    return jnp.where(t0 < TP, 0, 1 + (t0 - TP) // DEC_SEQ)


def _mm(a, b):
    return jnp.dot(a, b, preferred_element_type=F32)


def _mm_nt(a, b):
    return lax.dot_general(a, b, (((1,), (1,)), ((), ())), preferred_element_type=F32)


def _split_bf16(v):
    hi = v.astype(BF16)
    lo = (v - hi.astype(F32)).astype(BF16)
    return hi, lo


def _silu(v):
    return v * jax.nn.sigmoid(v)


def _rms(v, n):
    return v * lax.rsqrt(jnp.sum(v * v, axis=-1, keepdims=True) * (1.0 / n) + EPS)


def _adaln_kernel(c_ref, w_ref, b_ref, o_ref):
    s = _silu(c_ref[...])
    o_ref[0] = _mm(s.astype(BF16), w_ref[0].astype(BF16)) + b_ref[0]


def _adaln(cond8, w_mod, b_mod):
    depth, d, n6 = w_mod.shape
    tn = 1536
    return pl.pallas_call(
        _adaln_kernel,
        grid=(depth, n6 // tn),
        in_specs=[pl.BlockSpec((8, d), lambda l, j: (0, 0)),
                  pl.BlockSpec((1, d, tn), lambda l, j: (l, 0, j)),
                  pl.BlockSpec((1, 1, tn), lambda l, j: (l, 0, j))],
        out_specs=pl.BlockSpec((1, 8, tn), lambda l, j: (l, 0, j)),
        out_shape=jax.ShapeDtypeStruct((depth, 8, n6), F32),
        compiler_params=_cp(("parallel", "parallel")),
        name="adaln",
    )(cond8, w_mod, b_mod.reshape(depth, 1, n6))


def _modulate(x, g, mod_ref, shift_row, scale_row):
    y = x * lax.rsqrt(jnp.mean(x * x, axis=-1, keepdims=True) + EPS) * g
    return y * (1.0 + mod_ref[0, scale_row:scale_row + 1, :]) + mod_ref[0, shift_row:shift_row + 1, :]


def _modmm_kernel(x_ref, mod_ref, g_ref, w_ref, o_ref):
    h = _modulate(x_ref[...], g_ref[...], mod_ref, 0, 1)
    o_ref[...] = _mm(h.astype(BF16), w_ref[...])


def _modmm(x, mod, g, w, tm=512):
    t, d = x.shape
    n = w.shape[1]
    return pl.pallas_call(
        _modmm_kernel,
        grid=(t // tm,),
        in_specs=[pl.BlockSpec((tm, d), lambda i: (i, 0)),
                  pl.BlockSpec((1, 8, d), lambda i: (_cond_idx(i, tm), 0, 0)),
                  pl.BlockSpec((1, d), lambda i: (0, 0)),
                  pl.BlockSpec((d, n), lambda i: (0, 0))],
        out_specs=pl.BlockSpec((tm, n), lambda i: (i, 0)),
        out_shape=jax.ShapeDtypeStruct((t, n), F32),
        compiler_params=_cp(("parallel",)),
        name="modmm",
    )(x, mod, g.reshape(1, d), w)


def _swap_pairs(v):
    lane = lax.broadcasted_iota(jnp.int32, v.shape, 1)
    nxt = pltpu.roll(v, LANE - 1, 1)
    prv = pltpu.roll(v, 1, 1)
    return jnp.where((lane & 1) == 0, nxt, prv)


def _rope(v, cos, sin):
    return v * cos + _swap_pairs(v) * sin


def _with_ones(v):
    lane = lax.broadcasted_iota(jnp.int32, v.shape, 1)
    return jnp.where((lane & (LANE - 1)) == A_V, 1.0, v)


def _mla_kv(ckvn, kr_blk, wk_ref, wv_ref, gkn, rope_tabs, ka_o, va_o):
    cb = ckvn.astype(BF16)
    kn = _mm(cb, wk_ref[...])
    va_o[...] = _with_ones(_mm(cb, wv_ref[...])).astype(BF16)
    for h in range(A_HEADS):
        sl = slice(h * LANE, (h + 1) * LANE)
        kh = _rms(kn[:, sl] + kr_blk, A_QK) * gkn
        if rope_tabs is not None:
            kh = _rope(kh, rope_tabs[0], rope_tabs[1])
        ka_o[:, sl] = kh.astype(BF16)


def _evenprep_kernel(*refs, rope):
    it = iter(refs)
    proj = next(it)
    if rope:
        cos_a, sin_a, cos_b, sin_b = (next(it)[...] for _ in range(4))
    gql, wuq, gqn, gkvl, wk, wv, gkn, gbq, gbk = (next(it) for _ in range(9))
    qa_o, ka_o, va_o, qb_o, kb_o, vb_o = (next(it) for _ in range(6))
    if not rope:
        ckv_o, kr_o, kbn_o, vbf_o = (next(it) for _ in range(4))

    cqn = _rms(proj[:, EV_CQ:EV_CQ + A_Q_LORA], A_Q_LORA) * gql[...]
    q = _mm(cqn.astype(BF16), wuq[...])
    for h in range(A_HEADS):
        sl = slice(h * LANE, (h + 1) * LANE)
        qh = _rms(q[:, sl], A_QK) * gqn[...]
        if rope:
            qh = _rope(qh, cos_a, sin_a)
        qa_o[:, sl] = qh.astype(BF16)

    ckvn = _rms(proj[:, EV_CKV:EV_CKV + A_KV_LORA], A_KV_LORA) * gkvl[...]
    kr_blk = proj[:, EV_KR:EV_KR + LANE]
    _mla_kv(ckvn, kr_blk, wk, wv, gkn[...], (cos_a, sin_a) if rope else None, ka_o, va_o)

    for h in range(B_HEADS):
        sl = slice(h * LANE, (h + 1) * LANE)
        qh = _rms(proj[:, EV_QB + h * LANE:EV_QB + (h + 1) * LANE], B_HD) * gbq[...]
        if rope:
            qh = _rope(qh, cos_b, sin_b)
        qb_o[:, sl] = qh.astype(BF16)
    for j in range(B_KV_HEADS):
        sl = slice(j * LANE, (j + 1) * LANE)
        kh = _rms(proj[:, EV_KB + j * LANE:EV_KB + (j + 1) * LANE], B_HD) * gbk[...]
        if not rope:
            kbn_o[0, j] = kh[:, :B_HD]
        else:
            kh = _rope(kh, cos_b, sin_b)
        kb_o[:, sl] = kh.astype(BF16)
    vb = proj[:, EV_VB:EV_VB + B_KV_HEADS * LANE]
    vb_o[...] = _with_ones(vb).astype(BF16)
    if not rope:
        for j in range(B_KV_HEADS):
            vbf_o[0, j] = vb[:, j * LANE:j * LANE + B_HD]
        ckv_o[...] = ckvn
        kr_o[...] = kr_blk


def _evenprep(proj, row_off, nrows, ew, rope_tabs):
    tm = 256
    rope = rope_tabs is not None
    off = row_off // tm
    full = lambda a: pl.BlockSpec(a.shape, lambda i: (0,) * a.ndim)
    rows = lambda w: pl.BlockSpec((tm, w), lambda i: (i, 0))
    ins = [proj]
    in_specs = [pl.BlockSpec((tm, EV_N), lambda i: (i + off, 0))]
    if rope:
        ins += list(rope_tabs)
        in_specs += [pl.BlockSpec((tm, LANE), lambda i: (i % (DEC_SEQ // tm), 0))] * 4
    ws = [ew["gql"], ew["wuq"], ew["gqn"], ew["gkvl"], ew["wk"], ew["wv"], ew["gkn"], ew["gbq"], ew["gbk"]]
    ins += ws
    in_specs += [full(a) for a in ws]
    hw = A_HEADS * LANE
    kvw = B_KV_HEADS * LANE
    out_shape = [jax.ShapeDtypeStruct((nrows, hw), BF16)] * 4 + [jax.ShapeDtypeStruct((nrows, kvw), BF16)] * 2
    out_specs = [rows(hw)] * 4 + [rows(kvw)] * 2
    if not rope:
        assert tm == SEQ
        cache_b = jax.ShapeDtypeStruct((nrows // SEQ, B_KV_HEADS, SEQ, B_HD), F32)
        cache_b_spec = pl.BlockSpec((1, B_KV_HEADS, SEQ, B_HD), lambda i: (i, 0, 0, 0))
        out_shape += [jax.ShapeDtypeStruct((nrows, A_KV_LORA), F32), jax.ShapeDtypeStruct((nrows, LANE), F32),
                      cache_b, cache_b]
        out_specs += [rows(A_KV_LORA), rows(LANE), cache_b_spec, cache_b_spec]
    return pl.pallas_call(
        functools.partial(_evenprep_kernel, rope=rope),
        grid=(nrows // tm,),
        in_specs=in_specs,
        out_specs=out_specs,
        out_shape=out_shape,
        compiler_params=_cp(("parallel",)),
        name="evenprep_rope" if rope else "evenprep",
    )(*ins)


def _ctxkv_kernel(ckv_ref, kr_ref, wk, wv, gkn, ka_o, va_o):
    _mla_kv(ckv_ref[...], kr_ref[...], wk, wv, gkn[...], None, ka_o, va_o)


def _ctxkv(ckvn, kr_blk, ew):
    n = ckvn.shape[0]
    hw = A_HEADS * LANE
    return pl.pallas_call(
        _ctxkv_kernel,
        out_shape=[jax.ShapeDtypeStruct((n, hw), BF16)] * 2,
        compiler_params=pltpu.CompilerParams(vmem_limit_bytes=VMEM_LIMIT),
        name="ctxkv",
    )(ckvn, kr_blk, ew["wk"], ew["wv"], ew["gkn"])


def _attn_kernel(*refs, hp, shared_kv, c, has_ctx):
    if has_ctx:
        q_ref, kn_ref, vn_ref, kc_ref, vc_ref, _, o_ref = refs
    else:
        q_ref, kn_ref, vn_ref, _, o_ref = refs
    outs = []
    for h in range(hp):
        kv = slice(0, LANE) if shared_kv else slice(h * LANE, (h + 1) * LANE)
        q = q_ref[:, h * LANE:(h + 1) * LANE]
        lk = kn_ref.shape[0]
        ck = lk // ATT_CHUNKS if has_ctx else lk
        parts = [(kn_ref, vn_ref, k0, ck) for k0 in range(0, lk, ck)]
        if has_ctx:
            parts.append((kc_ref, vc_ref, 0, kc_ref.shape[0]))
        m = acc = None
        for k_ref, v_ref, k0, n in parts:
            s = _mm_nt(q, k_ref[k0:k0 + n, kv])
            m_c = jnp.max(s, axis=-1, keepdims=True)
            m_new = m_c if m is None else jnp.maximum(m, m_c)
            p = jnp.exp2((s - m_new) * c)
            a_c = _mm(p.astype(BF16), v_ref[k0:k0 + n, kv])
            acc = a_c if m is None else jnp.exp2((m - m_new) * c) * acc + a_c
            m = m_new
        outs.append(acc * (1.0 / acc[:, A_V:A_V + 1]))
    for p2 in range(hp // 2):
        pair = jnp.concatenate([outs[2 * p2][:, :A_V], outs[2 * p2 + 1][:, :A_V]], axis=-1)
        o_ref[:, p2 * LANE:(p2 + 1) * LANE] = pair.astype(BF16)


def _attn(q, kn, vn, ctx, into, *, nb, lq, lk, nheads, hp, shared_kv, scale, tq):
    ngroups = nheads // hp
    kvw = LANE if shared_kv else hp * LANE
    nq = lq // tq
    ins = [q, kn, vn]
    in_specs = [pl.BlockSpec((tq, hp * LANE), lambda b, g, i: (b * nq + i, g)),
                pl.BlockSpec((lk, kvw), lambda b, g, i: (b, g)),
                pl.BlockSpec((lk, kvw), lambda b, g, i: (b, g))]
    row0 = 0
    if ctx is not None:
        ins += list(ctx)
        in_specs += [pl.BlockSpec((PAST_LEN, kvw), lambda b, g, i: (b, g))] * 2
        row0 = TP // tq
    ins.append(into)
    in_specs.append(pl.BlockSpec(memory_space=pl.ANY))
    aliases = {len(ins) - 1: 0}
    return pl.pallas_call(
        functools.partial(_attn_kernel, hp=hp, shared_kv=shared_kv, c=scale * LOG2E, has_ctx=ctx is not None),
        grid=(nb, ngroups, nq),
        in_specs=in_specs,
        out_specs=pl.BlockSpec((tq, hp * A_V), lambda b, g, i: (row0 + b * nq + i, g)),
        out_shape=jax.ShapeDtypeStruct((T_ALL, nheads * A_V), BF16),
        input_output_aliases=aliases,
        compiler_params=_cp(("parallel", "parallel", "arbitrary")),
        name="attn_ctx" if ctx is not None else "attn",
    )(*ins)


def _outproj_kernel(*refs, glu):
    if glu:
        x_ref, mod_ref, a_ref, b_ref, wa_ref, wb_ref, wg_ref, o_ref = refs
        yd = jax.nn.gelu(b_ref[...])
        b = (yd * jax.nn.sigmoid(_mm(yd.astype(BF16), wg_ref[...]))).astype(BF16)
    else:
        x_ref, mod_ref, a_ref, b_ref, wa_ref, wb_ref, o_ref = refs
        b = b_ref[...]
    acc = _mm(a_ref[...], wa_ref[...]) + _mm(b, wb_ref[...])
    o_ref[...] = x_ref[...] + mod_ref[0, 2:3, :] * acc


def _outproj(x, mod, a, b, wa, wb, wg=None, tm=512):
    t, d = x.shape
    full = lambda v: pl.BlockSpec(v.shape, lambda i: (0,) * v.ndim)
    rows = lambda v: pl.BlockSpec((tm, v.shape[1]), lambda i: (i, 0))
    ins = [x, mod, a, b, wa, wb]
    in_specs = [rows(x), pl.BlockSpec((1, 8, d), lambda i: (_cond_idx(i, tm), 0, 0)), rows(a), rows(b),
                full(wa), full(wb)]
    if wg is not None:
        ins.append(wg)
        in_specs.append(full(wg))
    return pl.pallas_call(
        functools.partial(_outproj_kernel, glu=wg is not None),
        grid=(t // tm,),
        in_specs=in_specs,
        out_specs=rows(x),
        out_shape=jax.ShapeDtypeStruct((t, d), F32),
        compiler_params=_cp(("parallel",)),
        name="outproj_glu" if wg is not None else "outproj",
    )(*ins)


FF_SPLIT = (0, 768, FF_EXPERT)


MOE_BLK = 256


def _swiglu(h, w1, w3, w2):
    y = None
    for k in range(len(FF_SPLIT) - 1):
        sl = slice(FF_SPLIT[k], FF_SPLIT[k + 1])
        act = (_silu(_mm(h, w1[:, sl])) * _mm(h, w3[:, sl])).astype(BF16)
        part = _mm(act, w2[sl, :])
        y = part if y is None else y + part
    return y


def _ffn_kernel(*refs, routed, nj):
    if routed:
        (x_ref, mod_ref, g_ref, rwh_ref, rwl_ref, rb_ref, w1_ref, w3_ref, w2_ref, o_ref,
         h_sc, acc_sc, comb_sc, rank_sc, rank_t_sc, cnt_sc) = refs
    else:
        x_ref, mod_ref, g_ref, w1_ref, w3_ref, w2_ref, o_ref, h_sc, acc_sc = refs
    j = pl.program_id(1)
    tm = x_ref.shape[0]

    @pl.when(j == 0)
    def _():
        h = _modulate(x_ref[...], g_ref[...], mod_ref, 3, 4)
        h_hi = h.astype(BF16)
        h_sc[...] = h_hi
        acc_sc[...] = jnp.zeros(acc_sc.shape, F32)
        if routed:
            h_lo = (h - h_hi.astype(F32)).astype(BF16)
            logits = _mm(h_hi, rwh_ref[...]) + _mm(h_lo, rwh_ref[...]) + _mm(h_hi, rwl_ref[...]) + rb_ref[...]
            lane = lax.broadcasted_iota(jnp.int32, logits.shape, 1)
            lg = jnp.where(lane < N_EXPERTS, logits, -jnp.inf)
            m1 = jnp.max(lg, axis=-1, keepdims=True)
            i1 = jnp.min(jnp.where(lg == m1, lane, LANE), axis=-1, keepdims=True)
            lg2 = jnp.where(lane == i1, -jnp.inf, lg)
            m2 = jnp.max(lg2, axis=-1, keepdims=True)
            i2 = jnp.min(jnp.where(lg2 == m2, lane, LANE), axis=-1, keepdims=True)
            e2 = jnp.exp(m2 - m1)
            inv = 1.0 / (1.0 + e2)
            comb_sc[...] = jnp.where(lane == i1, inv, 0.0) + jnp.where(lane == i2, e2 * inv, 0.0)
            sel = jnp.where(lane == i1, 1.0, 0.0) + jnp.where(lane == i2, 1.0, 0.0)
            ti = lax.broadcasted_iota(jnp.int32, (tm, tm), 0)
            si = lax.broadcasted_iota(jnp.int32, (tm, tm), 1)
            before = _mm((si < ti).astype(BF16), sel.astype(BF16))
            rank = jnp.where(sel > 0.0, before, -1.0)
            rank_sc[...] = rank
            cnt_sc[...] = jnp.broadcast_to(before[tm - 1:tm, :] + sel[tm - 1:tm, :], cnt_sc.shape)
            for k in range(tm // LANE):
                rank_t_sc[:, k * LANE:(k + 1) * LANE] = rank[k * LANE:(k + 1) * LANE, :].T

    if routed:
        lane = lax.broadcasted_iota(jnp.int32, comb_sc.shape, 1)
        lane8 = lax.broadcasted_iota(jnp.int32, cnt_sc.shape, 1)
        n = jnp.sum(jnp.where(lane8 == j, cnt_sc[...], 0.0)[0:1, :]).astype(jnp.int32)
        shift = int(math.log2(MOE_BLK))
        nfull = lax.shift_right_logical(n, shift)
        rem = n - lax.shift_left(nfull, shift)
        gate = jnp.sum(jnp.where(lane == j, comb_sc[...], 0.0), axis=-1, keepdims=True)
        rank_col = jnp.sum(jnp.where(lane == j, rank_sc[...], 0.0), axis=-1, keepdims=True)
        rank_row = rank_t_sc[pl.ds(j, 1), :]

        def run(r0, rows):
            r0 = r0.astype(F32)
            r_iota = lax.broadcasted_iota(jnp.int32, (rows, tm), 0).astype(F32)
            c_iota = lax.broadcasted_iota(jnp.int32, (tm, rows), 1).astype(F32)
            pick = jnp.where(rank_row - r0 == r_iota, 1.0, 0.0).astype(BF16)
            hb = _mm(pick, h_sc[...]).astype(BF16)
            yb = _swiglu(hb, w1_ref.at[0, 0], w3_ref.at[0, 0], w2_ref.at[0, 0]).astype(BF16)
            place = jnp.where(rank_col - r0 == c_iota, 1.0, 0.0).astype(BF16)
            acc_sc[...] += gate * _mm(place, yb)

        def block(b, carry):
            run(b * MOE_BLK, MOE_BLK)
            return carry
        lax.fori_loop(0, nfull + (rem > MOE_BLK // 2).astype(jnp.int32), block, 0)

        @pl.when((rem > 0) & (rem <= MOE_BLK // 2))
        def _():
            run(nfull * MOE_BLK, MOE_BLK // 2)
    else:
        acc_sc[...] += _swiglu(h_sc[...], w1_ref.at[0], w3_ref.at[0], w2_ref.at[0])

    @pl.when(j == nj - 1)
    def _():
        o_ref[...] = x_ref[...] + mod_ref[0, 5:6, :] * acc_sc[...]


def _ffn(x, mod, g, w1, w3, w2, li, router=None, tm=1024):
    t, d = x.shape
    routed = router is not None
    tf = FF_EXPERT
    nj = w1.shape[1] if routed else w1.shape[2] // tf
    ins = [x, mod, g.reshape(1, d)]
    in_specs = [pl.BlockSpec((tm, d), lambda i, j: (i, 0)),
                pl.BlockSpec((1, 8, d), lambda i, j: (_cond_idx(i, tm), 0, 0)),
                pl.BlockSpec((1, d), lambda i, j: (0, 0))]
    scratch = [pltpu.VMEM((tm, d), BF16), pltpu.VMEM((tm, d), F32)]
    if routed:
        ins += list(router)
        in_specs += [pl.BlockSpec((d, LANE), lambda i, j: (0, 0))] * 2 + [pl.BlockSpec((1, LANE), lambda i, j: (0, 0))]
        in_specs += ([pl.BlockSpec((1, 1, d, tf), lambda i, j: (li, j, 0, 0))] * 2
                     + [pl.BlockSpec((1, 1, tf, d), lambda i, j: (li, j, 0, 0))])
        scratch += [pltpu.VMEM((tm, LANE), F32), pltpu.VMEM((tm, LANE), F32), pltpu.VMEM((LANE, tm), F32),
                    pltpu.VMEM((8, LANE), F32)]
    else:
        in_specs += ([pl.BlockSpec((1, d, tf), lambda i, j: (li, 0, j))] * 2
                     + [pl.BlockSpec((1, tf, d), lambda i, j: (li, j, 0))])
    ins += [w1, w3, w2]
    return pl.pallas_call(
        functools.partial(_ffn_kernel, routed=routed, nj=nj),
        grid=(t // tm, nj),
        in_specs=in_specs,
        out_specs=pl.BlockSpec((tm, d), lambda i, j: (i, 0)),
        out_shape=jax.ShapeDtypeStruct((t, d), F32),
        scratch_shapes=scratch,
        compiler_params=_cp(("parallel", "arbitrary")),
        name="moe" if routed else "ffn",
    )(*ins)


CONV_ROWS = 4096


def _conv_kernel(x_ref, w_ref, b_ref, o_ref):
    i = pl.program_id(0)
    seq = jnp.where(i < TP // CONV_ROWS, SEQ, DEC_SEQ)
    x = x_ref[...]
    pos = lax.broadcasted_iota(jnp.int32, x.shape, 0) & (seq - 1)
    prev = jnp.where(pos == 0, 0.0, pltpu.roll(x, 1, 0))
    nxt = jnp.where(pos == seq - 1, 0.0, pltpu.roll(x, CONV_ROWS - 1, 0))
    y = prev * w_ref[0:1, :] + x * w_ref[1:2, :] + nxt * w_ref[2:3, :] + b_ref[...]
    o_ref[...] = _silu(y)


def _conv(proj, w, b):
    tc = 256
    t = proj.shape[0]
    return pl.pallas_call(
        _conv_kernel,
        grid=(t // CONV_ROWS, C_CONV_DIM // tc),
        in_specs=[pl.BlockSpec((CONV_ROWS, tc), lambda i, j: (i, j + OD_XBC // tc)),
                  pl.BlockSpec((3, tc), lambda i, j: (0, j)),
                  pl.BlockSpec((1, tc), lambda i, j: (0, j))],
        out_specs=pl.BlockSpec((CONV_ROWS, tc), lambda i, j: (i, j)),
        out_shape=jax.ShapeDtypeStruct((t, C_CONV_DIM), F32),
        compiler_params=_cp(("parallel", "parallel")),
        name="conv",
    )(proj, w, b.reshape(1, C_CONV_DIM))


Q = SSD_CHUNK
NCH = T_ALL // Q
NCH_P = TP // Q
NC_P = SEQ // Q
NC_S = DEC_SEQ // Q


def _chunk_seq(s):
    is_p = s < NCH_P
    nc = jnp.where(is_p, NC_P, NC_S)
    seq = jnp.where(is_p, s // NC_P, BATCH + (s - NCH_P) // NC_S)
    c = jnp.where(is_p, s % NC_P, (s - NCH_P) % NC_S)
    return seq, c, nc, s - c


def _chunk_mirror(s):
    _, c, nc, start = _chunk_seq(s)
    return start + nc - 1 - c


def _ssd_dt(dt_raw, dt_bias, a_row):
    z = dt_raw + dt_bias
    dt = jnp.maximum(z, 0.0) + jnp.log(1.0 + jnp.exp(-jnp.abs(z)))
    la = dt * a_row
    r = lax.broadcasted_iota(jnp.int32, (Q, Q), 0)
    cc = lax.broadcasted_iota(jnp.int32, (Q, Q), 1)
    tri = (cc <= r).astype(BF16)
    hi, lo = _split_bf16(la)
    return dt, la, _mm(tri, hi) + _mm(tri, lo)


def _expand(v, e32_ref):
    hi, lo = _split_bf16(v)
    return _mm(hi, e32_ref[...]) + _mm(lo, e32_ref[...])


def _ssd1_kernel(xf_ref, xb_ref, df_ref, db_ref, dtb_ref, a_ref, e32_ref, h0_ref, hf_o, hb_o, fin_o, hf_sc, hb_sc):
    s = pl.program_id(0)
    _, c, nc, _ = _chunk_seq(s)

    @pl.when(c == 0)
    def _():
        hf_sc[...] = h0_ref[0, 0]
        hb_sc[...] = h0_ref[0, 1]

    hf_o[0] = hf_sc[...].astype(BF16)
    hb_o[0] = hb_sc[...].astype(BF16)

    lane = lax.broadcasted_iota(jnp.int32, (Q, LANE), 1)
    is_f = lane < C_HEADS
    dt_f, _, ac_f = _ssd_dt(df_ref[...], dtb_ref[...], a_ref[...])
    tot_f = ac_f[Q - 1:Q, :]
    dt_b, la_b, ac_b = _ssd_dt(db_ref[...], dtb_ref[...], a_ref[...])
    tot_b = ac_b[Q - 1:Q, :]
    w = jnp.where(is_f, jnp.exp(tot_f - ac_f) * dt_f, jnp.exp(ac_b - la_b) * dt_b)
    is_f8 = lax.broadcasted_iota(jnp.int32, (8, LANE), 1) < C_HEADS
    dec = jnp.where(is_f8, jnp.exp(jnp.broadcast_to(tot_f, (8, LANE))), jnp.exp(jnp.broadcast_to(tot_b, (8, LANE))))
    w_x = _expand(w, e32_ref)
    dec_x = _expand(dec, e32_ref)[0:1, :]
    for d, (x_ref, st) in enumerate(((xf_ref, hf_sc), (xb_ref, hb_sc))):
        xw = (x_ref[:, 0:C_INNER] * w_x[:, d * C_INNER:(d + 1) * C_INNER]).astype(BF16)
        for g in range(C_GROUPS):
            bm_t = x_ref[:, C_INNER + g * C_STATE:C_INNER + (g + 1) * C_STATE].T.astype(BF16)
            cs = slice(g * C_HPG * C_HD, (g + 1) * C_HPG * C_HD)
            dcs = slice(d * C_INNER + g * C_HPG * C_HD, d * C_INNER + (g + 1) * C_HPG * C_HD)
            st[:, cs] = st[:, cs] * dec_x[:, dcs] + _mm(bm_t, xw[:, cs])

    @pl.when((c == nc - 1) & (s < NCH_P))
    def _():
        for d, st in enumerate((hf_sc, hb_sc)):
            for k in range(C_INNER // LANE):
                fin_o[0, d, k * LANE:(k + 1) * LANE, :] = st[:, k * LANE:(k + 1) * LANE].T


def _ssd1(xa, proj, dtb, a_row, e32, h0):
    dcol = OD_DT // LANE
    h0_idx = lambda s: (jnp.where(s < NCH_P, 0, 1 + (s - NCH_P) // NC_S), 0, 0, 0)
    return pl.pallas_call(
        _ssd1_kernel,
        grid=(NCH,),
        in_specs=[pl.BlockSpec((Q, C_CONV_DIM), lambda s: (s, 0)),
                  pl.BlockSpec((Q, C_CONV_DIM), lambda s: (_chunk_mirror(s), 0)),
                  pl.BlockSpec((Q, LANE), lambda s: (s, dcol)),
                  pl.BlockSpec((Q, LANE), lambda s: (_chunk_mirror(s), dcol)),
                  pl.BlockSpec((1, LANE), lambda s: (0, 0)),
                  pl.BlockSpec((1, LANE), lambda s: (0, 0)),
                  pl.BlockSpec((LANE, 2 * C_INNER), lambda s: (0, 0)),
                  pl.BlockSpec((1, 2, C_STATE, C_INNER), h0_idx)],
        out_specs=[pl.BlockSpec((1, C_STATE, C_INNER), lambda s: (s, 0, 0)),
                   pl.BlockSpec((1, C_STATE, C_INNER), lambda s: (_chunk_mirror(s), 0, 0)),
                   pl.BlockSpec((1, 2, C_INNER, C_STATE), lambda s: (jnp.minimum(_chunk_seq(s)[0], BATCH - 1), 0, 0, 0))],
        out_shape=[jax.ShapeDtypeStruct((NCH, C_STATE, C_INNER), BF16),
                   jax.ShapeDtypeStruct((NCH, C_STATE, C_INNER), BF16),
                   jax.ShapeDtypeStruct((BATCH, 2, C_INNER, C_STATE), F32)],
        scratch_shapes=[pltpu.VMEM((C_STATE, C_INNER), F32)] * 2,
        compiler_params=_cp(("arbitrary",)),
        name="ssd_states",
    )(xa, xa, proj, proj, dtb, a_row, e32, h0)


def _ssd2_kernel(xa_ref, d_ref, z_ref, hf_ref, hb_ref, dtb_ref, a_ref, e32_ref, dsk_ref, gn_ref, o_ref, y_sc):
    dt, la, acum = _ssd_dt(d_ref[...], dtb_ref[...], a_ref[...])
    tot = acum[Q - 1:Q, :]
    excl = acum - la
    lane = lax.broadcasted_iota(jnp.int32, (Q, LANE), 1)
    is_f = lane < C_HEADS
    logdt = jnp.log(dt)
    col = jnp.where(is_f, acum, excl)
    row_t = jnp.where(is_f, acum - logdt, excl + logdt).T
    dec_x = _expand(jnp.where(is_f, jnp.exp(acum), jnp.exp(tot - excl)), e32_ref)
    x = xa_ref[:, 0:C_INNER]
    xb16 = x.astype(BF16)
    ti = lax.broadcasted_iota(jnp.int32, (Q, Q), 0)
    si = lax.broadcasted_iota(jnp.int32, (Q, Q), 1)
    low = si <= ti
    upp = si >= ti
    lane_q = lax.broadcasted_iota(jnp.int32, (Q, LANE), 1)
    gw = C_HPG * C_HD
    for g in range(C_GROUPS):
        bm = xa_ref[:, C_INNER + g * C_STATE:C_INNER + (g + 1) * C_STATE].astype(BF16)
        cm = xa_ref[:, C_INNER + (C_GROUPS + g) * C_STATE:C_INNER + (C_GROUPS + g + 1) * C_STATE].astype(BF16)
        gmat = _mm_nt(cm, bm)
        cs = slice(g * gw, (g + 1) * gw)
        ys = (_mm(cm, hf_ref[0, :, cs]) * dec_x[:, cs]
              + _mm(cm, hb_ref[0, :, cs]) * dec_x[:, C_INNER + g * gw:C_INNER + (g + 1) * gw])
        for pr in range(C_HPG // 2):
            xpair = xb16[:, g * gw + pr * LANE:g * gw + (pr + 1) * LANE]
            res = []
            for e in range(2):
                h = g * C_HPG + 2 * pr + e
                hb_ = C_HEADS + h
                m_f = jnp.where(low, jnp.exp(col[:, h:h + 1] - row_t[h:h + 1, :]), 0.0)
                m_b = jnp.where(upp, jnp.exp(row_t[hb_:hb_ + 1, :] - col[:, hb_:hb_ + 1]), 0.0)
                res.append(_mm((gmat * (m_f + m_b)).astype(BF16), xpair))
            y_sc[:, g * gw + pr * LANE:g * gw + (pr + 1) * LANE] = (
                jnp.where(lane_q < C_HD, res[0], res[1]) + ys[:, pr * LANE:(pr + 1) * LANE])
    y = y_sc[...] + dsk_ref[...] * x
    y = y * _silu(z_ref[...])
    y = y * lax.rsqrt(jnp.mean(y * y, axis=-1, keepdims=True) + EPS) * gn_ref[...]
    o_ref[...] = y.astype(BF16)


def _ssd2(xa, proj, hf, hb, dtb, a_row, e32, dsk, gn):
    dcol = OD_DT // LANE
    one = lambda w: pl.BlockSpec((1, w), lambda s: (0, 0))
    return pl.pallas_call(
        _ssd2_kernel,
        grid=(NCH,),
        in_specs=[pl.BlockSpec((Q, C_CONV_DIM), lambda s: (s, 0)),
                  pl.BlockSpec((Q, LANE), lambda s: (s, dcol)),
                  pl.BlockSpec((Q, C_INNER), lambda s: (s, 0)),
                  pl.BlockSpec((1, C_STATE, C_INNER), lambda s: (s, 0, 0)),
                  pl.BlockSpec((1, C_STATE, C_INNER), lambda s: (s, 0, 0)),
                  one(LANE), one(LANE),
                  pl.BlockSpec((LANE, 2 * C_INNER), lambda s: (0, 0)),
                  one(C_INNER), one(C_INNER)],
        out_specs=pl.BlockSpec((Q, C_INNER), lambda s: (s, 0)),
        out_shape=jax.ShapeDtypeStruct((T_ALL, C_INNER), BF16),
        scratch_shapes=[pltpu.VMEM((Q, C_INNER), F32)],
        compiler_params=_cp(("parallel",)),
        name="ssd_out",
    )(xa, proj, proj, hf, hb, dtb, a_row, e32, dsk, gn)


S5_MM_STEPS = 32


def _cmul(ar, ai, br, bi):
    return ar * br - ai * bi, ar * bi + ai * br


def _s5_kernel(u_ref, wx_ref, cw_ref, a_ref, dsk_ref, h0_ref, y_ref, st_ref, xs, *, nseg):
    ct = pl.program_id(0)
    quarter = ct % 4
    nslab = S5_STEPS // S5_MM_STEPS
    rows = S5_MM_STEPS * S5_CHAINS

    def fill(k, carry):
        r0 = pl.multiple_of(k * S5_MM_STEPS, S5_MM_STEPS)
        ub = u_ref[pl.ds(r0, S5_MM_STEPS)].reshape(rows, LANE).astype(BF16)
        xs[pl.ds(r0, S5_MM_STEPS)] = _mm(ub, wx_ref[0]).reshape(S5_MM_STEPS, S5_CHAINS, 4 * LANE)
        return carry
    lax.fori_loop(0, nslab, fill, 0)

    shape = (S5_CHAINS, LANE)
    a = a_ref[0]
    afr, afi, abr, abi = (jnp.broadcast_to(a[k:k + 1, :], shape) for k in range(4))

    def scan(init, store):
        def body(i, carry):
            hfr, hfi, hbr, hbi = carry
            j = S5_STEPS - 1 - i
            pr, pi = _cmul(afr, afi, hfr, hfi)
            nfr = pr + xs[i, :, 0:LANE]
            nfi = pi + xs[i, :, LANE:2 * LANE]
            pr, pi = _cmul(abr, abi, hbr, hbi)
            nbr = pr + xs[j, :, 2 * LANE:3 * LANE]
            nbi = pi + xs[j, :, 3 * LANE:4 * LANE]
            if store:
                xs[i, :, 0:LANE] = nfr
                xs[i, :, LANE:2 * LANE] = nfi
                xs[j, :, 2 * LANE:3 * LANE] = nbr
                xs[j, :, 3 * LANE:4 * LANE] = nbi
            return nfr, nfi, nbr, nbi
        return lax.fori_loop(0, S5_STEPS, body, init)

    zero = jnp.zeros(shape, F32)
    if nseg == 1:
        init = (zero, zero, zero, zero)
    else:
        efr, efi, ebr, ebi = scan((zero, zero, zero, zero), store=False)
        pfr, pfi, pbr, pbi = afr, afi, abr, abi
        for _ in range(int(math.log2(S5_STEPS))):
            pfr, pfi = _cmul(pfr, pfi, pfr, pfi)
            pbr, pbi = _cmul(pbr, pbi, pbr, pbi)
        row = lax.broadcasted_iota(jnp.int32, shape, 0) % nseg
        first = row == 0
        last = row == nseg - 1
        h0 = h0_ref[...]
        cfr, cfi, cbr, cbi = zero, zero, zero, zero
        dn = lambda v: pltpu.roll(v, 1, 0)
        up = lambda v: pltpu.roll(v, S5_CHAINS - 1, 0)
        for _ in range(nseg):
            pr, pi = _cmul(pfr, pfi, dn(cfr), dn(cfi))
            cfr = jnp.where(first, h0[0], pr + dn(efr))
            cfi = jnp.where(first, h0[1], pi + dn(efi))
            pr, pi = _cmul(pbr, pbi, up(cbr), up(cbi))
            cbr = jnp.where(last, h0[2], pr + up(ebr))
            cbi = jnp.where(last, h0[3], pi + up(ebi))
        init = (cfr, cfi, cbr, cbi)
    fin = scan(init, store=True)
    for k in range(4):
        st_ref[k] = fin[k]

    def emit(k, carry):
        r0 = pl.multiple_of(k * S5_MM_STEPS, S5_MM_STEPS)
        blk = xs[pl.ds(r0, S5_MM_STEPS)]
        s_re = blk[:, :, 0:LANE] + blk[:, :, 2 * LANE:3 * LANE]
        s_im = blk[:, :, LANE:2 * LANE] + blk[:, :, 3 * LANE:4 * LANE]
        s2 = jnp.concatenate([s_re, s_im], axis=-1).reshape(rows, 2 * LANE).astype(BF16)
        yv = _mm(s2, cw_ref[0]).reshape(S5_MM_STEPS, S5_CHAINS, LANE)

        @pl.when(quarter == 0)
        def _():
            y_ref[pl.ds(r0, S5_MM_STEPS)] = yv + dsk_ref[...] * u_ref[pl.ds(r0, S5_MM_STEPS)]

        @pl.when(quarter != 0)
        def _():
            y_ref[pl.ds(r0, S5_MM_STEPS)] += yv
        return carry
    lax.fori_loop(0, nslab, emit, 0)


def _s5(u_perm, sw, h0, nseg):
    nct = D_GROUPS // 2
    return pl.pallas_call(
        functools.partial(_s5_kernel, nseg=nseg),
        grid=(nct,),
        in_specs=[pl.BlockSpec((S5_STEPS, S5_CHAINS, LANE), lambda ct: (0, 0, ct // 4)),
                  pl.BlockSpec((1, LANE, 4 * LANE), lambda ct: (ct, 0, 0)),
                  pl.BlockSpec((1, 2 * LANE, LANE), lambda ct: (ct, 0, 0)),
                  pl.BlockSpec((1, 8, LANE), lambda ct: (ct, 0, 0)),
                  pl.BlockSpec((1, LANE), lambda ct: (0, ct // 4)),
                  pl.BlockSpec((4, S5_CHAINS, LANE), lambda ct: (0, 0, ct))],
        out_specs=[pl.BlockSpec((S5_STEPS, S5_CHAINS, LANE), lambda ct: (0, 0, ct // 4)),
                   pl.BlockSpec((4, S5_CHAINS, LANE), lambda ct: (0, 0, ct))],
        out_shape=[jax.ShapeDtypeStruct((S5_STEPS, S5_CHAINS, D_WIDTH), F32),
                   jax.ShapeDtypeStruct((4, S5_CHAINS, nct * LANE), F32)],
        scratch_shapes=[pltpu.VMEM((S5_STEPS, S5_CHAINS, 4 * LANE), F32)],
        compiler_params=_cp(("arbitrary",)),
        name="s5_seg" if nseg > 1 else "s5",
    )(u_perm, sw["wx"], sw["cw"], sw["a"], sw["dsk"], h0)


def _pad_last(a, n):
    return jnp.pad(a, [(0, 0)] * (a.ndim - 1) + [(0, n - a.shape[-1])])


def _even_weights(w_in, q_lora_g, w_uq, kv_lora_g, w_ukv, q_norm_g, k_norm_g, bq_g, bk_g):
    d = w_in.shape[0]
    s0, s1, s2, s3, s4 = 384, 640, 672, 1184, 1312
    kr = jnp.pad(w_in[:, s1:s2], ((0, 0), (A_NOPE, LANE - A_QK)))
    heads = lambda w, n: _pad_last(w.reshape(d, n, B_HD), LANE).reshape(d, n * LANE)
    w_perm = jnp.concatenate([w_in[:, :s0], w_in[:, s0:s1], kr, heads(w_in[:, s2:s3], B_HEADS),
                              heads(w_in[:, s3:s4], B_KV_HEADS), heads(w_in[:, s4:], B_KV_HEADS)], axis=1)
    row = lambda g: _pad_last(g.reshape(1, -1), LANE)
    return {
        "w_in": w_perm.astype(BF16),
        "gql": q_lora_g.reshape(1, -1),
        "wuq": _pad_last(w_uq, LANE).reshape(A_Q_LORA, A_HEADS * LANE).astype(BF16),
        "gqn": row(q_norm_g),
        "gkvl": kv_lora_g.reshape(1, -1),
        "wk": _pad_last(w_ukv[:, :, :A_NOPE], LANE).reshape(A_KV_LORA, A_HEADS * LANE).astype(BF16),
        "wv": _pad_last(w_ukv[:, :, A_NOPE:], LANE).reshape(A_KV_LORA, A_HEADS * LANE).astype(BF16),
        "gkn": row(k_norm_g),
        "gbq": row(bq_g),
        "gbk": row(bk_g),
    }


def _rope_tables():
    rows = DEC_SEQ // GRID_W

    def angles(dim):
        half = dim // 2
        inv = ROPE_THETA ** (-jnp.arange(0, half, 2, dtype=F32) / half)
        r = jnp.repeat(jnp.arange(rows, dtype=F32), GRID_W)
        col = jnp.tile(jnp.arange(GRID_W, dtype=F32), rows)
        return jnp.concatenate([r[:, None] * inv, col[:, None] * inv], axis=-1)

    def lanes(dim, start):
        ang = angles(dim)
        cos = jnp.repeat(jnp.cos(ang), 2, axis=-1)
        sin = jnp.repeat(jnp.sin(ang), 2, axis=-1) * jnp.tile(jnp.array([-1.0, 1.0], F32), dim // 2)
        pad = ((0, 0), (start, LANE - start - dim))
        return jnp.pad(cos, pad, constant_values=1.0), jnp.pad(sin, pad)

    cos_a, sin_a = lanes(A_ROPE, A_NOPE)
    cos_b, sin_b = lanes(B_HD, 0)
    return cos_a, sin_a, cos_b, sin_b


def _odd_weights(w_in, dt_bias, a_log, c_d_skip):
    w_perm = jnp.concatenate([w_in[:, :C_INNER], w_in[:, C_INNER:C_INNER + C_CONV_DIM],
                              w_in[:, C_INNER + C_CONV_DIM + 2 * C_HEADS:],
                              _pad_last(w_in[:, C_INNER + C_CONV_DIM:C_INNER + C_CONV_DIM + 2 * C_HEADS], LANE)], axis=1)
    lane = jnp.arange(LANE)
    col = jnp.arange(2 * C_INNER)
    e32 = ((lane[:, None] < 2 * C_HEADS) & (lane[:, None] == col[None, :] // C_HD)).astype(BF16)
    return {
        "w_in": w_perm.astype(BF16),
        "dtb": _pad_last(dt_bias.reshape(1, -1), LANE),
        "a_row": _pad_last(-jnp.exp(a_log).reshape(1, -1), LANE),
        "e32": e32,
        "dsk": jnp.repeat(c_d_skip, C_HD).reshape(1, C_INNER),
    }


def _s5_weights(lam_re, lam_im, log_dt, b_re, b_im, c_re, c_im, d_skip):
    nct = D_GROUPS // 2
    dt = jnp.exp(log_dt)[..., None]
    mag = jnp.exp(lam_re * dt)
    ar, ai = mag * jnp.cos(lam_im * dt), mag * jnp.sin(lam_im * dt)
    den = lam_re * lam_re + lam_im * lam_im
    kr = ((ar - 1.0) * lam_re + ai * lam_im) / den
    ki = (ai * lam_re - (ar - 1.0) * lam_im) / den
    xr = kr[..., None] * b_re[None] - ki[..., None] * b_im[None]
    xi = kr[..., None] * b_im[None] + ki[..., None] * b_re[None]
    xc = jnp.stack([xr[0], xi[0], xr[1], xi[1]]).reshape(4, nct, 2, D_STATE, D_GROUP_SIZE)
    quarter = (jnp.arange(nct)[:, None] % 4 == jnp.arange(4)[None, :]).astype(F32)
    eye2 = jnp.eye(2, dtype=F32)
    wx = jnp.einsum("tq,gh,ktgnj->tqgjkhn", quarter, eye2, xc).reshape(nct, LANE, 4 * LANE)
    cc = jnp.stack([c_re, -c_im]).reshape(2, nct, 2, D_GROUP_SIZE, D_STATE)
    cw = jnp.einsum("tq,gh,rtgcn->trhnqgc", quarter, eye2, cc).reshape(nct, 2 * LANE, LANE)
    planes = jnp.stack([ar[0], ai[0], ar[1], ai[1]]).reshape(4, nct, LANE).transpose(1, 0, 2)
    return {"wx": wx.astype(BF16), "cw": cw.astype(BF16), "a": jnp.pad(planes, ((0, 0), (0, 4), (0, 0))),
            "dsk": d_skip.reshape(1, D_WIDTH)}


def _even_layer(x, mod, norm1_g, norm2_g, ew, w_out, ffn_w, caches, rope_tabs):
    c_ckv, c_krope, c_kb, c_vb = caches
    proj = _modmm(x, mod, norm1_g, ew["w_in"])
    qa_p, ka_p, va_p, qb_p, kb_p, vb_p, ckv_c, kr_c, kbn_c, vb_c = _evenprep(proj, 0, TP, ew, None)
    qa_s, ka_s, va_s, qb_s, kb_s, vb_s = _evenprep(proj, TP, TS, ew, rope_tabs)
    n_ctx = DEC_BATCH * PAST_LEN
    kr_blk = jnp.pad(c_krope.reshape(n_ctx, A_ROPE), ((0, 0), (A_NOPE, LANE - A_QK)))
    ka_c, va_c = _ctxkv(c_ckv.reshape(n_ctx, A_KV_LORA), kr_blk, ew)
    heads_b = lambda a: _pad_last(a.transpose(0, 2, 1, 3), LANE).reshape(n_ctx, B_KV_HEADS * LANE)
    ones_col = (jnp.arange(B_KV_HEADS * LANE) % LANE == A_V).astype(F32)
    sa, sb = A_QK ** -0.5, B_HD ** -0.5
    oa = jnp.zeros((T_ALL, A_HEADS * A_V), BF16)
    ob = jnp.zeros((T_ALL, B_HEADS * A_V), BF16)
    oa = _attn(qa_p, ka_p, va_p, None, oa, nb=BATCH, lq=SEQ, lk=SEQ, nheads=A_HEADS, hp=2, shared_kv=False,
               scale=sa, tq=SEQ)
    ob = _attn(qb_p, kb_p, vb_p, None, ob, nb=BATCH, lq=SEQ, lk=SEQ, nheads=B_HEADS, hp=4, shared_kv=True,
               scale=sb, tq=SEQ)
    oa = _attn(qa_s, ka_s, va_s, (ka_c, va_c), oa, nb=DEC_BATCH, lq=DEC_SEQ, lk=DEC_SEQ, nheads=A_HEADS, hp=2,
               shared_kv=False, scale=sa, tq=ATT_TQ)
    ctx_b = (heads_b(c_kb).astype(BF16), (heads_b(c_vb) + ones_col).astype(BF16))
    ob = _attn(qb_s, kb_s, vb_s, ctx_b, ob, nb=DEC_BATCH, lq=DEC_SEQ, lk=DEC_SEQ,
               nheads=B_HEADS, hp=4, shared_kv=True, scale=sb, tq=ATT_TQ)
    x = _outproj(x, mod, oa, ob, w_out[:A_HEADS * A_V], w_out[A_HEADS * A_V:])
    x = _ffn(x, mod, norm2_g, *ffn_w)
    new_cache = (ckv_c.reshape(BATCH, SEQ, A_KV_LORA), kr_c[:, A_NOPE:A_QK].reshape(BATCH, SEQ, A_ROPE),
                 kbn_c, vb_c)
    return x, new_cache


def _odd_layer(x, mod, norm1_g, norm2_g, ow, sw, conv_w, conv_b, c_norm_g, w_glu, w_out, moe_w, states):
    st_c, st_re, st_im = states
    proj = _modmm(x, mod, norm1_g, ow["w_in"])
    xa = _conv(proj, conv_w, conv_b)
    h0 = st_c.transpose(0, 1, 4, 2, 3).reshape(DEC_BATCH, 2, C_STATE, C_INNER)
    h0 = jnp.concatenate([jnp.zeros((1,) + h0.shape[1:], F32), h0], axis=0)
    hf, hb, fin = _ssd1(xa, proj, ow["dtb"], ow["a_row"], ow["e32"], h0)
    yc = _ssd2(xa, proj, hf, hb, ow["dtb"], ow["a_row"], ow["e32"], ow["dsk"], c_norm_g.reshape(1, C_INNER))
    u = proj[:, OD_U:OD_U + D_WIDTH]
    u_p = u[:TP].reshape(BATCH, SEQ, D_WIDTH).transpose(1, 0, 2)
    u_s = u[TP:].reshape(DEC_BATCH * S5_SEG, S5_STEPS, D_WIDTH).transpose(1, 0, 2)
    zeros_h0 = jnp.zeros((4, S5_CHAINS, D_GROUPS * D_STATE), F32)
    h0_s = jnp.stack([st_re[:, 0], st_im[:, 0], st_re[:, 1], st_im[:, 1]]).reshape(4, DEC_BATCH, D_GROUPS * D_STATE)
    h0_s = jnp.repeat(h0_s, S5_SEG, axis=1)
    yd_p, fin_d = _s5(u_p, sw, zeros_h0, 1)
    yd_s, _ = _s5(u_s, sw, h0_s, S5_SEG)
    yd = jnp.concatenate([yd_p.transpose(1, 0, 2).reshape(TP, D_WIDTH),
                          yd_s.transpose(1, 0, 2).reshape(TS, D_WIDTH)], axis=0)
    x = _outproj(x, mod, yc, yd, w_out[:C_INNER], w_out[C_INNER:], wg=w_glu)
    x = _ffn(x, mod, norm2_g, *moe_w[:4], router=moe_w[4])
    new_c = fin.reshape(BATCH, 2, C_HEADS, C_HD, C_STATE)
    fin_d = fin_d.reshape(4, BATCH, D_GROUPS, D_STATE)
    new_re = jnp.stack([fin_d[0], fin_d[2]], axis=1)
    new_im = jnp.stack([fin_d[1], fin_d[3]], axis=1)
    return x, (new_c, new_re, new_im)


def kernel(x_prompt, x_sample, cache_a_ckv, cache_a_krope, cache_b_k, cache_b_v, state_c_ssm, state_d_re, state_d_im, c, c_ctx, w_mod, b_mod, norm1_g, norm2_g, ev_w_in, ev_a_q_lora_g, ev_a_w_uq, ev_a_kv_lora_g, ev_a_w_ukv, ev_a_q_norm_g, ev_a_k_norm_g, ev_b_q_norm_g, ev_b_k_norm_g, ev_w_out, ffn_w1, ffn_w3, ffn_w2, od_w_in, od_conv_w, od_conv_b, od_dt_bias, od_a_log, od_c_d_skip, od_c_norm_g, od_lam_re, od_lam_im, od_log_dt, od_b_re, od_b_im, od_c_re, od_c_im, od_d_skip, od_w_glu, od_w_out, moe_router_w, moe_router_b, moe_w1, moe_w3, moe_w2):
    x = jnp.concatenate([x_prompt.reshape(TP, D_MODEL), x_sample.reshape(TS, D_MODEL)], axis=0)
    cond8 = jnp.zeros((8, D_MODEL), F32).at[0].set(c_ctx).at[1:1 + DEC_BATCH].set(c)
    mods = _adaln(cond8, w_mod, b_mod)[:, :1 + DEC_BATCH].reshape(DEPTH, 1 + DEC_BATCH, 6, D_MODEL)
    mods = jnp.pad(mods, ((0, 0), (0, 0), (0, 2), (0, 0)))
    rope_tabs = _rope_tables()
    ffn_bf = (ffn_w1.astype(BF16), ffn_w3.astype(BF16), ffn_w2.astype(BF16))
    moe_bf = (moe_w1.astype(BF16), moe_w3.astype(BF16), moe_w2.astype(BF16))
    ev_new, od_new = [], []
    for layer in range(DEPTH):
        mod = mods[layer]
        if layer % 2 == 0:
            e = layer // 2
            ew = _even_weights(ev_w_in[e], ev_a_q_lora_g[e], ev_a_w_uq[e], ev_a_kv_lora_g[e], ev_a_w_ukv[e],
                               ev_a_q_norm_g[e], ev_a_k_norm_g[e], ev_b_q_norm_g[e], ev_b_k_norm_g[e])
            ffn_w = ffn_bf + (e,)
            x, new = _even_layer(x, mod, norm1_g[layer], norm2_g[layer], ew, ev_w_out[e].astype(BF16), ffn_w,
                                 (cache_a_ckv[:, e], cache_a_krope[:, e], cache_b_k[:, e], cache_b_v[:, e]),
                                 rope_tabs)
            ev_new.append(new)
        else:
            o = layer // 2
            ow = _odd_weights(od_w_in[o], od_dt_bias[o], od_a_log[o], od_c_d_skip[o])
            sw = _s5_weights(od_lam_re[o], od_lam_im[o], od_log_dt[o], od_b_re[o], od_b_im[o], od_c_re[o],
                             od_c_im[o], od_d_skip[o])
            rw_hi = _pad_last(moe_router_w[o], LANE).astype(BF16)
            rw_lo = (_pad_last(moe_router_w[o], LANE) - rw_hi.astype(F32)).astype(BF16)
            moe_w = moe_bf + (o, (rw_hi, rw_lo, _pad_last(moe_router_b[o].reshape(1, -1), LANE)))
            x, new = _odd_layer(x, mod, norm1_g[layer], norm2_g[layer], ow, sw, od_conv_w[o], od_conv_b[o],
                                od_c_norm_g[o], od_w_glu[o].astype(BF16), od_w_out[o].astype(BF16), moe_w,
                                (state_c_ssm[:, o], state_d_re[:, o], state_d_im[:, o]))
            od_new.append(new)
    stack = lambda items, k: jnp.stack([it[k] for it in items], axis=1)
    return (x[:TP].reshape(BATCH, SEQ, D_MODEL), x[TP:].reshape(DEC_BATCH, DEC_SEQ, D_MODEL),
            stack(ev_new, 0), stack(ev_new, 1), stack(ev_new, 2), stack(ev_new, 3),
            stack(od_new, 0), stack(od_new, 1), stack(od_new, 2))
```

```python
import functools
import math

import jax
import jax.numpy as jnp
from jax import lax
from jax.experimental import pallas as pl
from jax.experimental.pallas import tpu as pltpu

F32 = jnp.float32
BF16 = jnp.bfloat16

D_MODEL = 1024
BATCH = 32
SEQ = 256
DEPTH = 4
DEC_BATCH = 2
DEC_SEQ = 4096
PAST_LEN = 256
GRID_W = 64
ROPE_THETA = 10000.0
EPS = 1e-6
A_HEADS = 8
A_NOPE = 64
A_ROPE = 32
A_QK = A_NOPE + A_ROPE
A_V = 64
A_Q_LORA = 384
A_KV_LORA = 256
B_HEADS = 8
B_KV_HEADS = 2
B_HD = 64
C_HEADS = 16
C_HD = 64
C_GROUPS = 2
C_HPG = C_HEADS // C_GROUPS
C_STATE = 128
C_INNER = C_HEADS * C_HD
C_CONV_DIM = C_INNER + 2 * C_GROUPS * C_STATE
SSD_CHUNK = 128
D_GROUP_SIZE = 16
D_GROUPS = 32
D_WIDTH = D_GROUPS * D_GROUP_SIZE
D_STATE = 64
FF_DENSE = 2816
N_EXPERTS = 8
FF_EXPERT = 1408

TP = BATCH * SEQ
TS = DEC_BATCH * DEC_SEQ
T_ALL = TP + TS
LANE = 128
LOG2E = 1.4426950408889634
VMEM_LIMIT = 56 * 1024 * 1024

EV_CQ = 0
EV_CKV = 384
EV_KR = 640
EV_QB = 768
EV_KB = EV_QB + B_HEADS * LANE
EV_VB = EV_KB + B_KV_HEADS * LANE
EV_N = EV_VB + B_KV_HEADS * LANE
OD_Z = 0
OD_XBC = C_INNER
OD_U = OD_XBC + C_CONV_DIM
OD_DT = OD_U + D_WIDTH
OD_N = OD_DT + LANE
S5_STEPS = 256
S5_CHAINS = 32
S5_SEG = DEC_SEQ // S5_STEPS
ATT_TQ = 256
ATT_CHUNKS = 2


def _cp(sem):
    return pltpu.CompilerParams(dimension_semantics=sem, vmem_limit_bytes=VMEM_LIMIT)


def _cond_idx(i, tm):
    t0 = i * tm
    return jnp.where(t0 < TP, 0, 1 + (t0 - TP) // DEC_SEQ)


def _mm(a, b):
    return jnp.dot(a, b, preferred_element_type=F32)


def _mm_nt(a, b):
    return lax.dot_general(a, b, (((1,), (1,)), ((), ())), preferred_element_type=F32)


def _split_bf16(v):
    hi = v.astype(BF16)
    lo = (v - hi.astype(F32)).astype(BF16)
    return hi, lo


def _silu(v):
    return v * jax.nn.sigmoid(v)


def _rms(v, n):
    return v * lax.rsqrt(jnp.sum(v * v, axis=-1, keepdims=True) * (1.0 / n) + EPS)


def _adaln_kernel(c_ref, w_ref, b_ref, o_ref):
    s = _silu(c_ref[...])
    o_ref[0] = _mm(s.astype(BF16), w_ref[0].astype(BF16)) + b_ref[0]


def _adaln(cond8, w_mod, b_mod):
    depth, d, n6 = w_mod.shape
    tn = 1536
    return pl.pallas_call(
        _adaln_kernel,
        grid=(depth, n6 // tn),
        in_specs=[pl.BlockSpec((8, d), lambda l, j: (0, 0)),
                  pl.BlockSpec((1, d, tn), lambda l, j: (l, 0, j)),
                  pl.BlockSpec((1, 1, tn), lambda l, j: (l, 0, j))],
        out_specs=pl.BlockSpec((1, 8, tn), lambda l, j: (l, 0, j)),
        out_shape=jax.ShapeDtypeStruct((depth, 8, n6), F32),
        compiler_params=_cp(("parallel", "parallel")),
        name="adaln",
    )(cond8, w_mod, b_mod.reshape(depth, 1, n6))


def _modulate(x, g, mod_ref, shift_row, scale_row):
    y = x * lax.rsqrt(jnp.mean(x * x, axis=-1, keepdims=True) + EPS) * g
    return y * (1.0 + mod_ref[0, scale_row:scale_row + 1, :]) + mod_ref[0, shift_row:shift_row + 1, :]


def _modmm_kernel(x_ref, mod_ref, g_ref, w_ref, o_ref):
    h = _modulate(x_ref[...], g_ref[...], mod_ref, 0, 1)
    o_ref[...] = _mm(h.astype(BF16), w_ref[...])


def _modmm(x, mod, g, w, tm=512):
    t, d = x.shape
    n = w.shape[1]
    return pl.pallas_call(
        _modmm_kernel,
        grid=(t // tm,),
        in_specs=[pl.BlockSpec((tm, d), lambda i: (i, 0)),
                  pl.BlockSpec((1, 8, d), lambda i: (_cond_idx(i, tm), 0, 0)),
                  pl.BlockSpec((1, d), lambda i: (0, 0)),
                  pl.BlockSpec((d, n), lambda i: (0, 0))],
        out_specs=pl.BlockSpec((tm, n), lambda i: (i, 0)),
        out_shape=jax.ShapeDtypeStruct((t, n), F32),
        compiler_params=_cp(("parallel",)),
        name="modmm",
    )(x, mod, g.reshape(1, d), w)


def _swap_pairs(v):
    lane = lax.broadcasted_iota(jnp.int32, v.shape, 1)
    nxt = pltpu.roll(v, LANE - 1, 1)
    prv = pltpu.roll(v, 1, 1)
    return jnp.where((lane & 1) == 0, nxt, prv)


def _rope(v, cos, sin):
    return v * cos + _swap_pairs(v) * sin


def _with_ones(v):
    lane = lax.broadcasted_iota(jnp.int32, v.shape, 1)
    return jnp.where((lane & (LANE - 1)) == A_V, 1.0, v)


def _mla_kv(ckvn, kr_blk, wk_ref, wv_ref, gkn, rope_tabs, ka_o, va_o):
    cb = ckvn.astype(BF16)
    kn = _mm(cb, wk_ref[...])
    va_o[...] = _with_ones(_mm(cb, wv_ref[...])).astype(BF16)
    for h in range(A_HEADS):
        sl = slice(h * LANE, (h + 1) * LANE)
        kh = _rms(kn[:, sl] + kr_blk, A_QK) * gkn
        if rope_tabs is not None:
            kh = _rope(kh, rope_tabs[0], rope_tabs[1])
        ka_o[:, sl] = kh.astype(BF16)


def _evenprep_kernel(*refs, rope):
    it = iter(refs)
    proj = next(it)
    if rope:
        cos_a, sin_a, cos_b, sin_b = (next(it)[...] for _ in range(4))
    gql, wuq, gqn, gkvl, wk, wv, gkn, gbq, gbk = (next(it) for _ in range(9))
    qa_o, ka_o, va_o, qb_o, kb_o, vb_o = (next(it) for _ in range(6))
    if not rope:
        ckv_o, kr_o, kbn_o, vbf_o = (next(it) for _ in range(4))

    cqn = _rms(proj[:, EV_CQ:EV_CQ + A_Q_LORA], A_Q_LORA) * gql[...]
    q = _mm(cqn.astype(BF16), wuq[...])
    for h in range(A_HEADS):
        sl = slice(h * LANE, (h + 1) * LANE)
        qh = _rms(q[:, sl], A_QK) * gqn[...]
        if rope:
            qh = _rope(qh, cos_a, sin_a)
        qa_o[:, sl] = (qh * (A_QK ** -0.5 * LOG2E)).astype(BF16)

    ckvn = _rms(proj[:, EV_CKV:EV_CKV + A_KV_LORA], A_KV_LORA) * gkvl[...]
    kr_blk = proj[:, EV_KR:EV_KR + LANE]
    _mla_kv(ckvn, kr_blk, wk, wv, gkn[...], (cos_a, sin_a) if rope else None, ka_o, va_o)

    for h in range(B_HEADS):
        sl = slice(h * LANE, (h + 1) * LANE)
        qh = _rms(proj[:, EV_QB + h * LANE:EV_QB + (h + 1) * LANE], B_HD) * gbq[...]
        if rope:
            qh = _rope(qh, cos_b, sin_b)
        qb_o[:, sl] = (qh * (B_HD ** -0.5 * LOG2E)).astype(BF16)
    for j in range(B_KV_HEADS):
        sl = slice(j * LANE, (j + 1) * LANE)
        kh = _rms(proj[:, EV_KB + j * LANE:EV_KB + (j + 1) * LANE], B_HD) * gbk[...]
        if not rope:
            kbn_o[0, j] = kh[:, :B_HD]
        else:
            kh = _rope(kh, cos_b, sin_b)
        kb_o[:, sl] = kh.astype(BF16)
    vb = proj[:, EV_VB:EV_VB + B_KV_HEADS * LANE]
    vb_o[...] = _with_ones(vb).astype(BF16)
    if not rope:
        for j in range(B_KV_HEADS):
            vbf_o[0, j] = vb[:, j * LANE:j * LANE + B_HD]
        ckv_o[...] = ckvn
        kr_o[...] = kr_blk


def _evenprep(proj, row_off, nrows, ew, rope_tabs):
    tm = 256
    rope = rope_tabs is not None
    off = row_off // tm
    full = lambda a: pl.BlockSpec(a.shape, lambda i: (0,) * a.ndim)
    rows = lambda w: pl.BlockSpec((tm, w), lambda i: (i, 0))
    ins = [proj]
    in_specs = [pl.BlockSpec((tm, EV_N), lambda i: (i + off, 0))]
    if rope:
        ins += list(rope_tabs)
        in_specs += [pl.BlockSpec((tm, LANE), lambda i: (i % (DEC_SEQ // tm), 0))] * 4
    ws = [ew["gql"], ew["wuq"], ew["gqn"], ew["gkvl"], ew["wk"], ew["wv"], ew["gkn"], ew["gbq"], ew["gbk"]]
    ins += ws
    in_specs += [full(a) for a in ws]
    hw = A_HEADS * LANE
    kvw = B_KV_HEADS * LANE
    out_shape = [jax.ShapeDtypeStruct((nrows, hw), BF16)] * 4 + [jax.ShapeDtypeStruct((nrows, kvw), BF16)] * 2
    out_specs = [rows(hw)] * 4 + [rows(kvw)] * 2
    if not rope:
        assert tm == SEQ
        cache_b = jax.ShapeDtypeStruct((nrows // SEQ, B_KV_HEADS, SEQ, B_HD), F32)
        cache_b_spec = pl.BlockSpec((1, B_KV_HEADS, SEQ, B_HD), lambda i: (i, 0, 0, 0))
        out_shape += [jax.ShapeDtypeStruct((nrows, A_KV_LORA), F32), jax.ShapeDtypeStruct((nrows, LANE), F32),
                      cache_b, cache_b]
        out_specs += [rows(A_KV_LORA), rows(LANE), cache_b_spec, cache_b_spec]
    return pl.pallas_call(
        functools.partial(_evenprep_kernel, rope=rope),
        grid=(nrows // tm,),
        in_specs=in_specs,
        out_specs=out_specs,
        out_shape=out_shape,
        compiler_params=_cp(("parallel",)),
        name="evenprep_rope" if rope else "evenprep",
    )(*ins)


def _ctxkv_kernel(ckv_ref, kr_ref, wk, wv, gkn, ka_o, va_o):
    _mla_kv(ckv_ref[...], kr_ref[...], wk, wv, gkn[...], None, ka_o, va_o)


def _ctxkv(ckvn, kr_blk, ew):
    n = ckvn.shape[0]
    hw = A_HEADS * LANE
    return pl.pallas_call(
        _ctxkv_kernel,
        out_shape=[jax.ShapeDtypeStruct((n, hw), BF16)] * 2,
        compiler_params=pltpu.CompilerParams(vmem_limit_bytes=VMEM_LIMIT),
        name="ctxkv",
    )(ckvn, kr_blk, ew["wk"], ew["wv"], ew["gkn"])


def _attn_kernel(*refs, hp, q_per_kv, has_ctx):
    if has_ctx:
        q_ref, kn_ref, vn_ref, kc_ref, vc_ref, _, o_ref = refs
    else:
        q_ref, kn_ref, vn_ref, _, o_ref = refs
    outs = []
    for h in range(hp):
        kv = slice(h // q_per_kv * LANE, (h // q_per_kv + 1) * LANE)
        q = q_ref[:, h * LANE:(h + 1) * LANE]
        lk = kn_ref.shape[0]
        ck = lk // ATT_CHUNKS if has_ctx else lk
        parts = [(kn_ref, vn_ref, k0, ck) for k0 in range(0, lk, ck)]
        if has_ctx:
            parts.append((kc_ref, vc_ref, 0, kc_ref.shape[0]))
        m = acc = None
        for k_ref, v_ref, k0, n in parts:
            s = _mm_nt(q, k_ref[k0:k0 + n, kv])
            m_c = jnp.max(s, axis=-1, keepdims=True)
            m_new = m_c if m is None else jnp.maximum(m, m_c)
            p = jnp.exp2(s - m_new)
            a_c = _mm(p.astype(BF16), v_ref[k0:k0 + n, kv])
            acc = a_c if m is None else jnp.exp2(m - m_new) * acc + a_c
            m = m_new
        outs.append(acc * (1.0 / acc[:, A_V:A_V + 1]))
    for p2 in range(hp // 2):
        pair = jnp.concatenate([outs[2 * p2][:, :A_V], outs[2 * p2 + 1][:, :A_V]], axis=-1)
        o_ref[:, p2 * LANE:(p2 + 1) * LANE] = pair.astype(BF16)


def _attn(q, kn, vn, ctx, into, *, nb, lq, lk, nheads, hp, q_per_kv, tq):
    ngroups = nheads // hp
    kvw = hp // q_per_kv * LANE
    nq = lq // tq
    ins = [q, kn, vn]
    in_specs = [pl.BlockSpec((tq, hp * LANE), lambda b, g, i: (b * nq + i, g)),
                pl.BlockSpec((lk, kvw), lambda b, g, i: (b, g)),
                pl.BlockSpec((lk, kvw), lambda b, g, i: (b, g))]
    row0 = 0
    if ctx is not None:
        ins += list(ctx)
        in_specs += [pl.BlockSpec((PAST_LEN, kvw), lambda b, g, i: (b, g))] * 2
        row0 = TP // tq
    ins.append(into)
    in_specs.append(pl.BlockSpec(memory_space=pl.ANY))
    aliases = {len(ins) - 1: 0}
    return pl.pallas_call(
        functools.partial(_attn_kernel, hp=hp, q_per_kv=q_per_kv, has_ctx=ctx is not None),
        grid=(nb, ngroups, nq),
        in_specs=in_specs,
        out_specs=pl.BlockSpec((tq, hp * A_V), lambda b, g, i: (row0 + b * nq + i, g)),
        out_shape=jax.ShapeDtypeStruct((T_ALL, nheads * A_V), BF16),
        input_output_aliases=aliases,
        compiler_params=_cp(("parallel", "parallel", "arbitrary")),
        name="attn_ctx" if ctx is not None else "attn",
    )(*ins)


def _outproj_kernel(*refs, glu):
    if glu:
        x_ref, mod_ref, a_ref, b_ref, wa_ref, wb_ref, wg_ref, o_ref = refs
        yd = jax.nn.gelu(b_ref[...])
        b = (yd * jax.nn.sigmoid(_mm(yd.astype(BF16), wg_ref[...]))).astype(BF16)
    else:
        x_ref, mod_ref, a_ref, b_ref, wa_ref, wb_ref, o_ref = refs
        b = b_ref[...]
    acc = _mm(a_ref[...], wa_ref[...]) + _mm(b, wb_ref[...])
    o_ref[...] = x_ref[...] + mod_ref[0, 2:3, :] * acc


def _outproj(x, mod, a, b, wa, wb, wg=None, tm=512):
    t, d = x.shape
    full = lambda v: pl.BlockSpec(v.shape, lambda i: (0,) * v.ndim)
    rows = lambda v: pl.BlockSpec((tm, v.shape[1]), lambda i: (i, 0))
    ins = [x, mod, a, b, wa, wb]
    in_specs = [rows(x), pl.BlockSpec((1, 8, d), lambda i: (_cond_idx(i, tm), 0, 0)), rows(a), rows(b),
                full(wa), full(wb)]
    if wg is not None:
        ins.append(wg)
        in_specs.append(full(wg))
    return pl.pallas_call(
        functools.partial(_outproj_kernel, glu=wg is not None),
        grid=(t // tm,),
        in_specs=in_specs,
        out_specs=rows(x),
        out_shape=jax.ShapeDtypeStruct((t, d), F32),
        compiler_params=_cp(("parallel",)),
        name="outproj_glu" if wg is not None else "outproj",
    )(*ins)


FF_SPLIT = (0, 768, FF_EXPERT)


MOE_BLK = 256


def _swiglu(h, w1, w3, w2):
    y = None
    for k in range(len(FF_SPLIT) - 1):
        sl = slice(FF_SPLIT[k], FF_SPLIT[k + 1])
        act = (_silu(_mm(h, w1[:, sl])) * _mm(h, w3[:, sl])).astype(BF16)
        part = _mm(act, w2[sl, :])
        y = part if y is None else y + part
    return y


def _ffn_kernel(*refs, routed, nj):
    if routed:
        (x_ref, mod_ref, g_ref, rwh_ref, rwl_ref, rb_ref, w1_ref, w3_ref, w2_ref, o_ref,
         h_sc, acc_sc, comb_sc, rank_sc, rank_t_sc, cnt_sc) = refs
    else:
        x_ref, mod_ref, g_ref, w1_ref, w3_ref, w2_ref, o_ref, h_sc, acc_sc = refs
    j = pl.program_id(1)
    tm = x_ref.shape[0]

    @pl.when(j == 0)
    def _():
        h = _modulate(x_ref[...], g_ref[...], mod_ref, 3, 4)
        h_hi = h.astype(BF16)
        h_sc[...] = h_hi
        acc_sc[...] = jnp.zeros(acc_sc.shape, F32)
        if routed:
            h_lo = (h - h_hi.astype(F32)).astype(BF16)
            logits = _mm(h_hi, rwh_ref[...]) + _mm(h_lo, rwh_ref[...]) + _mm(h_hi, rwl_ref[...]) + rb_ref[...]
            lane = lax.broadcasted_iota(jnp.int32, logits.shape, 1)
            lg = jnp.where(lane < N_EXPERTS, logits, -jnp.inf)
            m1 = jnp.max(lg, axis=-1, keepdims=True)
            i1 = jnp.min(jnp.where(lg == m1, lane, LANE), axis=-1, keepdims=True)
            lg2 = jnp.where(lane == i1, -jnp.inf, lg)
            m2 = jnp.max(lg2, axis=-1, keepdims=True)
            i2 = jnp.min(jnp.where(lg2 == m2, lane, LANE), axis=-1, keepdims=True)
            e2 = jnp.exp(m2 - m1)
            inv = 1.0 / (1.0 + e2)
            comb_sc[...] = jnp.where(lane == i1, inv, 0.0) + jnp.where(lane == i2, e2 * inv, 0.0)
            sel = jnp.where(lane == i1, 1.0, 0.0) + jnp.where(lane == i2, 1.0, 0.0)
            ti = lax.broadcasted_iota(jnp.int32, (tm, tm), 0)
            si = lax.broadcasted_iota(jnp.int32, (tm, tm), 1)
            before = _mm((si < ti).astype(BF16), sel.astype(BF16))
            rank = jnp.where(sel > 0.0, before, -1.0)
            rank_sc[...] = rank
            cnt_sc[...] = jnp.broadcast_to(before[tm - 1:tm, :] + sel[tm - 1:tm, :], cnt_sc.shape)
            for k in range(tm // LANE):
                rank_t_sc[:, k * LANE:(k + 1) * LANE] = rank[k * LANE:(k + 1) * LANE, :].T

    if routed:
        lane = lax.broadcasted_iota(jnp.int32, comb_sc.shape, 1)
        lane8 = lax.broadcasted_iota(jnp.int32, cnt_sc.shape, 1)
        n = jnp.sum(jnp.where(lane8 == j, cnt_sc[...], 0.0)[0:1, :]).astype(jnp.int32)
        shift = int(math.log2(MOE_BLK))
        nfull = lax.shift_right_logical(n, shift)
        rem = n - lax.shift_left(nfull, shift)
        gate = jnp.sum(jnp.where(lane == j, comb_sc[...], 0.0), axis=-1, keepdims=True)
        rank_col = jnp.sum(jnp.where(lane == j, rank_sc[...], 0.0), axis=-1, keepdims=True)
        rank_row = rank_t_sc[pl.ds(j, 1), :]

        def run(r0, rows):
            r0 = r0.astype(F32)
            r_iota = lax.broadcasted_iota(jnp.int32, (rows, tm), 0).astype(F32)
            c_iota = lax.broadcasted_iota(jnp.int32, (tm, rows), 1).astype(F32)
            pick = jnp.where(rank_row - r0 == r_iota, 1.0, 0.0).astype(BF16)
            hb = _mm(pick, h_sc[...]).astype(BF16)
            yb = _swiglu(hb, w1_ref.at[0, 0], w3_ref.at[0, 0], w2_ref.at[0, 0]).astype(BF16)
            place = jnp.where(rank_col - r0 == c_iota, 1.0, 0.0).astype(BF16)
            acc_sc[...] += gate * _mm(place, yb)

        def block(b, carry):
            run(b * MOE_BLK, MOE_BLK)
            return carry
        lax.fori_loop(0, nfull + (rem > MOE_BLK // 2).astype(jnp.int32), block, 0)

        @pl.when((rem > 0) & (rem <= MOE_BLK // 2))
        def _():
            run(nfull * MOE_BLK, MOE_BLK // 2)
    else:
        acc_sc[...] += _swiglu(h_sc[...], w1_ref.at[0], w3_ref.at[0], w2_ref.at[0])

    @pl.when(j == nj - 1)
    def _():
        o_ref[...] = x_ref[...] + mod_ref[0, 5:6, :] * acc_sc[...]


def _ffn(x, mod, g, w1, w3, w2, li, router=None, tm=1024):
    t, d = x.shape
    routed = router is not None
    tf = FF_EXPERT
    nj = w1.shape[1] if routed else w1.shape[2] // tf
    ins = [x, mod, g.reshape(1, d)]
    in_specs = [pl.BlockSpec((tm, d), lambda i, j: (i, 0)),
                pl.BlockSpec((1, 8, d), lambda i, j: (_cond_idx(i, tm), 0, 0)),
                pl.BlockSpec((1, d), lambda i, j: (0, 0))]
    scratch = [pltpu.VMEM((tm, d), BF16), pltpu.VMEM((tm, d), F32)]
    if routed:
        ins += list(router)
        in_specs += [pl.BlockSpec((d, LANE), lambda i, j: (0, 0))] * 2 + [pl.BlockSpec((1, LANE), lambda i, j: (0, 0))]
        in_specs += ([pl.BlockSpec((1, 1, d, tf), lambda i, j: (li, j, 0, 0))] * 2
                     + [pl.BlockSpec((1, 1, tf, d), lambda i, j: (li, j, 0, 0))])
        scratch += [pltpu.VMEM((tm, LANE), F32), pltpu.VMEM((tm, LANE), F32), pltpu.VMEM((LANE, tm), F32),
                    pltpu.VMEM((8, LANE), F32)]
    else:
        in_specs += ([pl.BlockSpec((1, d, tf), lambda i, j: (li, 0, j))] * 2
                     + [pl.BlockSpec((1, tf, d), lambda i, j: (li, j, 0))])
    ins += [w1, w3, w2]
    return pl.pallas_call(
        functools.partial(_ffn_kernel, routed=routed, nj=nj),
        grid=(t // tm, nj),
        in_specs=in_specs,
        out_specs=pl.BlockSpec((tm, d), lambda i, j: (i, 0)),
        out_shape=jax.ShapeDtypeStruct((t, d), F32),
        scratch_shapes=scratch,
        compiler_params=_cp(("parallel", "arbitrary")),
        name="moe" if routed else "ffn",
    )(*ins)


CONV_ROWS = 4096


def _conv_kernel(x_ref, w_ref, b_ref, o_ref):
    i = pl.program_id(0)
    seq = jnp.where(i < TP // CONV_ROWS, SEQ, DEC_SEQ)
    x = x_ref[...]
    pos = lax.broadcasted_iota(jnp.int32, x.shape, 0) & (seq - 1)
    prev = jnp.where(pos == 0, 0.0, pltpu.roll(x, 1, 0))
    nxt = jnp.where(pos == seq - 1, 0.0, pltpu.roll(x, CONV_ROWS - 1, 0))
    y = prev * w_ref[0:1, :] + x * w_ref[1:2, :] + nxt * w_ref[2:3, :] + b_ref[...]
    o_ref[...] = _silu(y)


def _conv(proj, w, b):
    tc = 256
    t = proj.shape[0]
    return pl.pallas_call(
        _conv_kernel,
        grid=(t // CONV_ROWS, C_CONV_DIM // tc),
        in_specs=[pl.BlockSpec((CONV_ROWS, tc), lambda i, j: (i, j + OD_XBC // tc)),
                  pl.BlockSpec((3, tc), lambda i, j: (0, j)),
                  pl.BlockSpec((1, tc), lambda i, j: (0, j))],
        out_specs=pl.BlockSpec((CONV_ROWS, tc), lambda i, j: (i, j)),
        out_shape=jax.ShapeDtypeStruct((t, C_CONV_DIM), F32),
        compiler_params=_cp(("parallel", "parallel")),
        name="conv",
    )(proj, w, b.reshape(1, C_CONV_DIM))


Q = SSD_CHUNK
NCH = T_ALL // Q
NCH_P = TP // Q
NC_P = SEQ // Q
NC_S = DEC_SEQ // Q
SSD_PAR = 2


def _chunk_seq(s):
    is_p = s < NCH_P
    nc = jnp.where(is_p, NC_P, NC_S)
    seq = jnp.where(is_p, s // NC_P, BATCH + (s - NCH_P) // NC_S)
    c = jnp.where(is_p, s % NC_P, (s - NCH_P) % NC_S)
    return seq, c, nc, s - c


def _chunk_mirror(s):
    _, c, nc, start = _chunk_seq(s)
    return start + nc - 1 - c


def _ssd_dt(dt_raw, dt_bias, a_row):
    z = dt_raw + dt_bias
    dt = jnp.maximum(z, 0.0) + jnp.log(1.0 + jnp.exp(-jnp.abs(z)))
    la = dt * a_row
    r = lax.broadcasted_iota(jnp.int32, (Q, Q), 0)
    cc = lax.broadcasted_iota(jnp.int32, (Q, Q), 1)
    tri = (cc <= r).astype(BF16)
    hi, lo = _split_bf16(la)
    return dt, la, _mm(tri, hi) + _mm(tri, lo)


def _expand(v, e32_ref):
    hi, lo = _split_bf16(v)
    return _mm(hi, e32_ref[...]) + _mm(lo, e32_ref[...])


def _ssd1_kernel(xf_ref, xb_ref, df_ref, db_ref, dtb_ref, a_ref, e32_ref, h0_ref, hf_o, hb_o, fin_o, hf_sc, hb_sc):
    s = SSD_PAR * pl.program_id(0)
    _, c, nc, _ = _chunk_seq(s)

    @pl.when(c == 0)
    def _():
        hf_sc[...] = h0_ref[0, 0]
        hb_sc[...] = h0_ref[0, 1]

    lane = lax.broadcasted_iota(jnp.int32, (Q, LANE), 1)
    is_f = lane < C_HEADS
    is_f8 = lax.broadcasted_iota(jnp.int32, (8, LANE), 1) < C_HEADS
    for k in range(SSD_PAR):
        rf = slice(k * Q, (k + 1) * Q)
        rb = slice((SSD_PAR - 1 - k) * Q, (SSD_PAR - k) * Q)
        hf_o[k] = hf_sc[...].astype(BF16)
        hb_o[SSD_PAR - 1 - k] = hb_sc[...].astype(BF16)
        dt_f, _, ac_f = _ssd_dt(df_ref[rf, :], dtb_ref[...], a_ref[...])
        tot_f = ac_f[Q - 1:Q, :]
        dt_b, la_b, ac_b = _ssd_dt(db_ref[rb, :], dtb_ref[...], a_ref[...])
        tot_b = ac_b[Q - 1:Q, :]
        w = jnp.where(is_f, jnp.exp(tot_f - ac_f) * dt_f, jnp.exp(ac_b - la_b) * dt_b)
        dec = jnp.where(is_f8, jnp.exp(jnp.broadcast_to(tot_f, (8, LANE))),
                        jnp.exp(jnp.broadcast_to(tot_b, (8, LANE))))
        w_x = _expand(w, e32_ref)
        dec_x = _expand(dec, e32_ref)[0:1, :]
        for d, (x_ref, r, st) in enumerate(((xf_ref, rf, hf_sc), (xb_ref, rb, hb_sc))):
            xw = (x_ref[r, 0:C_INNER] * w_x[:, d * C_INNER:(d + 1) * C_INNER]).astype(BF16)
            for g in range(C_GROUPS):
                bm_t = x_ref[r, C_INNER + g * C_STATE:C_INNER + (g + 1) * C_STATE].T.astype(BF16)
                cs = slice(g * C_HPG * C_HD, (g + 1) * C_HPG * C_HD)
                dcs = slice(d * C_INNER + g * C_HPG * C_HD, d * C_INNER + (g + 1) * C_HPG * C_HD)
                st[:, cs] = st[:, cs] * dec_x[:, dcs] + _mm(bm_t, xw[:, cs])

    @pl.when((c == nc - SSD_PAR) & (s < NCH_P))
    def _():
        for d, st in enumerate((hf_sc, hb_sc)):
            for k in range(C_INNER // LANE):
                fin_o[0, d, k * LANE:(k + 1) * LANE, :] = st[:, k * LANE:(k + 1) * LANE].T


def _ssd1(xa, proj, dtb, a_row, e32, h0):
    dcol = OD_DT // LANE
    rows = SSD_PAR * Q
    first = lambda i: SSD_PAR * i
    mirror = lambda i: _chunk_mirror(SSD_PAR * i + SSD_PAR - 1) // SSD_PAR
    h0_idx = lambda i: (jnp.where(first(i) < NCH_P, 0, 1 + (first(i) - NCH_P) // NC_S), 0, 0, 0)
    return pl.pallas_call(
        _ssd1_kernel,
        grid=(NCH // SSD_PAR,),
        in_specs=[pl.BlockSpec((rows, C_CONV_DIM), lambda i: (i, 0)),
                  pl.BlockSpec((rows, C_CONV_DIM), lambda i: (mirror(i), 0)),
                  pl.BlockSpec((rows, LANE), lambda i: (i, dcol)),
                  pl.BlockSpec((rows, LANE), lambda i: (mirror(i), dcol)),
                  pl.BlockSpec((1, LANE), lambda i: (0, 0)),
                  pl.BlockSpec((1, LANE), lambda i: (0, 0)),
                  pl.BlockSpec((LANE, 2 * C_INNER), lambda i: (0, 0)),
                  pl.BlockSpec((1, 2, C_STATE, C_INNER), h0_idx)],
        out_specs=[pl.BlockSpec((SSD_PAR, C_STATE, C_INNER), lambda i: (i, 0, 0)),
                   pl.BlockSpec((SSD_PAR, C_STATE, C_INNER), lambda i: (mirror(i), 0, 0)),
                   pl.BlockSpec((1, 2, C_INNER, C_STATE),
                                lambda i: (jnp.minimum(_chunk_seq(first(i))[0], BATCH - 1), 0, 0, 0))],
        out_shape=[jax.ShapeDtypeStruct((NCH, C_STATE, C_INNER), BF16),
                   jax.ShapeDtypeStruct((NCH, C_STATE, C_INNER), BF16),
                   jax.ShapeDtypeStruct((BATCH, 2, C_INNER, C_STATE), F32)],
        scratch_shapes=[pltpu.VMEM((C_STATE, C_INNER), F32)] * 2,
        compiler_params=_cp(("arbitrary",)),
        name="ssd_states",
    )(xa, xa, proj, proj, dtb, a_row, e32, h0)


def _ssd2_kernel(xa_ref, d_ref, z_ref, hf_ref, hb_ref, dtb_ref, a_ref, e32_ref, dsk_ref, gn_ref, o_ref, y_sc):
    for k in range(SSD_PAR):
        _ssd2_chunk(k, xa_ref, d_ref, z_ref, hf_ref, hb_ref, dtb_ref, a_ref, e32_ref, dsk_ref, gn_ref, o_ref, y_sc)


def _ssd2_chunk(k, xa_ref, d_ref, z_ref, hf_ref, hb_ref, dtb_ref, a_ref, e32_ref, dsk_ref, gn_ref, o_ref, y_sc):
    r = slice(k * Q, (k + 1) * Q)
    dt, la, acum = _ssd_dt(d_ref[r, :], dtb_ref[...], a_ref[...])
    tot = acum[Q - 1:Q, :]
    excl = acum - la
    lane = lax.broadcasted_iota(jnp.int32, (Q, LANE), 1)
    is_f = lane < C_HEADS
    logdt = jnp.log(dt)
    col = jnp.where(is_f, acum, excl)
    row_t = jnp.where(is_f, acum - logdt, excl + logdt).T
    dec_x = _expand(jnp.where(is_f, jnp.exp(acum), jnp.exp(tot - excl)), e32_ref)
    x = xa_ref[r, 0:C_INNER]
    xb16 = x.astype(BF16)
    ti = lax.broadcasted_iota(jnp.int32, (Q, Q), 0)
    si = lax.broadcasted_iota(jnp.int32, (Q, Q), 1)
    low = si <= ti
    upp = si >= ti
    lane_q = lax.broadcasted_iota(jnp.int32, (Q, LANE), 1)
    gw = C_HPG * C_HD
    for g in range(C_GROUPS):
        bm = xa_ref[r, C_INNER + g * C_STATE:C_INNER + (g + 1) * C_STATE].astype(BF16)
        cm = xa_ref[r, C_INNER + (C_GROUPS + g) * C_STATE:C_INNER + (C_GROUPS + g + 1) * C_STATE].astype(BF16)
        gmat = _mm_nt(cm, bm)
        cs = slice(g * gw, (g + 1) * gw)
        ys = (_mm(cm, hf_ref[k, :, cs]) * dec_x[:, cs]
              + _mm(cm, hb_ref[k, :, cs]) * dec_x[:, C_INNER + g * gw:C_INNER + (g + 1) * gw])
        for pr in range(C_HPG // 2):
            xpair = xb16[:, g * gw + pr * LANE:g * gw + (pr + 1) * LANE]
            res = []
            for e in range(2):
                h = g * C_HPG + 2 * pr + e
                hb_ = C_HEADS + h
                m_f = jnp.where(low, jnp.exp(col[:, h:h + 1] - row_t[h:h + 1, :]), 0.0)
                m_b = jnp.where(upp, jnp.exp(row_t[hb_:hb_ + 1, :] - col[:, hb_:hb_ + 1]), 0.0)
                res.append(_mm((gmat * (m_f + m_b)).astype(BF16), xpair))
            y_sc[r, g * gw + pr * LANE:g * gw + (pr + 1) * LANE] = (
                jnp.where(lane_q < C_HD, res[0], res[1]) + ys[:, pr * LANE:(pr + 1) * LANE])
    y = y_sc[r, :] + dsk_ref[...] * x
    y = y * _silu(z_ref[r, :])
    y = y * lax.rsqrt(jnp.mean(y * y, axis=-1, keepdims=True) + EPS) * gn_ref[...]
    o_ref[r, :] = y.astype(BF16)


def _ssd2(xa, proj, hf, hb, dtb, a_row, e32, dsk, gn):
    dcol = OD_DT // LANE
    rows = SSD_PAR * Q
    one = lambda w: pl.BlockSpec((1, w), lambda s: (0, 0))
    return pl.pallas_call(
        _ssd2_kernel,
        grid=(NCH // SSD_PAR,),
        in_specs=[pl.BlockSpec((rows, C_CONV_DIM), lambda s: (s, 0)),
                  pl.BlockSpec((rows, LANE), lambda s: (s, dcol)),
                  pl.BlockSpec((rows, C_INNER), lambda s: (s, 0)),
                  pl.BlockSpec((SSD_PAR, C_STATE, C_INNER), lambda s: (s, 0, 0)),
                  pl.BlockSpec((SSD_PAR, C_STATE, C_INNER), lambda s: (s, 0, 0)),
                  one(LANE), one(LANE),
                  pl.BlockSpec((LANE, 2 * C_INNER), lambda s: (0, 0)),
                  one(C_INNER), one(C_INNER)],
        out_specs=pl.BlockSpec((rows, C_INNER), lambda s: (s, 0)),
        out_shape=jax.ShapeDtypeStruct((T_ALL, C_INNER), BF16),
        scratch_shapes=[pltpu.VMEM((rows, C_INNER), F32)],
        compiler_params=_cp(("parallel",)),
        name="ssd_out",
    )(xa, proj, proj, hf, hb, dtb, a_row, e32, dsk, gn)


S5_MM_STEPS = 32


def _cmul(ar, ai, br, bi):
    return ar * br - ai * bi, ar * bi + ai * br


def _s5_kernel(u_ref, wx_ref, cw_ref, a_ref, dsk_ref, h0_ref, y_ref, st_ref, xs, *, nseg):
    ct = pl.program_id(0)
    quarter = ct % 4
    nslab = S5_STEPS // S5_MM_STEPS
    rows = S5_MM_STEPS * S5_CHAINS

    def fill(k, carry):
        r0 = pl.multiple_of(k * S5_MM_STEPS, S5_MM_STEPS)
        ub = u_ref[pl.ds(r0, S5_MM_STEPS)].reshape(rows, LANE).astype(BF16)
        xs[pl.ds(r0, S5_MM_STEPS)] = _mm(ub, wx_ref[0]).reshape(S5_MM_STEPS, S5_CHAINS, 4 * LANE)
        return carry
    lax.fori_loop(0, nslab, fill, 0)

    shape = (S5_CHAINS, LANE)
    a = a_ref[0]
    afr, afi, abr, abi = (jnp.broadcast_to(a[k:k + 1, :], shape) for k in range(4))

    def scan(init, store):
        def body(i, carry):
            hfr, hfi, hbr, hbi = carry
            j = S5_STEPS - 1 - i
            pr, pi = _cmul(afr, afi, hfr, hfi)
            nfr = pr + xs[i, :, 0:LANE]
            nfi = pi + xs[i, :, LANE:2 * LANE]
            pr, pi = _cmul(abr, abi, hbr, hbi)
            nbr = pr + xs[j, :, 2 * LANE:3 * LANE]
            nbi = pi + xs[j, :, 3 * LANE:4 * LANE]
            if store:
                xs[i, :, 0:LANE] = nfr
                xs[i, :, LANE:2 * LANE] = nfi
                xs[j, :, 2 * LANE:3 * LANE] = nbr
                xs[j, :, 3 * LANE:4 * LANE] = nbi
            return nfr, nfi, nbr, nbi
        return lax.fori_loop(0, S5_STEPS, body, init)

    zero = jnp.zeros(shape, F32)
    if nseg == 1:
        init = (zero, zero, zero, zero)
    else:
        efr, efi, ebr, ebi = scan((zero, zero, zero, zero), store=False)
        pfr, pfi, pbr, pbi = afr, afi, abr, abi
        for _ in range(int(math.log2(S5_STEPS))):
            pfr, pfi = _cmul(pfr, pfi, pfr, pfi)
            pbr, pbi = _cmul(pbr, pbi, pbr, pbi)
        row = lax.broadcasted_iota(jnp.int32, shape, 0) % nseg
        first = row == 0
        last = row == nseg - 1
        h0 = h0_ref[...]
        cfr, cfi, cbr, cbi = zero, zero, zero, zero
        dn = lambda v: pltpu.roll(v, 1, 0)
        up = lambda v: pltpu.roll(v, S5_CHAINS - 1, 0)
        for _ in range(nseg):
            pr, pi = _cmul(pfr, pfi, dn(cfr), dn(cfi))
            cfr = jnp.where(first, h0[0], pr + dn(efr))
            cfi = jnp.where(first, h0[1], pi + dn(efi))
            pr, pi = _cmul(pbr, pbi, up(cbr), up(cbi))
            cbr = jnp.where(last, h0[2], pr + up(ebr))
            cbi = jnp.where(last, h0[3], pi + up(ebi))
        init = (cfr, cfi, cbr, cbi)
    fin = scan(init, store=True)
    for k in range(4):
        st_ref[k] = fin[k]

    def emit(k, carry):
        r0 = pl.multiple_of(k * S5_MM_STEPS, S5_MM_STEPS)
        blk = xs[pl.ds(r0, S5_MM_STEPS)]
        s_re = blk[:, :, 0:LANE] + blk[:, :, 2 * LANE:3 * LANE]
        s_im = blk[:, :, LANE:2 * LANE] + blk[:, :, 3 * LANE:4 * LANE]
        s2 = jnp.concatenate([s_re, s_im], axis=-1).reshape(rows, 2 * LANE).astype(BF16)
        yv = _mm(s2, cw_ref[0]).reshape(S5_MM_STEPS, S5_CHAINS, LANE)

        @pl.when(quarter == 0)
        def _():
            y_ref[pl.ds(r0, S5_MM_STEPS)] = yv + dsk_ref[...] * u_ref[pl.ds(r0, S5_MM_STEPS)]

        @pl.when(quarter != 0)
        def _():
            y_ref[pl.ds(r0, S5_MM_STEPS)] += yv
        return carry
    lax.fori_loop(0, nslab, emit, 0)


def _s5(u_perm, sw, h0, nseg):
    nct = D_GROUPS // 2
    return pl.pallas_call(
        functools.partial(_s5_kernel, nseg=nseg),
        grid=(nct,),
        in_specs=[pl.BlockSpec((S5_STEPS, S5_CHAINS, LANE), lambda ct: (0, 0, ct // 4)),
                  pl.BlockSpec((1, LANE, 4 * LANE), lambda ct: (ct, 0, 0)),
                  pl.BlockSpec((1, 2 * LANE, LANE), lambda ct: (ct, 0, 0)),
                  pl.BlockSpec((1, 8, LANE), lambda ct: (ct, 0, 0)),
                  pl.BlockSpec((1, LANE), lambda ct: (0, ct // 4)),
                  pl.BlockSpec((4, S5_CHAINS, LANE), lambda ct: (0, 0, ct))],
        out_specs=[pl.BlockSpec((S5_STEPS, S5_CHAINS, LANE), lambda ct: (0, 0, ct // 4)),
                   pl.BlockSpec((4, S5_CHAINS, LANE), lambda ct: (0, 0, ct))],
        out_shape=[jax.ShapeDtypeStruct((S5_STEPS, S5_CHAINS, D_WIDTH), F32),
                   jax.ShapeDtypeStruct((4, S5_CHAINS, nct * LANE), F32)],
        scratch_shapes=[pltpu.VMEM((S5_STEPS, S5_CHAINS, 4 * LANE), F32)],
        compiler_params=_cp(("arbitrary",)),
        name="s5_seg" if nseg > 1 else "s5",
    )(u_perm, sw["wx"], sw["cw"], sw["a"], sw["dsk"], h0)


def _pad_last(a, n):
    return jnp.pad(a, [(0, 0)] * (a.ndim - 1) + [(0, n - a.shape[-1])])


def _even_weights(w_in, q_lora_g, w_uq, kv_lora_g, w_ukv, q_norm_g, k_norm_g, bq_g, bk_g):
    d = w_in.shape[0]
    s0, s1, s2, s3, s4 = 384, 640, 672, 1184, 1312
    kr = jnp.pad(w_in[:, s1:s2], ((0, 0), (A_NOPE, LANE - A_QK)))
    heads = lambda w, n: _pad_last(w.reshape(d, n, B_HD), LANE).reshape(d, n * LANE)
    w_perm = jnp.concatenate([w_in[:, :s0], w_in[:, s0:s1], kr, heads(w_in[:, s2:s3], B_HEADS),
                              heads(w_in[:, s3:s4], B_KV_HEADS), heads(w_in[:, s4:], B_KV_HEADS)], axis=1)
    row = lambda g: _pad_last(g.reshape(1, -1), LANE)
    return {
        "w_in": w_perm.astype(BF16),
        "gql": q_lora_g.reshape(1, -1),
        "wuq": _pad_last(w_uq, LANE).reshape(A_Q_LORA, A_HEADS * LANE).astype(BF16),
        "gqn": row(q_norm_g),
        "gkvl": kv_lora_g.reshape(1, -1),
        "wk": _pad_last(w_ukv[:, :, :A_NOPE], LANE).reshape(A_KV_LORA, A_HEADS * LANE).astype(BF16),
        "wv": _pad_last(w_ukv[:, :, A_NOPE:], LANE).reshape(A_KV_LORA, A_HEADS * LANE).astype(BF16),
        "gkn": row(k_norm_g),
        "gbq": row(bq_g),
        "gbk": row(bk_g),
    }


def _rope_tables():
    rows = DEC_SEQ // GRID_W

    def angles(dim):
        half = dim // 2
        inv = ROPE_THETA ** (-jnp.arange(0, half, 2, dtype=F32) / half)
        r = jnp.repeat(jnp.arange(rows, dtype=F32), GRID_W)
        col = jnp.tile(jnp.arange(GRID_W, dtype=F32), rows)
        return jnp.concatenate([r[:, None] * inv, col[:, None] * inv], axis=-1)

    def lanes(dim, start):
        ang = angles(dim)
        cos = jnp.repeat(jnp.cos(ang), 2, axis=-1)
        sin = jnp.repeat(jnp.sin(ang), 2, axis=-1) * jnp.tile(jnp.array([-1.0, 1.0], F32), dim // 2)
        pad = ((0, 0), (start, LANE - start - dim))
        return jnp.pad(cos, pad, constant_values=1.0), jnp.pad(sin, pad)

    cos_a, sin_a = lanes(A_ROPE, A_NOPE)
    cos_b, sin_b = lanes(B_HD, 0)
    return cos_a, sin_a, cos_b, sin_b


def _odd_weights(w_in, dt_bias, a_log, c_d_skip):
    w_perm = jnp.concatenate([w_in[:, :C_INNER], w_in[:, C_INNER:C_INNER + C_CONV_DIM],
                              w_in[:, C_INNER + C_CONV_DIM + 2 * C_HEADS:],
                              _pad_last(w_in[:, C_INNER + C_CONV_DIM:C_INNER + C_CONV_DIM + 2 * C_HEADS], LANE)], axis=1)
    lane = jnp.arange(LANE)
    col = jnp.arange(2 * C_INNER)
    e32 = ((lane[:, None] < 2 * C_HEADS) & (lane[:, None] == col[None, :] // C_HD)).astype(BF16)
    return {
        "w_in": w_perm.astype(BF16),
        "dtb": _pad_last(dt_bias.reshape(1, -1), LANE),
        "a_row": _pad_last(-jnp.exp(a_log).reshape(1, -1), LANE),
        "e32": e32,
        "dsk": jnp.repeat(c_d_skip, C_HD).reshape(1, C_INNER),
    }


def _s5_weights(lam_re, lam_im, log_dt, b_re, b_im, c_re, c_im, d_skip):
    nct = D_GROUPS // 2
    dt = jnp.exp(log_dt)[..., None]
    mag = jnp.exp(lam_re * dt)
    ar, ai = mag * jnp.cos(lam_im * dt), mag * jnp.sin(lam_im * dt)
    den = lam_re * lam_re + lam_im * lam_im
    kr = ((ar - 1.0) * lam_re + ai * lam_im) / den
    ki = (ai * lam_re - (ar - 1.0) * lam_im) / den
    xr = kr[..., None] * b_re[None] - ki[..., None] * b_im[None]
    xi = kr[..., None] * b_im[None] + ki[..., None] * b_re[None]
    xc = jnp.stack([xr[0], xi[0], xr[1], xi[1]]).reshape(4, nct, 2, D_STATE, D_GROUP_SIZE)
    quarter = (jnp.arange(nct)[:, None] % 4 == jnp.arange(4)[None, :]).astype(F32)
    eye2 = jnp.eye(2, dtype=F32)
    wx = jnp.einsum("tq,gh,ktgnj->tqgjkhn", quarter, eye2, xc).reshape(nct, LANE, 4 * LANE)
    cc = jnp.stack([c_re, -c_im]).reshape(2, nct, 2, D_GROUP_SIZE, D_STATE)
    cw = jnp.einsum("tq,gh,rtgcn->trhnqgc", quarter, eye2, cc).reshape(nct, 2 * LANE, LANE)
    planes = jnp.stack([ar[0], ai[0], ar[1], ai[1]]).reshape(4, nct, LANE).transpose(1, 0, 2)
    return {"wx": wx.astype(BF16), "cw": cw.astype(BF16), "a": jnp.pad(planes, ((0, 0), (0, 4), (0, 0))),
            "dsk": d_skip.reshape(1, D_WIDTH)}


def _even_layer(x, mod, norm1_g, norm2_g, ew, w_out, ffn_w, caches, rope_tabs):
    c_ckv, c_krope, c_kb, c_vb = caches
    proj = _modmm(x, mod, norm1_g, ew["w_in"])
    qa_p, ka_p, va_p, qb_p, kb_p, vb_p, ckv_c, kr_c, kbn_c, vb_c = _evenprep(proj, 0, TP, ew, None)
    qa_s, ka_s, va_s, qb_s, kb_s, vb_s = _evenprep(proj, TP, TS, ew, rope_tabs)
    n_ctx = DEC_BATCH * PAST_LEN
    kr_blk = jnp.pad(c_krope.reshape(n_ctx, A_ROPE), ((0, 0), (A_NOPE, LANE - A_QK)))
    ka_c, va_c = _ctxkv(c_ckv.reshape(n_ctx, A_KV_LORA), kr_blk, ew)
    heads_b = lambda a: _pad_last(a.transpose(0, 2, 1, 3), LANE).reshape(n_ctx, B_KV_HEADS * LANE)
    ones_col = (jnp.arange(B_KV_HEADS * LANE) % LANE == A_V).astype(F32)
    gqa = B_HEADS // B_KV_HEADS
    oa = jnp.zeros((T_ALL, A_HEADS * A_V), BF16)
    ob = jnp.zeros((T_ALL, B_HEADS * A_V), BF16)
    oa = _attn(qa_p, ka_p, va_p, None, oa, nb=BATCH, lq=SEQ, lk=SEQ, nheads=A_HEADS, hp=8, q_per_kv=1, tq=SEQ)
    ob = _attn(qb_p, kb_p, vb_p, None, ob, nb=BATCH, lq=SEQ, lk=SEQ, nheads=B_HEADS, hp=4, q_per_kv=gqa, tq=SEQ)
    oa = _attn(qa_s, ka_s, va_s, (ka_c, va_c), oa, nb=DEC_BATCH, lq=DEC_SEQ, lk=DEC_SEQ, nheads=A_HEADS, hp=4,
               q_per_kv=1, tq=ATT_TQ)
    ctx_b = (heads_b(c_kb).astype(BF16), (heads_b(c_vb) + ones_col).astype(BF16))
    ob = _attn(qb_s, kb_s, vb_s, ctx_b, ob, nb=DEC_BATCH, lq=DEC_SEQ, lk=DEC_SEQ, nheads=B_HEADS, hp=8,
               q_per_kv=gqa, tq=ATT_TQ)
    x = _outproj(x, mod, oa, ob, w_out[:A_HEADS * A_V], w_out[A_HEADS * A_V:])
    x = _ffn(x, mod, norm2_g, *ffn_w)
    new_cache = (ckv_c.reshape(BATCH, SEQ, A_KV_LORA), kr_c[:, A_NOPE:A_QK].reshape(BATCH, SEQ, A_ROPE),
                 kbn_c, vb_c)
    return x, new_cache


def _odd_layer(x, mod, norm1_g, norm2_g, ow, sw, conv_w, conv_b, c_norm_g, w_glu, w_out, moe_w, states):
    st_c, st_re, st_im = states
    proj = _modmm(x, mod, norm1_g, ow["w_in"])
    xa = _conv(proj, conv_w, conv_b)
    h0 = st_c.transpose(0, 1, 4, 2, 3).reshape(DEC_BATCH, 2, C_STATE, C_INNER)
    h0 = jnp.concatenate([jnp.zeros((1,) + h0.shape[1:], F32), h0], axis=0)
    hf, hb, fin = _ssd1(xa, proj, ow["dtb"], ow["a_row"], ow["e32"], h0)
    yc = _ssd2(xa, proj, hf, hb, ow["dtb"], ow["a_row"], ow["e32"], ow["dsk"], c_norm_g.reshape(1, C_INNER))
    u = proj[:, OD_U:OD_U + D_WIDTH]
    u_p = u[:TP].reshape(BATCH, SEQ, D_WIDTH).transpose(1, 0, 2)
    u_s = u[TP:].reshape(DEC_BATCH * S5_SEG, S5_STEPS, D_WIDTH).transpose(1, 0, 2)
    zeros_h0 = jnp.zeros((4, S5_CHAINS, D_GROUPS * D_STATE), F32)
    h0_s = jnp.stack([st_re[:, 0], st_im[:, 0], st_re[:, 1], st_im[:, 1]]).reshape(4, DEC_BATCH, D_GROUPS * D_STATE)
    h0_s = jnp.repeat(h0_s, S5_SEG, axis=1)
    yd_p, fin_d = _s5(u_p, sw, zeros_h0, 1)
    yd_s, _ = _s5(u_s, sw, h0_s, S5_SEG)
    yd = jnp.concatenate([yd_p.transpose(1, 0, 2).reshape(TP, D_WIDTH),
                          yd_s.transpose(1, 0, 2).reshape(TS, D_WIDTH)], axis=0)
    x = _outproj(x, mod, yc, yd, w_out[:C_INNER], w_out[C_INNER:], wg=w_glu)
    x = _ffn(x, mod, norm2_g, *moe_w[:4], router=moe_w[4])
    new_c = fin.reshape(BATCH, 2, C_HEADS, C_HD, C_STATE)
    fin_d = fin_d.reshape(4, BATCH, D_GROUPS, D_STATE)
    new_re = jnp.stack([fin_d[0], fin_d[2]], axis=1)
    new_im = jnp.stack([fin_d[1], fin_d[3]], axis=1)
    return x, (new_c, new_re, new_im)


def kernel(x_prompt, x_sample, cache_a_ckv, cache_a_krope, cache_b_k, cache_b_v, state_c_ssm, state_d_re, state_d_im, c, c_ctx, w_mod, b_mod, norm1_g, norm2_g, ev_w_in, ev_a_q_lora_g, ev_a_w_uq, ev_a_kv_lora_g, ev_a_w_ukv, ev_a_q_norm_g, ev_a_k_norm_g, ev_b_q_norm_g, ev_b_k_norm_g, ev_w_out, ffn_w1, ffn_w3, ffn_w2, od_w_in, od_conv_w, od_conv_b, od_dt_bias, od_a_log, od_c_d_skip, od_c_norm_g, od_lam_re, od_lam_im, od_log_dt, od_b_re, od_b_im, od_c_re, od_c_im, od_d_skip, od_w_glu, od_w_out, moe_router_w, moe_router_b, moe_w1, moe_w3, moe_w2):
    x = jnp.concatenate([x_prompt.reshape(TP, D_MODEL), x_sample.reshape(TS, D_MODEL)], axis=0)
    cond8 = jnp.zeros((8, D_MODEL), F32).at[0].set(c_ctx).at[1:1 + DEC_BATCH].set(c)
    mods = _adaln(cond8, w_mod, b_mod)[:, :1 + DEC_BATCH].reshape(DEPTH, 1 + DEC_BATCH, 6, D_MODEL)
    mods = jnp.pad(mods, ((0, 0), (0, 0), (0, 2), (0, 0)))
    rope_tabs = _rope_tables()
    ffn_bf = (ffn_w1.astype(BF16), ffn_w3.astype(BF16), ffn_w2.astype(BF16))
    moe_bf = (moe_w1.astype(BF16), moe_w3.astype(BF16), moe_w2.astype(BF16))
    ev_new, od_new = [], []
    for layer in range(DEPTH):
        mod = mods[layer]
        if layer % 2 == 0:
            e = layer // 2
            ew = _even_weights(ev_w_in[e], ev_a_q_lora_g[e], ev_a_w_uq[e], ev_a_kv_lora_g[e], ev_a_w_ukv[e],
                               ev_a_q_norm_g[e], ev_a_k_norm_g[e], ev_b_q_norm_g[e], ev_b_k_norm_g[e])
            ffn_w = ffn_bf + (e,)
            x, new = _even_layer(x, mod, norm1_g[layer], norm2_g[layer], ew, ev_w_out[e].astype(BF16), ffn_w,
                                 (cache_a_ckv[:, e], cache_a_krope[:, e], cache_b_k[:, e], cache_b_v[:, e]),
                                 rope_tabs)
            ev_new.append(new)
        else:
            o = layer // 2
            ow = _odd_weights(od_w_in[o], od_dt_bias[o], od_a_log[o], od_c_d_skip[o])
            sw = _s5_weights(od_lam_re[o], od_lam_im[o], od_log_dt[o], od_b_re[o], od_b_im[o], od_c_re[o],
                             od_c_im[o], od_d_skip[o])
            rw_hi = _pad_last(moe_router_w[o], LANE).astype(BF16)
            rw_lo = (_pad_last(moe_router_w[o], LANE) - rw_hi.astype(F32)).astype(BF16)
            moe_w = moe_bf + (o, (rw_hi, rw_lo, _pad_last(moe_router_b[o].reshape(1, -1), LANE)))
            x, new = _odd_layer(x, mod, norm1_g[layer], norm2_g[layer], ow, sw, od_conv_w[o], od_conv_b[o],
                                od_c_norm_g[o], od_w_glu[o].astype(BF16), od_w_out[o].astype(BF16), moe_w,
                                (state_c_ssm[:, o], state_d_re[:, o], state_d_im[:, o]))
            od_new.append(new)
    stack = lambda items, k: jnp.stack([it[k] for it in items], axis=1)
    return (x[:TP].reshape(BATCH, SEQ, D_MODEL), x[TP:].reshape(DEC_BATCH, DEC_SEQ, D_MODEL),
            stack(ev_new, 0), stack(ev_new, 1), stack(ev_new, 2), stack(ev_new, 3),
            stack(od_new, 0), stack(od_new, 1), stack(od_new, 2))
```

```python
import functools
import math

import jax
import jax.numpy as jnp
from jax import lax
from jax.experimental import pallas as pl
from jax.experimental.pallas import tpu as pltpu

F32 = jnp.float32
BF16 = jnp.bfloat16

D_MODEL = 1024
BATCH = 32
SEQ = 256
DEPTH = 4
DEC_BATCH = 2
DEC_SEQ = 4096
PAST_LEN = 256
GRID_W = 64
ROPE_THETA = 10000.0
EPS = 1e-6
A_HEADS = 8
A_NOPE = 64
A_ROPE = 32
A_QK = A_NOPE + A_ROPE
A_V = 64
A_Q_LORA = 384
A_KV_LORA = 256
B_HEADS = 8
B_KV_HEADS = 2
B_HD = 64
C_HEADS = 16
C_HD = 64
C_GROUPS = 2
C_HPG = C_HEADS // C_GROUPS
C_STATE = 128
C_INNER = C_HEADS * C_HD
C_CONV_DIM = C_INNER + 2 * C_GROUPS * C_STATE
SSD_CHUNK = 128
D_GROUP_SIZE = 16
D_GROUPS = 32
D_WIDTH = D_GROUPS * D_GROUP_SIZE
D_STATE = 64
FF_DENSE = 2816
N_EXPERTS = 8
FF_EXPERT = 1408

TP = BATCH * SEQ
TS = DEC_BATCH * DEC_SEQ
T_ALL = TP + TS
LANE = 128
LOG2E = 1.4426950408889634
VMEM_LIMIT = 56 * 1024 * 1024

EV_CQ = 0
EV_CKV = 384
EV_KR = 640
EV_QB = 768
EV_KB = EV_QB + B_HEADS * LANE
EV_VB = EV_KB + B_KV_HEADS * LANE
EV_N = EV_VB + B_KV_HEADS * LANE
OD_Z = 0
OD_XBC = C_INNER
OD_U = OD_XBC + C_CONV_DIM
OD_DT = OD_U + D_WIDTH
OD_N = OD_DT + LANE
S5_STEPS = 256
S5_CHAINS = 32
S5_SEG = DEC_SEQ // S5_STEPS
ATT_TQ = 256
ATT_CHUNKS = 2


def _cp(sem):
    return pltpu.CompilerParams(dimension_semantics=sem, vmem_limit_bytes=VMEM_LIMIT)


def _cond_idx(i, tm):
    t0 = i * tm
    return jnp.where(t0 < TP, 0, 1 + (t0 - TP) // DEC_SEQ)


def _mm(a, b):
    return jnp.dot(a, b, preferred_element_type=F32)


def _mm_nt(a, b):
    return lax.dot_general(a, b, (((1,), (1,)), ((), ())), preferred_element_type=F32)


def _split_bf16(v):
    hi = v.astype(BF16)
    lo = (v - hi.astype(F32)).astype(BF16)
    return hi, lo


def _silu(v):
    return v * jax.nn.sigmoid(v)


def _rms(v, n):
    return v * lax.rsqrt(jnp.sum(v * v, axis=-1, keepdims=True) * (1.0 / n) + EPS)


def _adaln_kernel(c_ref, w_ref, b_ref, o_ref):
    s = _silu(c_ref[...])
    o_ref[0] = _mm(s.astype(BF16), w_ref[0].astype(BF16)) + b_ref[0]


def _adaln(cond8, w_mod, b_mod):
    depth, d, n6 = w_mod.shape
    tn = 1536
    return pl.pallas_call(
        _adaln_kernel,
        grid=(depth, n6 // tn),
        in_specs=[pl.BlockSpec((8, d), lambda l, j: (0, 0)),
                  pl.BlockSpec((1, d, tn), lambda l, j: (l, 0, j)),
                  pl.BlockSpec((1, 1, tn), lambda l, j: (l, 0, j))],
        out_specs=pl.BlockSpec((1, 8, tn), lambda l, j: (l, 0, j)),
        out_shape=jax.ShapeDtypeStruct((depth, 8, n6), F32),
        compiler_params=_cp(("parallel", "parallel")),
        name="adaln",
    )(cond8, w_mod, b_mod.reshape(depth, 1, n6))


def _modulate(x, g, mod_ref, shift_row, scale_row):
    y = x * lax.rsqrt(jnp.mean(x * x, axis=-1, keepdims=True) + EPS) * g
    return y * (1.0 + mod_ref[0, scale_row:scale_row + 1, :]) + mod_ref[0, shift_row:shift_row + 1, :]


def _token_specs(x, tm):
    if not isinstance(x, tuple):
        return [x], [pl.BlockSpec((tm, x.shape[1]), lambda i, *_: (i, 0))], lambda refs: refs[0][...]
    n0 = x[0].shape[0] // tm
    n1 = x[1].shape[0] // tm
    specs = [pl.BlockSpec((tm, x[0].shape[1]), lambda i, *_: (jnp.minimum(i, n0 - 1), 0)),
             pl.BlockSpec((tm, x[1].shape[1]), lambda i, *_: (jnp.clip(i - n0, 0, n1 - 1), 0))]
    return list(x), specs, lambda refs: jnp.where(pl.program_id(0) < n0, refs[0][...], refs[1][...])


def _modmm_kernel(*refs, read_x, nx):
    mod_ref, g_ref, w_ref, o_ref = refs[nx:]
    h = _modulate(read_x(refs[:nx]), g_ref[...], mod_ref, 0, 1)
    o_ref[...] = _mm(h.astype(BF16), w_ref[...])


def _modmm(x, mod, g, w, tm=512):
    xs, x_specs, read_x = _token_specs(x, tm)
    t = sum(a.shape[0] for a in xs)
    d = xs[0].shape[1]
    n = w.shape[1]
    return pl.pallas_call(
        functools.partial(_modmm_kernel, read_x=read_x, nx=len(xs)),
        grid=(t // tm,),
        in_specs=x_specs + [pl.BlockSpec((1, 8, d), lambda i: (_cond_idx(i, tm), 0, 0)),
                            pl.BlockSpec((1, d), lambda i: (0, 0)),
                            pl.BlockSpec((d, n), lambda i: (0, 0))],
        out_specs=pl.BlockSpec((tm, n), lambda i: (i, 0)),
        out_shape=jax.ShapeDtypeStruct((t, n), F32),
        compiler_params=_cp(("parallel",)),
        name="modmm",
    )(*xs, mod, g.reshape(1, d), w)


def _swap_pairs(v):
    lane = lax.broadcasted_iota(jnp.int32, v.shape, 1)
    nxt = pltpu.roll(v, LANE - 1, 1)
    prv = pltpu.roll(v, 1, 1)
    return jnp.where((lane & 1) == 0, nxt, prv)


def _rope(v, cos, sin):
    return v * cos + _swap_pairs(v) * sin


def _with_ones(v):
    lane = lax.broadcasted_iota(jnp.int32, v.shape, 1)
    return jnp.where((lane & (LANE - 1)) == A_V, 1.0, v)


def _mla_kv(ckvn, kr_blk, wk_ref, wv_ref, gkn, rope_tabs, ka_o, va_o):
    cb = ckvn.astype(BF16)
    kn = _mm(cb, wk_ref[...])
    va_o[...] = _with_ones(_mm(cb, wv_ref[...])).astype(BF16)
    for h in range(A_HEADS):
        sl = slice(h * LANE, (h + 1) * LANE)
        kh = _rms(kn[:, sl] + kr_blk, A_QK) * gkn
        if rope_tabs is not None:
            kh = _rope(kh, rope_tabs[0], rope_tabs[1])
        ka_o[:, sl] = kh.astype(BF16)


def _evenprep_kernel(*refs, rope):
    it = iter(refs)
    proj = next(it)
    if rope:
        cos_a, sin_a, cos_b, sin_b = (next(it)[...] for _ in range(4))
    gql, wuq, gqn, gkvl, wk, wv, gkn, gbq, gbk = (next(it) for _ in range(9))
    qa_o, ka_o, va_o, qb_o, kb_o, vb_o = (next(it) for _ in range(6))
    if not rope:
        ckv_o, kr_o, kbn_o, vbf_o = (next(it) for _ in range(4))

    cqn = _rms(proj[:, EV_CQ:EV_CQ + A_Q_LORA], A_Q_LORA) * gql[...]
    q = _mm(cqn.astype(BF16), wuq[...])
    for h in range(A_HEADS):
        sl = slice(h * LANE, (h + 1) * LANE)
        qh = _rms(q[:, sl], A_QK) * gqn[...]
        if rope:
            qh = _rope(qh, cos_a, sin_a)
        qa_o[:, sl] = (qh * (A_QK ** -0.5 * LOG2E)).astype(BF16)

    ckvn = _rms(proj[:, EV_CKV:EV_CKV + A_KV_LORA], A_KV_LORA) * gkvl[...]
    kr_blk = proj[:, EV_KR:EV_KR + LANE]
    _mla_kv(ckvn, kr_blk, wk, wv, gkn[...], (cos_a, sin_a) if rope else None, ka_o, va_o)

    for h in range(B_HEADS):
        sl = slice(h * LANE, (h + 1) * LANE)
        qh = _rms(proj[:, EV_QB + h * LANE:EV_QB + (h + 1) * LANE], B_HD) * gbq[...]
        if rope:
            qh = _rope(qh, cos_b, sin_b)
        qb_o[:, sl] = (qh * (B_HD ** -0.5 * LOG2E)).astype(BF16)
    for j in range(B_KV_HEADS):
        sl = slice(j * LANE, (j + 1) * LANE)
        kh = _rms(proj[:, EV_KB + j * LANE:EV_KB + (j + 1) * LANE], B_HD) * gbk[...]
        if not rope:
            kbn_o[0, j] = kh[:, :B_HD]
        else:
            kh = _rope(kh, cos_b, sin_b)
        kb_o[:, sl] = kh.astype(BF16)
    vb = proj[:, EV_VB:EV_VB + B_KV_HEADS * LANE]
    vb_o[...] = _with_ones(vb).astype(BF16)
    if not rope:
        for j in range(B_KV_HEADS):
            vbf_o[0, j] = vb[:, j * LANE:j * LANE + B_HD]
        ckv_o[...] = ckvn
        kr_o[...] = kr_blk


def _evenprep(proj, row_off, nrows, ew, rope_tabs):
    tm = 256
    rope = rope_tabs is not None
    off = row_off // tm
    full = lambda a: pl.BlockSpec(a.shape, lambda i: (0,) * a.ndim)
    rows = lambda w: pl.BlockSpec((tm, w), lambda i: (i, 0))
    ins = [proj]
    in_specs = [pl.BlockSpec((tm, EV_N), lambda i: (i + off, 0))]
    if rope:
        ins += list(rope_tabs)
        in_specs += [pl.BlockSpec((tm, LANE), lambda i: (i % (DEC_SEQ // tm), 0))] * 4
    ws = [ew["gql"], ew["wuq"], ew["gqn"], ew["gkvl"], ew["wk"], ew["wv"], ew["gkn"], ew["gbq"], ew["gbk"]]
    ins += ws
    in_specs += [full(a) for a in ws]
    hw = A_HEADS * LANE
    kvw = B_KV_HEADS * LANE
    out_shape = [jax.ShapeDtypeStruct((nrows, hw), BF16)] * 4 + [jax.ShapeDtypeStruct((nrows, kvw), BF16)] * 2
    out_specs = [rows(hw)] * 4 + [rows(kvw)] * 2
    if not rope:
        assert tm == SEQ
        cache_b = jax.ShapeDtypeStruct((nrows // SEQ, B_KV_HEADS, SEQ, B_HD), F32)
        cache_b_spec = pl.BlockSpec((1, B_KV_HEADS, SEQ, B_HD), lambda i: (i, 0, 0, 0))
        out_shape += [jax.ShapeDtypeStruct((nrows, A_KV_LORA), F32), jax.ShapeDtypeStruct((nrows, LANE), F32),
                      cache_b, cache_b]
        out_specs += [rows(A_KV_LORA), rows(LANE), cache_b_spec, cache_b_spec]
    return pl.pallas_call(
        functools.partial(_evenprep_kernel, rope=rope),
        grid=(nrows // tm,),
        in_specs=in_specs,
        out_specs=out_specs,
        out_shape=out_shape,
        compiler_params=_cp(("parallel",)),
        name="evenprep_rope" if rope else "evenprep",
    )(*ins)


def _ctxkv_kernel(ckv_ref, kr_ref, wk, wv, gkn, ka_o, va_o):
    _mla_kv(ckv_ref[...], kr_ref[...], wk, wv, gkn[...], None, ka_o, va_o)


def _ctxkv(ckvn, kr_blk, ew):
    n = ckvn.shape[0]
    hw = A_HEADS * LANE
    return pl.pallas_call(
        _ctxkv_kernel,
        out_shape=[jax.ShapeDtypeStruct((n, hw), BF16)] * 2,
        compiler_params=pltpu.CompilerParams(vmem_limit_bytes=VMEM_LIMIT),
        name="ctxkv",
    )(ckvn, kr_blk, ew["wk"], ew["wv"], ew["gkn"])


def _attn_kernel(*refs, hp, q_per_kv, has_ctx):
    if has_ctx:
        q_ref, kn_ref, vn_ref, kc_ref, vc_ref, _, o_ref = refs
    else:
        q_ref, kn_ref, vn_ref, _, o_ref = refs
    outs = []
    for h in range(hp):
        kv = slice(h // q_per_kv * LANE, (h // q_per_kv + 1) * LANE)
        q = q_ref[:, h * LANE:(h + 1) * LANE]
        lk = kn_ref.shape[0]
        ck = lk // ATT_CHUNKS if has_ctx else lk
        parts = [(kn_ref, vn_ref, k0, ck) for k0 in range(0, lk, ck)]
        if has_ctx:
            parts.append((kc_ref, vc_ref, 0, kc_ref.shape[0]))
        m = acc = None
        for k_ref, v_ref, k0, n in parts:
            s = _mm_nt(q, k_ref[k0:k0 + n, kv])
            m_c = jnp.max(s, axis=-1, keepdims=True)
            m_new = m_c if m is None else jnp.maximum(m, m_c)
            p = jnp.exp2(s - m_new)
            a_c = _mm(p.astype(BF16), v_ref[k0:k0 + n, kv])
            acc = a_c if m is None else jnp.exp2(m - m_new) * acc + a_c
            m = m_new
        outs.append(acc * (1.0 / acc[:, A_V:A_V + 1]))
    for p2 in range(hp // 2):
        pair = jnp.concatenate([outs[2 * p2][:, :A_V], outs[2 * p2 + 1][:, :A_V]], axis=-1)
        o_ref[:, p2 * LANE:(p2 + 1) * LANE] = pair.astype(BF16)


def _attn(q, kn, vn, ctx, into, *, nb, lq, lk, nheads, hp, q_per_kv, tq):
    ngroups = nheads // hp
    kvw = hp // q_per_kv * LANE
    nq = lq // tq
    ins = [q, kn, vn]
    in_specs = [pl.BlockSpec((tq, hp * LANE), lambda b, g, i: (b * nq + i, g)),
                pl.BlockSpec((lk, kvw), lambda b, g, i: (b, g)),
                pl.BlockSpec((lk, kvw), lambda b, g, i: (b, g))]
    row0 = 0
    if ctx is not None:
        ins += list(ctx)
        in_specs += [pl.BlockSpec((PAST_LEN, kvw), lambda b, g, i: (b, g))] * 2
        row0 = TP // tq
    ins.append(into)
    in_specs.append(pl.BlockSpec(memory_space=pl.ANY))
    aliases = {len(ins) - 1: 0}
    return pl.pallas_call(
        functools.partial(_attn_kernel, hp=hp, q_per_kv=q_per_kv, has_ctx=ctx is not None),
        grid=(nb, ngroups, nq),
        in_specs=in_specs,
        out_specs=pl.BlockSpec((tq, hp * A_V), lambda b, g, i: (row0 + b * nq + i, g)),
        out_shape=jax.ShapeDtypeStruct((T_ALL, nheads * A_V), BF16),
        input_output_aliases=aliases,
        compiler_params=_cp(("parallel", "parallel", "arbitrary")),
        name="attn_ctx" if ctx is not None else "attn",
    )(*ins)


def _outproj_kernel(*refs, glu, read_x, nx):
    x = read_x(refs[:nx])
    if glu:
        mod_ref, a_ref, b_ref, wa_ref, wb_ref, wg_ref, o_ref = refs[nx:]
        yd = jax.nn.gelu(b_ref[...])
        b = (yd * jax.nn.sigmoid(_mm(yd.astype(BF16), wg_ref[...]))).astype(BF16)
    else:
        mod_ref, a_ref, b_ref, wa_ref, wb_ref, o_ref = refs[nx:]
        b = b_ref[...]
    acc = _mm(a_ref[...], wa_ref[...]) + _mm(b, wb_ref[...])
    o_ref[...] = x + mod_ref[0, 2:3, :] * acc


def _outproj(x, mod, a, b, wa, wb, wg=None, tm=512):
    xs, x_specs, read_x = _token_specs(x, tm)
    t = sum(v.shape[0] for v in xs)
    d = xs[0].shape[1]
    full = lambda v: pl.BlockSpec(v.shape, lambda i: (0,) * v.ndim)
    rows = lambda w: pl.BlockSpec((tm, w), lambda i: (i, 0))
    ins = xs + [mod, a, b, wa, wb]
    in_specs = x_specs + [pl.BlockSpec((1, 8, d), lambda i: (_cond_idx(i, tm), 0, 0)), rows(a.shape[1]),
                          rows(b.shape[1]), full(wa), full(wb)]
    if wg is not None:
        ins.append(wg)
        in_specs.append(full(wg))
    return pl.pallas_call(
        functools.partial(_outproj_kernel, glu=wg is not None, read_x=read_x, nx=len(xs)),
        grid=(t // tm,),
        in_specs=in_specs,
        out_specs=rows(d),
        out_shape=jax.ShapeDtypeStruct((t, d), F32),
        compiler_params=_cp(("parallel",)),
        name="outproj_glu" if wg is not None else "outproj",
    )(*ins)


FF_SPLIT = (0, 768, FF_EXPERT)


MOE_BLK = 256
MOE_TAIL = 128


def _swiglu(h, w1, w3, w2):
    y = None
    for k in range(len(FF_SPLIT) - 1):
        sl = slice(FF_SPLIT[k], FF_SPLIT[k + 1])
        act = (_silu(_mm(h, w1[:, sl])) * _mm(h, w3[:, sl])).astype(BF16)
        part = _mm(act, w2[sl, :])
        y = part if y is None else y + part
    return y


def _ffn_kernel(*refs, routed, nj):
    if routed:
        (x_ref, mod_ref, g_ref, rwh_ref, rwl_ref, rb_ref, w1_ref, w3_ref, w2_ref, o_ref,
         h_sc, acc_sc, comb_sc, rank_sc, rank_t_sc, cnt_sc) = refs
    else:
        x_ref, mod_ref, g_ref, w1_ref, w3_ref, w2_ref, o_ref, h_sc, acc_sc = refs
    j = pl.program_id(1)
    tm = x_ref.shape[0]

    @pl.when(j == 0)
    def _():
        h = _modulate(x_ref[...], g_ref[...], mod_ref, 3, 4)
        h_hi = h.astype(BF16)
        h_sc[...] = h_hi
        acc_sc[...] = jnp.zeros(acc_sc.shape, F32)
        if routed:
            h_lo = (h - h_hi.astype(F32)).astype(BF16)
            logits = _mm(h_hi, rwh_ref[...]) + _mm(h_lo, rwh_ref[...]) + _mm(h_hi, rwl_ref[...]) + rb_ref[...]
            lane = lax.broadcasted_iota(jnp.int32, logits.shape, 1)
            lg = jnp.where(lane < N_EXPERTS, logits, -jnp.inf)
            m1 = jnp.max(lg, axis=-1, keepdims=True)
            i1 = jnp.min(jnp.where(lg == m1, lane, LANE), axis=-1, keepdims=True)
            lg2 = jnp.where(lane == i1, -jnp.inf, lg)
            m2 = jnp.max(lg2, axis=-1, keepdims=True)
            i2 = jnp.min(jnp.where(lg2 == m2, lane, LANE), axis=-1, keepdims=True)
            e2 = jnp.exp(m2 - m1)
            inv = 1.0 / (1.0 + e2)
            comb_sc[...] = jnp.where(lane == i1, inv, 0.0) + jnp.where(lane == i2, e2 * inv, 0.0)
            sel = jnp.where(lane == i1, 1.0, 0.0) + jnp.where(lane == i2, 1.0, 0.0)
            ti = lax.broadcasted_iota(jnp.int32, (tm, tm), 0)
            si = lax.broadcasted_iota(jnp.int32, (tm, tm), 1)
            before = _mm((si < ti).astype(BF16), sel.astype(BF16))
            rank = jnp.where(sel > 0.0, before, -1.0)
            rank_sc[...] = rank
            cnt_sc[...] = jnp.broadcast_to(before[tm - 1:tm, :] + sel[tm - 1:tm, :], cnt_sc.shape)
            for k in range(tm // LANE):
                rank_t_sc[:, k * LANE:(k + 1) * LANE] = rank[k * LANE:(k + 1) * LANE, :].T

    if routed:
        lane = lax.broadcasted_iota(jnp.int32, comb_sc.shape, 1)
        lane8 = lax.broadcasted_iota(jnp.int32, cnt_sc.shape, 1)
        n = jnp.sum(jnp.where(lane8 == j, cnt_sc[...], 0.0)[0:1, :]).astype(jnp.int32)
        shift = int(math.log2(MOE_BLK))
        nfull = lax.shift_right_logical(n, shift)
        rem = n - lax.shift_left(nfull, shift)
        gate = jnp.sum(jnp.where(lane == j, comb_sc[...], 0.0), axis=-1, keepdims=True)
        rank_col = jnp.sum(jnp.where(lane == j, rank_sc[...], 0.0), axis=-1, keepdims=True)
        rank_row = rank_t_sc[pl.ds(j, 1), :]

        def run(r0, rows):
            r0 = r0.astype(F32)
            r_iota = lax.broadcasted_iota(jnp.int32, (rows, tm), 0).astype(F32)
            c_iota = lax.broadcasted_iota(jnp.int32, (tm, rows), 1).astype(F32)
            pick = jnp.where(rank_row - r0 == r_iota, 1.0, 0.0).astype(BF16)
            hb = _mm(pick, h_sc[...]).astype(BF16)
            yb = _swiglu(hb, w1_ref.at[0, 0], w3_ref.at[0, 0], w2_ref.at[0, 0]).astype(BF16)
            place = jnp.where(rank_col - r0 == c_iota, 1.0, 0.0).astype(BF16)
            acc_sc[...] += gate * _mm(place, yb)

        def block(b, carry):
            run(b * MOE_BLK, MOE_BLK)
            return carry
        lax.fori_loop(0, nfull + (rem > MOE_TAIL).astype(jnp.int32), block, 0)

        @pl.when((rem > 0) & (rem <= MOE_TAIL))
        def _():
            run(nfull * MOE_BLK, MOE_TAIL)
    else:
        acc_sc[...] += _swiglu(h_sc[...], w1_ref.at[0], w3_ref.at[0], w2_ref.at[0])

    @pl.when(j == nj - 1)
    def _():
        o_ref[...] = x_ref[...] + mod_ref[0, 5:6, :] * acc_sc[...]


def _ffn(x, mod, g, w1, w3, w2, li, router=None, tm=1024):
    t, d = x.shape
    routed = router is not None
    tf = FF_EXPERT
    nj = w1.shape[1] if routed else w1.shape[2] // tf
    ins = [x, mod, g.reshape(1, d)]
    in_specs = [pl.BlockSpec((tm, d), lambda i, j: (i, 0)),
                pl.BlockSpec((1, 8, d), lambda i, j: (_cond_idx(i, tm), 0, 0)),
                pl.BlockSpec((1, d), lambda i, j: (0, 0))]
    scratch = [pltpu.VMEM((tm, d), BF16), pltpu.VMEM((tm, d), F32)]
    if routed:
        ins += list(router)
        in_specs += [pl.BlockSpec((d, LANE), lambda i, j: (0, 0))] * 2 + [pl.BlockSpec((1, LANE), lambda i, j: (0, 0))]
        in_specs += ([pl.BlockSpec((1, 1, d, tf), lambda i, j: (li, j, 0, 0))] * 2
                     + [pl.BlockSpec((1, 1, tf, d), lambda i, j: (li, j, 0, 0))])
        scratch += [pltpu.VMEM((tm, LANE), F32), pltpu.VMEM((tm, LANE), F32), pltpu.VMEM((LANE, tm), F32),
                    pltpu.VMEM((8, LANE), F32)]
    else:
        in_specs += ([pl.BlockSpec((1, d, tf), lambda i, j: (li, 0, j))] * 2
                     + [pl.BlockSpec((1, tf, d), lambda i, j: (li, j, 0))])
    ins += [w1, w3, w2]
    return pl.pallas_call(
        functools.partial(_ffn_kernel, routed=routed, nj=nj),
        grid=(t // tm, nj),
        in_specs=in_specs,
        out_specs=pl.BlockSpec((tm, d), lambda i, j: (i, 0)),
        out_shape=jax.ShapeDtypeStruct((t, d), F32),
        scratch_shapes=scratch,
        compiler_params=_cp(("parallel", "arbitrary")),
        name="moe" if routed else "ffn",
    )(*ins)


CONV_ROWS = 4096


def _conv_kernel(x_ref, w_ref, b_ref, o_ref):
    i = pl.program_id(0)
    seq = jnp.where(i < TP // CONV_ROWS, SEQ, DEC_SEQ)
    x = x_ref[...]
    pos = lax.broadcasted_iota(jnp.int32, x.shape, 0) & (seq - 1)
    prev = jnp.where(pos == 0, 0.0, pltpu.roll(x, 1, 0))
    nxt = jnp.where(pos == seq - 1, 0.0, pltpu.roll(x, CONV_ROWS - 1, 0))
    y = prev * w_ref[0:1, :] + x * w_ref[1:2, :] + nxt * w_ref[2:3, :] + b_ref[...]
    o_ref[...] = _silu(y)


def _conv(proj, w, b):
    tc = 256
    t = proj.shape[0]
    return pl.pallas_call(
        _conv_kernel,
        grid=(t // CONV_ROWS, C_CONV_DIM // tc),
        in_specs=[pl.BlockSpec((CONV_ROWS, tc), lambda i, j: (i, j + OD_XBC // tc)),
                  pl.BlockSpec((3, tc), lambda i, j: (0, j)),
                  pl.BlockSpec((1, tc), lambda i, j: (0, j))],
        out_specs=pl.BlockSpec((CONV_ROWS, tc), lambda i, j: (i, j)),
        out_shape=jax.ShapeDtypeStruct((t, C_CONV_DIM), F32),
        compiler_params=_cp(("parallel", "parallel")),
        name="conv",
    )(proj, w, b.reshape(1, C_CONV_DIM))


Q = SSD_CHUNK
NCH = T_ALL // Q
NCH_P = TP // Q
NC_P = SEQ // Q
NC_S = DEC_SEQ // Q
SSD_PAR = 2


def _chunk_seq(s):
    is_p = s < NCH_P
    nc = jnp.where(is_p, NC_P, NC_S)
    seq = jnp.where(is_p, s // NC_P, BATCH + (s - NCH_P) // NC_S)
    c = jnp.where(is_p, s % NC_P, (s - NCH_P) % NC_S)
    return seq, c, nc, s - c


def _chunk_mirror(s):
    _, c, nc, start = _chunk_seq(s)
    return start + nc - 1 - c


def _ssd_dt(dt_raw, dt_bias, a_row):
    z = dt_raw + dt_bias
    dt = jnp.maximum(z, 0.0) + jnp.log(1.0 + jnp.exp(-jnp.abs(z)))
    la = dt * a_row
    r = lax.broadcasted_iota(jnp.int32, (Q, Q), 0)
    cc = lax.broadcasted_iota(jnp.int32, (Q, Q), 1)
    tri = (cc <= r).astype(BF16)
    hi, lo = _split_bf16(la)
    return dt, la, _mm(tri, hi) + _mm(tri, lo)


def _expand(v, e32_ref):
    hi, lo = _split_bf16(v)
    return _mm(hi, e32_ref[...]) + _mm(lo, e32_ref[...])


def _ssd1_kernel(xf_ref, xb_ref, df_ref, db_ref, dtb_ref, a_ref, e32_ref, h0_ref, hf_o, hb_o, fin_o, hf_sc, hb_sc):
    s = SSD_PAR * pl.program_id(0)
    _, c, nc, _ = _chunk_seq(s)

    @pl.when(c == 0)
    def _():
        hf_sc[...] = h0_ref[0, 0]
        hb_sc[...] = h0_ref[0, 1]

    lane = lax.broadcasted_iota(jnp.int32, (Q, LANE), 1)
    is_f = lane < C_HEADS
    is_f8 = lax.broadcasted_iota(jnp.int32, (8, LANE), 1) < C_HEADS
    for k in range(SSD_PAR):
        rf = slice(k * Q, (k + 1) * Q)
        rb = slice((SSD_PAR - 1 - k) * Q, (SSD_PAR - k) * Q)
        hf_o[k] = hf_sc[...].astype(BF16)
        hb_o[SSD_PAR - 1 - k] = hb_sc[...].astype(BF16)
        dt_f, _, ac_f = _ssd_dt(df_ref[rf, :], dtb_ref[...], a_ref[...])
        tot_f = ac_f[Q - 1:Q, :]
        dt_b, la_b, ac_b = _ssd_dt(db_ref[rb, :], dtb_ref[...], a_ref[...])
        tot_b = ac_b[Q - 1:Q, :]
        w = jnp.where(is_f, jnp.exp(tot_f - ac_f) * dt_f, jnp.exp(ac_b - la_b) * dt_b)
        dec = jnp.where(is_f8, jnp.exp(jnp.broadcast_to(tot_f, (8, LANE))),
                        jnp.exp(jnp.broadcast_to(tot_b, (8, LANE))))
        w_x = _expand(w, e32_ref)
        dec_x = _expand(dec, e32_ref)[0:1, :]
        for d, (x_ref, r, st) in enumerate(((xf_ref, rf, hf_sc), (xb_ref, rb, hb_sc))):
            xw = (x_ref[r, 0:C_INNER] * w_x[:, d * C_INNER:(d + 1) * C_INNER]).astype(BF16)
            for g in range(C_GROUPS):
                bm_t = x_ref[r, C_INNER + g * C_STATE:C_INNER + (g + 1) * C_STATE].T.astype(BF16)
                cs = slice(g * C_HPG * C_HD, (g + 1) * C_HPG * C_HD)
                dcs = slice(d * C_INNER + g * C_HPG * C_HD, d * C_INNER + (g + 1) * C_HPG * C_HD)
                st[:, cs] = st[:, cs] * dec_x[:, dcs] + _mm(bm_t, xw[:, cs])

    @pl.when((c == nc - SSD_PAR) & (s < NCH_P))
    def _():
        for d, st in enumerate((hf_sc, hb_sc)):
            for k in range(C_INNER // LANE):
                fin_o[0, d, k * LANE:(k + 1) * LANE, :] = st[:, k * LANE:(k + 1) * LANE].T


def _ssd1(xa, proj, dtb, a_row, e32, h0):
    dcol = OD_DT // LANE
    rows = SSD_PAR * Q
    first = lambda i: SSD_PAR * i
    mirror = lambda i: _chunk_mirror(SSD_PAR * i + SSD_PAR - 1) // SSD_PAR
    h0_idx = lambda i: (jnp.where(first(i) < NCH_P, 0, 1 + (first(i) - NCH_P) // NC_S), 0, 0, 0)
    return pl.pallas_call(
        _ssd1_kernel,
        grid=(NCH // SSD_PAR,),
        in_specs=[pl.BlockSpec((rows, C_CONV_DIM), lambda i: (i, 0)),
                  pl.BlockSpec((rows, C_CONV_DIM), lambda i: (mirror(i), 0)),
                  pl.BlockSpec((rows, LANE), lambda i: (i, dcol)),
                  pl.BlockSpec((rows, LANE), lambda i: (mirror(i), dcol)),
                  pl.BlockSpec((1, LANE), lambda i: (0, 0)),
                  pl.BlockSpec((1, LANE), lambda i: (0, 0)),
                  pl.BlockSpec((LANE, 2 * C_INNER), lambda i: (0, 0)),
                  pl.BlockSpec((1, 2, C_STATE, C_INNER), h0_idx)],
        out_specs=[pl.BlockSpec((SSD_PAR, C_STATE, C_INNER), lambda i: (i, 0, 0)),
                   pl.BlockSpec((SSD_PAR, C_STATE, C_INNER), lambda i: (mirror(i), 0, 0)),
                   pl.BlockSpec((1, 2, C_INNER, C_STATE),
                                lambda i: (jnp.minimum(_chunk_seq(first(i))[0], BATCH - 1), 0, 0, 0))],
        out_shape=[jax.ShapeDtypeStruct((NCH, C_STATE, C_INNER), BF16),
                   jax.ShapeDtypeStruct((NCH, C_STATE, C_INNER), BF16),
                   jax.ShapeDtypeStruct((BATCH, 2, C_INNER, C_STATE), F32)],
        scratch_shapes=[pltpu.VMEM((C_STATE, C_INNER), F32)] * 2,
        compiler_params=_cp(("arbitrary",)),
        name="ssd_states",
    )(xa, xa, proj, proj, dtb, a_row, e32, h0)


def _ssd2_kernel(xa_ref, d_ref, z_ref, hf_ref, hb_ref, dtb_ref, a_ref, e32_ref, dsk_ref, gn_ref, o_ref, y_sc):
    for k in range(SSD_PAR):
        _ssd2_chunk(k, xa_ref, d_ref, z_ref, hf_ref, hb_ref, dtb_ref, a_ref, e32_ref, dsk_ref, gn_ref, o_ref, y_sc)


def _ssd2_chunk(k, xa_ref, d_ref, z_ref, hf_ref, hb_ref, dtb_ref, a_ref, e32_ref, dsk_ref, gn_ref, o_ref, y_sc):
    r = slice(k * Q, (k + 1) * Q)
    dt, la, acum = _ssd_dt(d_ref[r, :], dtb_ref[...], a_ref[...])
    tot = acum[Q - 1:Q, :]
    excl = acum - la
    lane = lax.broadcasted_iota(jnp.int32, (Q, LANE), 1)
    is_f = lane < C_HEADS
    logdt = jnp.log(dt)
    col = jnp.where(is_f, acum, excl)
    row_t = jnp.where(is_f, acum - logdt, excl + logdt).T
    dec_x = _expand(jnp.where(is_f, jnp.exp(acum), jnp.exp(tot - excl)), e32_ref)
    x = xa_ref[r, 0:C_INNER]
    xb16 = x.astype(BF16)
    ti = lax.broadcasted_iota(jnp.int32, (Q, Q), 0)
    si = lax.broadcasted_iota(jnp.int32, (Q, Q), 1)
    low = si <= ti
    upp = si >= ti
    lane_q = lax.broadcasted_iota(jnp.int32, (Q, LANE), 1)
    gw = C_HPG * C_HD
    for g in range(C_GROUPS):
        bm = xa_ref[r, C_INNER + g * C_STATE:C_INNER + (g + 1) * C_STATE].astype(BF16)
        cm = xa_ref[r, C_INNER + (C_GROUPS + g) * C_STATE:C_INNER + (C_GROUPS + g + 1) * C_STATE].astype(BF16)
        gmat = _mm_nt(cm, bm)
        cs = slice(g * gw, (g + 1) * gw)
        ys = (_mm(cm, hf_ref[k, :, cs]) * dec_x[:, cs]
              + _mm(cm, hb_ref[k, :, cs]) * dec_x[:, C_INNER + g * gw:C_INNER + (g + 1) * gw])
        for pr in range(C_HPG // 2):
            xpair = xb16[:, g * gw + pr * LANE:g * gw + (pr + 1) * LANE]
            res = []
            for e in range(2):
                h = g * C_HPG + 2 * pr + e
                hb_ = C_HEADS + h
                m_f = jnp.where(low, jnp.exp(col[:, h:h + 1] - row_t[h:h + 1, :]), 0.0)
                m_b = jnp.where(upp, jnp.exp(row_t[hb_:hb_ + 1, :] - col[:, hb_:hb_ + 1]), 0.0)
                res.append(_mm((gmat * (m_f + m_b)).astype(BF16), xpair))
            y_sc[r, g * gw + pr * LANE:g * gw + (pr + 1) * LANE] = (
                jnp.where(lane_q < C_HD, res[0], res[1]) + ys[:, pr * LANE:(pr + 1) * LANE])
    y = y_sc[r, :] + dsk_ref[...] * x
    y = y * _silu(z_ref[r, :])
    y = y * lax.rsqrt(jnp.mean(y * y, axis=-1, keepdims=True) + EPS) * gn_ref[...]
    o_ref[r, :] = y.astype(BF16)


def _ssd2(xa, proj, hf, hb, dtb, a_row, e32, dsk, gn):
    dcol = OD_DT // LANE
    rows = SSD_PAR * Q
    one = lambda w: pl.BlockSpec((1, w), lambda s: (0, 0))
    return pl.pallas_call(
        _ssd2_kernel,
        grid=(NCH // SSD_PAR,),
        in_specs=[pl.BlockSpec((rows, C_CONV_DIM), lambda s: (s, 0)),
                  pl.BlockSpec((rows, LANE), lambda s: (s, dcol)),
                  pl.BlockSpec((rows, C_INNER), lambda s: (s, 0)),
                  pl.BlockSpec((SSD_PAR, C_STATE, C_INNER), lambda s: (s, 0, 0)),
                  pl.BlockSpec((SSD_PAR, C_STATE, C_INNER), lambda s: (s, 0, 0)),
                  one(LANE), one(LANE),
                  pl.BlockSpec((LANE, 2 * C_INNER), lambda s: (0, 0)),
                  one(C_INNER), one(C_INNER)],
        out_specs=pl.BlockSpec((rows, C_INNER), lambda s: (s, 0)),
        out_shape=jax.ShapeDtypeStruct((T_ALL, C_INNER), BF16),
        scratch_shapes=[pltpu.VMEM((rows, C_INNER), F32)],
        compiler_params=_cp(("parallel",)),
        name="ssd_out",
    )(xa, proj, proj, hf, hb, dtb, a_row, e32, dsk, gn)


S5_MM_STEPS = 128


def _cmul(ar, ai, br, bi):
    return ar * br - ai * bi, ar * bi + ai * br


def _s5_kernel(u_ref, wx_ref, cw_ref, a_ref, dsk_ref, h0_ref, y_ref, st_ref, xs, *, nseg):
    ct = pl.program_id(0)
    quarter = ct % 4
    nslab = S5_STEPS // S5_MM_STEPS
    rows = S5_MM_STEPS * S5_CHAINS

    def fill(k, carry):
        r0 = pl.multiple_of(k * S5_MM_STEPS, S5_MM_STEPS)
        ub = u_ref[pl.ds(r0, S5_MM_STEPS)].reshape(rows, LANE).astype(BF16)
        xs[pl.ds(r0, S5_MM_STEPS)] = _mm(ub, wx_ref[0]).reshape(S5_MM_STEPS, S5_CHAINS, 4 * LANE)
        return carry
    lax.fori_loop(0, nslab, fill, 0)

    shape = (S5_CHAINS, LANE)
    a = a_ref[0]
    afr, afi, abr, abi = (jnp.broadcast_to(a[k:k + 1, :], shape) for k in range(4))

    def scan(init, store):
        def body(i, carry):
            hfr, hfi, hbr, hbi = carry
            j = S5_STEPS - 1 - i
            pr, pi = _cmul(afr, afi, hfr, hfi)
            nfr = pr + xs[i, :, 0:LANE]
            nfi = pi + xs[i, :, LANE:2 * LANE]
            pr, pi = _cmul(abr, abi, hbr, hbi)
            nbr = pr + xs[j, :, 2 * LANE:3 * LANE]
            nbi = pi + xs[j, :, 3 * LANE:4 * LANE]
            if store:
                xs[i, :, 0:LANE] = nfr
                xs[i, :, LANE:2 * LANE] = nfi
                xs[j, :, 2 * LANE:3 * LANE] = nbr
                xs[j, :, 3 * LANE:4 * LANE] = nbi
            return nfr, nfi, nbr, nbi
        return lax.fori_loop(0, S5_STEPS, body, init)

    zero = jnp.zeros(shape, F32)
    if nseg == 1:
        init = (zero, zero, zero, zero)
    else:
        efr, efi, ebr, ebi = scan((zero, zero, zero, zero), store=False)
        pfr, pfi, pbr, pbi = afr, afi, abr, abi
        for _ in range(int(math.log2(S5_STEPS))):
            pfr, pfi = _cmul(pfr, pfi, pfr, pfi)
            pbr, pbi = _cmul(pbr, pbi, pbr, pbi)
        row = lax.broadcasted_iota(jnp.int32, shape, 0) % nseg
        first = row == 0
        last = row == nseg - 1
        h0 = h0_ref[...]
        cfr, cfi, cbr, cbi = zero, zero, zero, zero
        dn = lambda v: pltpu.roll(v, 1, 0)
        up = lambda v: pltpu.roll(v, S5_CHAINS - 1, 0)
        for _ in range(nseg):
            pr, pi = _cmul(pfr, pfi, dn(cfr), dn(cfi))
            cfr = jnp.where(first, h0[0], pr + dn(efr))
            cfi = jnp.where(first, h0[1], pi + dn(efi))
            pr, pi = _cmul(pbr, pbi, up(cbr), up(cbi))
            cbr = jnp.where(last, h0[2], pr + up(ebr))
            cbi = jnp.where(last, h0[3], pi + up(ebi))
        init = (cfr, cfi, cbr, cbi)
    fin = scan(init, store=True)
    for k in range(4):
        st_ref[k] = fin[k]

    def emit(k, carry):
        r0 = pl.multiple_of(k * S5_MM_STEPS, S5_MM_STEPS)
        blk = xs[pl.ds(r0, S5_MM_STEPS)]
        s_re = blk[:, :, 0:LANE] + blk[:, :, 2 * LANE:3 * LANE]
        s_im = blk[:, :, LANE:2 * LANE] + blk[:, :, 3 * LANE:4 * LANE]
        s2 = jnp.concatenate([s_re, s_im], axis=-1).reshape(rows, 2 * LANE).astype(BF16)
        yv = _mm(s2, cw_ref[0]).reshape(S5_MM_STEPS, S5_CHAINS, LANE)

        @pl.when(quarter == 0)
        def _():
            y_ref[pl.ds(r0, S5_MM_STEPS)] = yv + dsk_ref[...] * u_ref[pl.ds(r0, S5_MM_STEPS)]

        @pl.when(quarter != 0)
        def _():
            y_ref[pl.ds(r0, S5_MM_STEPS)] += yv
        return carry
    lax.fori_loop(0, nslab, emit, 0)


def _s5(u_perm, sw, h0, nseg):
    nct = D_GROUPS // 2
    return pl.pallas_call(
        functools.partial(_s5_kernel, nseg=nseg),
        grid=(nct,),
        in_specs=[pl.BlockSpec((S5_STEPS, S5_CHAINS, LANE), lambda ct: (0, 0, ct // 4)),
                  pl.BlockSpec((1, LANE, 4 * LANE), lambda ct: (ct, 0, 0)),
                  pl.BlockSpec((1, 2 * LANE, LANE), lambda ct: (ct, 0, 0)),
                  pl.BlockSpec((1, 8, LANE), lambda ct: (ct, 0, 0)),
                  pl.BlockSpec((1, LANE), lambda ct: (0, ct // 4)),
                  pl.BlockSpec((4, S5_CHAINS, LANE), lambda ct: (0, 0, ct))],
        out_specs=[pl.BlockSpec((S5_STEPS, S5_CHAINS, LANE), lambda ct: (0, 0, ct // 4)),
                   pl.BlockSpec((4, S5_CHAINS, LANE), lambda ct: (0, 0, ct))],
        out_shape=[jax.ShapeDtypeStruct((S5_STEPS, S5_CHAINS, D_WIDTH), F32),
                   jax.ShapeDtypeStruct((4, S5_CHAINS, nct * LANE), F32)],
        scratch_shapes=[pltpu.VMEM((S5_STEPS, S5_CHAINS, 4 * LANE), F32)],
        compiler_params=_cp(("arbitrary",)),
        name="s5_seg" if nseg > 1 else "s5",
    )(u_perm, sw["wx"], sw["cw"], sw["a"], sw["dsk"], h0)


def _pad_last(a, n):
    return jnp.pad(a, [(0, 0)] * (a.ndim - 1) + [(0, n - a.shape[-1])])


def _even_weights(w_in, q_lora_g, w_uq, kv_lora_g, w_ukv, q_norm_g, k_norm_g, bq_g, bk_g):
    d = w_in.shape[0]
    s0, s1, s2, s3, s4 = 384, 640, 672, 1184, 1312
    kr = jnp.pad(w_in[:, s1:s2], ((0, 0), (A_NOPE, LANE - A_QK)))
    heads = lambda w, n: _pad_last(w.reshape(d, n, B_HD), LANE).reshape(d, n * LANE)
    w_perm = jnp.concatenate([w_in[:, :s0], w_in[:, s0:s1], kr, heads(w_in[:, s2:s3], B_HEADS),
                              heads(w_in[:, s3:s4], B_KV_HEADS), heads(w_in[:, s4:], B_KV_HEADS)], axis=1)
    row = lambda g: _pad_last(g.reshape(1, -1), LANE)
    return {
        "w_in": w_perm.astype(BF16),
        "gql": q_lora_g.reshape(1, -1),
        "wuq": _pad_last(w_uq, LANE).reshape(A_Q_LORA, A_HEADS * LANE).astype(BF16),
        "gqn": row(q_norm_g),
        "gkvl": kv_lora_g.reshape(1, -1),
        "wk": _pad_last(w_ukv[:, :, :A_NOPE], LANE).reshape(A_KV_LORA, A_HEADS * LANE).astype(BF16),
        "wv": _pad_last(w_ukv[:, :, A_NOPE:], LANE).reshape(A_KV_LORA, A_HEADS * LANE).astype(BF16),
        "gkn": row(k_norm_g),
        "gbq": row(bq_g),
        "gbk": row(bk_g),
    }


def _rope_tables():
    rows = DEC_SEQ // GRID_W

    def angles(dim):
        half = dim // 2
        inv = ROPE_THETA ** (-jnp.arange(0, half, 2, dtype=F32) / half)
        r = jnp.repeat(jnp.arange(rows, dtype=F32), GRID_W)
        col = jnp.tile(jnp.arange(GRID_W, dtype=F32), rows)
        return jnp.concatenate([r[:, None] * inv, col[:, None] * inv], axis=-1)

    def lanes(dim, start):
        ang = angles(dim)
        cos = jnp.repeat(jnp.cos(ang), 2, axis=-1)
        sin = jnp.repeat(jnp.sin(ang), 2, axis=-1) * jnp.tile(jnp.array([-1.0, 1.0], F32), dim // 2)
        pad = ((0, 0), (start, LANE - start - dim))
        return jnp.pad(cos, pad, constant_values=1.0), jnp.pad(sin, pad)

    cos_a, sin_a = lanes(A_ROPE, A_NOPE)
    cos_b, sin_b = lanes(B_HD, 0)
    return cos_a, sin_a, cos_b, sin_b


def _odd_weights(w_in, dt_bias, a_log, c_d_skip):
    w_perm = jnp.concatenate([w_in[:, :C_INNER], w_in[:, C_INNER:C_INNER + C_CONV_DIM],
                              w_in[:, C_INNER + C_CONV_DIM + 2 * C_HEADS:],
                              _pad_last(w_in[:, C_INNER + C_CONV_DIM:C_INNER + C_CONV_DIM + 2 * C_HEADS], LANE)], axis=1)
    lane = jnp.arange(LANE)
    col = jnp.arange(2 * C_INNER)
    e32 = ((lane[:, None] < 2 * C_HEADS) & (lane[:, None] == col[None, :] // C_HD)).astype(BF16)
    return {
        "w_in": w_perm.astype(BF16),
        "dtb": _pad_last(dt_bias.reshape(1, -1), LANE),
        "a_row": _pad_last(-jnp.exp(a_log).reshape(1, -1), LANE),
        "e32": e32,
        "dsk": jnp.repeat(c_d_skip, C_HD).reshape(1, C_INNER),
    }


def _s5_weights(lam_re, lam_im, log_dt, b_re, b_im, c_re, c_im, d_skip):
    nct = D_GROUPS // 2
    dt = jnp.exp(log_dt)[..., None]
    mag = jnp.exp(lam_re * dt)
    ar, ai = mag * jnp.cos(lam_im * dt), mag * jnp.sin(lam_im * dt)
    den = lam_re * lam_re + lam_im * lam_im
    kr = ((ar - 1.0) * lam_re + ai * lam_im) / den
    ki = (ai * lam_re - (ar - 1.0) * lam_im) / den
    xr = kr[..., None] * b_re[None] - ki[..., None] * b_im[None]
    xi = kr[..., None] * b_im[None] + ki[..., None] * b_re[None]
    xc = jnp.stack([xr[0], xi[0], xr[1], xi[1]]).reshape(4, nct, 2, D_STATE, D_GROUP_SIZE)
    quarter = (jnp.arange(nct)[:, None] % 4 == jnp.arange(4)[None, :]).astype(F32)
    eye2 = jnp.eye(2, dtype=F32)
    wx = jnp.einsum("tq,gh,ktgnj->tqgjkhn", quarter, eye2, xc).reshape(nct, LANE, 4 * LANE)
    cc = jnp.stack([c_re, -c_im]).reshape(2, nct, 2, D_GROUP_SIZE, D_STATE)
    cw = jnp.einsum("tq,gh,rtgcn->trhnqgc", quarter, eye2, cc).reshape(nct, 2 * LANE, LANE)
    planes = jnp.stack([ar[0], ai[0], ar[1], ai[1]]).reshape(4, nct, LANE).transpose(1, 0, 2)
    return {"wx": wx.astype(BF16), "cw": cw.astype(BF16), "a": jnp.pad(planes, ((0, 0), (0, 4), (0, 0))),
            "dsk": d_skip.reshape(1, D_WIDTH)}


def _even_layer(x, mod, norm1_g, norm2_g, ew, w_out, ffn_w, caches, rope_tabs):
    c_ckv, c_krope, c_kb, c_vb = caches
    proj = _modmm(x, mod, norm1_g, ew["w_in"])
    qa_p, ka_p, va_p, qb_p, kb_p, vb_p, ckv_c, kr_c, kbn_c, vb_c = _evenprep(proj, 0, TP, ew, None)
    qa_s, ka_s, va_s, qb_s, kb_s, vb_s = _evenprep(proj, TP, TS, ew, rope_tabs)
    n_ctx = DEC_BATCH * PAST_LEN
    kr_blk = jnp.pad(c_krope.reshape(n_ctx, A_ROPE), ((0, 0), (A_NOPE, LANE - A_QK)))
    ka_c, va_c = _ctxkv(c_ckv.reshape(n_ctx, A_KV_LORA), kr_blk, ew)
    heads_b = lambda a: _pad_last(a.transpose(0, 2, 1, 3), LANE).reshape(n_ctx, B_KV_HEADS * LANE)
    ones_col = (jnp.arange(B_KV_HEADS * LANE) % LANE == A_V).astype(F32)
    gqa = B_HEADS // B_KV_HEADS
    oa = jnp.zeros((T_ALL, A_HEADS * A_V), BF16)
    ob = jnp.zeros((T_ALL, B_HEADS * A_V), BF16)
    oa = _attn(qa_p, ka_p, va_p, None, oa, nb=BATCH, lq=SEQ, lk=SEQ, nheads=A_HEADS, hp=8, q_per_kv=1, tq=SEQ)
    ob = _attn(qb_p, kb_p, vb_p, None, ob, nb=BATCH, lq=SEQ, lk=SEQ, nheads=B_HEADS, hp=4, q_per_kv=gqa, tq=SEQ)
    oa = _attn(qa_s, ka_s, va_s, (ka_c, va_c), oa, nb=DEC_BATCH, lq=DEC_SEQ, lk=DEC_SEQ, nheads=A_HEADS, hp=4,
               q_per_kv=1, tq=ATT_TQ)
    ctx_b = (heads_b(c_kb).astype(BF16), (heads_b(c_vb) + ones_col).astype(BF16))
    ob = _attn(qb_s, kb_s, vb_s, ctx_b, ob, nb=DEC_BATCH, lq=DEC_SEQ, lk=DEC_SEQ, nheads=B_HEADS, hp=8,
               q_per_kv=gqa, tq=ATT_TQ)
    x = _outproj(x, mod, oa, ob, w_out[:A_HEADS * A_V], w_out[A_HEADS * A_V:])
    x = _ffn(x, mod, norm2_g, *ffn_w)
    new_cache = (ckv_c.reshape(BATCH, SEQ, A_KV_LORA), kr_c[:, A_NOPE:A_QK].reshape(BATCH, SEQ, A_ROPE),
                 kbn_c, vb_c)
    return x, new_cache


def _odd_layer(x, mod, norm1_g, norm2_g, ow, sw, conv_w, conv_b, c_norm_g, w_glu, w_out, moe_w, states):
    st_c, st_re, st_im = states
    proj = _modmm(x, mod, norm1_g, ow["w_in"])
    xa = _conv(proj, conv_w, conv_b)
    h0 = st_c.transpose(0, 1, 4, 2, 3).reshape(DEC_BATCH, 2, C_STATE, C_INNER)
    h0 = jnp.concatenate([jnp.zeros((1,) + h0.shape[1:], F32), h0], axis=0)
    hf, hb, fin = _ssd1(xa, proj, ow["dtb"], ow["a_row"], ow["e32"], h0)
    yc = _ssd2(xa, proj, hf, hb, ow["dtb"], ow["a_row"], ow["e32"], ow["dsk"], c_norm_g.reshape(1, C_INNER))
    u = proj[:, OD_U:OD_U + D_WIDTH]
    u_p = u[:TP].reshape(BATCH, SEQ, D_WIDTH).transpose(1, 0, 2)
    u_s = u[TP:].reshape(DEC_BATCH * S5_SEG, S5_STEPS, D_WIDTH).transpose(1, 0, 2)
    zeros_h0 = jnp.zeros((4, S5_CHAINS, D_GROUPS * D_STATE), F32)
    h0_s = jnp.stack([st_re[:, 0], st_im[:, 0], st_re[:, 1], st_im[:, 1]]).reshape(4, DEC_BATCH, D_GROUPS * D_STATE)
    h0_s = jnp.repeat(h0_s, S5_SEG, axis=1)
    yd_p, fin_d = _s5(u_p, sw, zeros_h0, 1)
    yd_s, _ = _s5(u_s, sw, h0_s, S5_SEG)
    yd = jnp.concatenate([yd_p.transpose(1, 0, 2).reshape(TP, D_WIDTH),
                          yd_s.transpose(1, 0, 2).reshape(TS, D_WIDTH)], axis=0)
    x = _outproj(x, mod, yc, yd, w_out[:C_INNER], w_out[C_INNER:], wg=w_glu)
    x = _ffn(x, mod, norm2_g, *moe_w[:4], router=moe_w[4])
    new_c = fin.reshape(BATCH, 2, C_HEADS, C_HD, C_STATE)
    fin_d = fin_d.reshape(4, BATCH, D_GROUPS, D_STATE)
    new_re = jnp.stack([fin_d[0], fin_d[2]], axis=1)
    new_im = jnp.stack([fin_d[1], fin_d[3]], axis=1)
    return x, (new_c, new_re, new_im)


def kernel(x_prompt, x_sample, cache_a_ckv, cache_a_krope, cache_b_k, cache_b_v, state_c_ssm, state_d_re, state_d_im, c, c_ctx, w_mod, b_mod, norm1_g, norm2_g, ev_w_in, ev_a_q_lora_g, ev_a_w_uq, ev_a_kv_lora_g, ev_a_w_ukv, ev_a_q_norm_g, ev_a_k_norm_g, ev_b_q_norm_g, ev_b_k_norm_g, ev_w_out, ffn_w1, ffn_w3, ffn_w2, od_w_in, od_conv_w, od_conv_b, od_dt_bias, od_a_log, od_c_d_skip, od_c_norm_g, od_lam_re, od_lam_im, od_log_dt, od_b_re, od_b_im, od_c_re, od_c_im, od_d_skip, od_w_glu, od_w_out, moe_router_w, moe_router_b, moe_w1, moe_w3, moe_w2):
    x = (x_prompt.reshape(TP, D_MODEL), x_sample.reshape(TS, D_MODEL))
    cond8 = jnp.zeros((8, D_MODEL), F32).at[0].set(c_ctx).at[1:1 + DEC_BATCH].set(c)
    mods = _adaln(cond8, w_mod, b_mod)[:, :1 + DEC_BATCH].reshape(DEPTH, 1 + DEC_BATCH, 6, D_MODEL)
    mods = jnp.pad(mods, ((0, 0), (0, 0), (0, 2), (0, 0)))
    rope_tabs = _rope_tables()
    ffn_bf = (ffn_w1.astype(BF16), ffn_w3.astype(BF16), ffn_w2.astype(BF16))
    moe_bf = (moe_w1.astype(BF16), moe_w3.astype(BF16), moe_w2.astype(BF16))
    ev_new, od_new = [], []
    for layer in range(DEPTH):
        mod = mods[layer]
        if layer % 2 == 0:
            e = layer // 2
            ew = _even_weights(ev_w_in[e], ev_a_q_lora_g[e], ev_a_w_uq[e], ev_a_kv_lora_g[e], ev_a_w_ukv[e],
                               ev_a_q_norm_g[e], ev_a_k_norm_g[e], ev_b_q_norm_g[e], ev_b_k_norm_g[e])
            ffn_w = ffn_bf + (e,)
            x, new = _even_layer(x, mod, norm1_g[layer], norm2_g[layer], ew, ev_w_out[e].astype(BF16), ffn_w,
                                 (cache_a_ckv[:, e], cache_a_krope[:, e], cache_b_k[:, e], cache_b_v[:, e]),
                                 rope_tabs)
            ev_new.append(new)
        else:
            o = layer // 2
            ow = _odd_weights(od_w_in[o], od_dt_bias[o], od_a_log[o], od_c_d_skip[o])
            sw = _s5_weights(od_lam_re[o], od_lam_im[o], od_log_dt[o], od_b_re[o], od_b_im[o], od_c_re[o],
                             od_c_im[o], od_d_skip[o])
            rw_hi = _pad_last(moe_router_w[o], LANE).astype(BF16)
            rw_lo = (_pad_last(moe_router_w[o], LANE) - rw_hi.astype(F32)).astype(BF16)
            moe_w = moe_bf + (o, (rw_hi, rw_lo, _pad_last(moe_router_b[o].reshape(1, -1), LANE)))
            x, new = _odd_layer(x, mod, norm1_g[layer], norm2_g[layer], ow, sw, od_conv_w[o], od_conv_b[o],
                                od_c_norm_g[o], od_w_glu[o].astype(BF16), od_w_out[o].astype(BF16), moe_w,
                                (state_c_ssm[:, o], state_d_re[:, o], state_d_im[:, o]))
            od_new.append(new)
    stack = lambda items, k: jnp.stack([it[k] for it in items], axis=1)
    return (x[:TP].reshape(BATCH, SEQ, D_MODEL), x[TP:].reshape(DEC_BATCH, DEC_SEQ, D_MODEL),
            stack(ev_new, 0), stack(ev_new, 1), stack(ev_new, 2), stack(ev_new, 3),
            stack(od_new, 0), stack(od_new, 1), stack(od_new, 2))
```

```python
import functools
import math

import jax
import jax.numpy as jnp
from jax import lax
from jax.experimental import pallas as pl
from jax.experimental.pallas import tpu as pltpu

F32 = jnp.float32
BF16 = jnp.bfloat16

D_MODEL = 1024
BATCH = 32
SEQ = 256
DEPTH = 4
DEC_BATCH = 2
DEC_SEQ = 4096
PAST_LEN = 256
GRID_W = 64
ROPE_THETA = 10000.0
EPS = 1e-6
A_HEADS = 8
A_NOPE = 64
A_ROPE = 32
A_QK = A_NOPE + A_ROPE
A_V = 64
A_Q_LORA = 384
A_KV_LORA = 256
B_HEADS = 8
B_KV_HEADS = 2
B_HD = 64
C_HEADS = 16
C_HD = 64
C_GROUPS = 2
C_HPG = C_HEADS // C_GROUPS
C_STATE = 128
C_INNER = C_HEADS * C_HD
C_CONV_DIM = C_INNER + 2 * C_GROUPS * C_STATE
SSD_CHUNK = 128
D_GROUP_SIZE = 16
D_GROUPS = 32
D_WIDTH = D_GROUPS * D_GROUP_SIZE
D_STATE = 64
FF_DENSE = 2816
N_EXPERTS = 8
FF_EXPERT = 1408

TP = BATCH * SEQ
TS = DEC_BATCH * DEC_SEQ
T_ALL = TP + TS
LANE = 128
LOG2E = 1.4426950408889634
VMEM_LIMIT = 56 * 1024 * 1024

EV_CQ = 0
EV_CKV = 384
EV_KR = 640
EV_QB = 768
EV_KB = EV_QB + B_HEADS * LANE
EV_VB = EV_KB + B_KV_HEADS * LANE
EV_N = EV_VB + B_KV_HEADS * LANE
OD_Z = 0
OD_XBC = C_INNER
OD_U = OD_XBC + C_CONV_DIM
OD_DT = OD_U + D_WIDTH
OD_N = OD_DT + LANE
S5_STEPS = 256
S5_CHAINS = 32
S5_SEG = DEC_SEQ // S5_STEPS
ATT_TQ = 256
ATT_CHUNKS = 2


def _cp(sem):
    return pltpu.CompilerParams(dimension_semantics=sem, vmem_limit_bytes=VMEM_LIMIT)


def _cond_idx(i, tm):
    t0 = i * tm
    return jnp.where(t0 < TP, 0, 1 + (t0 - TP) // DEC_SEQ)


def _mm(a, b):
    return jnp.dot(a, b, preferred_element_type=F32)


def _mm_nt(a, b):
    return lax.dot_general(a, b, (((1,), (1,)), ((), ())), preferred_element_type=F32)


def _split_bf16(v):
    hi = v.astype(BF16)
    lo = (v - hi.astype(F32)).astype(BF16)
    return hi, lo


def _silu(v):
    return v * jax.nn.sigmoid(v)


def _rms(v, n):
    return v * lax.rsqrt(jnp.sum(v * v, axis=-1, keepdims=True) * (1.0 / n) + EPS)


def _adaln_kernel(c_ref, w_ref, b_ref, o_ref):
    s = _silu(c_ref[...])
    o_ref[0] = _mm(s.astype(BF16), w_ref[0].astype(BF16)) + b_ref[0]


def _adaln(cond8, w_mod, b_mod):
    depth, d, n6 = w_mod.shape
    tn = 1536
    return pl.pallas_call(
        _adaln_kernel,
        grid=(depth, n6 // tn),
        in_specs=[pl.BlockSpec((8, d), lambda l, j: (0, 0)),
                  pl.BlockSpec((1, d, tn), lambda l, j: (l, 0, j)),
                  pl.BlockSpec((1, 1, tn), lambda l, j: (l, 0, j))],
        out_specs=pl.BlockSpec((1, 8, tn), lambda l, j: (l, 0, j)),
        out_shape=jax.ShapeDtypeStruct((depth, 8, n6), F32),
        compiler_params=_cp(("parallel", "parallel")),
        name="adaln",
    )(cond8, w_mod, b_mod.reshape(depth, 1, n6))


def _modulate(x, g, mod_ref, shift_row, scale_row):
    y = x * lax.rsqrt(jnp.mean(x * x, axis=-1, keepdims=True) + EPS) * g
    return y * (1.0 + mod_ref[0, scale_row:scale_row + 1, :]) + mod_ref[0, shift_row:shift_row + 1, :]


def _token_specs(x, tm):
    if not isinstance(x, tuple):
        return [x], [pl.BlockSpec((tm, x.shape[1]), lambda i, *_: (i, 0))], lambda refs: refs[0][...]
    n0 = x[0].shape[0] // tm
    n1 = x[1].shape[0] // tm
    specs = [pl.BlockSpec((tm, x[0].shape[1]), lambda i, *_: (jnp.minimum(i, n0 - 1), 0)),
             pl.BlockSpec((tm, x[1].shape[1]), lambda i, *_: (jnp.clip(i - n0, 0, n1 - 1), 0))]
    return list(x), specs, lambda refs: jnp.where(pl.program_id(0) < n0, refs[0][...], refs[1][...])


def _modmm_kernel(*refs, read_x, nx):
    mod_ref, g_ref, w_ref, o_ref = refs[nx:]
    h = _modulate(read_x(refs[:nx]), g_ref[...], mod_ref, 0, 1)
    o_ref[...] = _mm(h.astype(BF16), w_ref[...])


def _modmm(x, mod, g, w, tm=512):
    xs, x_specs, read_x = _token_specs(x, tm)
    t = sum(a.shape[0] for a in xs)
    d = xs[0].shape[1]
    n = w.shape[1]
    return pl.pallas_call(
        functools.partial(_modmm_kernel, read_x=read_x, nx=len(xs)),
        grid=(t // tm,),
        in_specs=x_specs + [pl.BlockSpec((1, 8, d), lambda i: (_cond_idx(i, tm), 0, 0)),
                            pl.BlockSpec((1, d), lambda i: (0, 0)),
                            pl.BlockSpec((d, n), lambda i: (0, 0))],
        out_specs=pl.BlockSpec((tm, n), lambda i: (i, 0)),
        out_shape=jax.ShapeDtypeStruct((t, n), F32),
        compiler_params=_cp(("parallel",)),
        name="modmm",
    )(*xs, mod, g.reshape(1, d), w)


def _swap_pairs(v):
    lane = lax.broadcasted_iota(jnp.int32, v.shape, 1)
    nxt = pltpu.roll(v, LANE - 1, 1)
    prv = pltpu.roll(v, 1, 1)
    return jnp.where((lane & 1) == 0, nxt, prv)


def _rope(v, cos, sin):
    return v * cos + _swap_pairs(v) * sin


def _with_ones(v):
    lane = lax.broadcasted_iota(jnp.int32, v.shape, 1)
    return jnp.where((lane & (LANE - 1)) == A_V, 1.0, v)


def _mla_kv(ckvn, kr_blk, wk_ref, wv_ref, gkn, rope_tabs, ka_o, va_o):
    cb = ckvn.astype(BF16)
    kn = _mm(cb, wk_ref[...])
    va_o[...] = _with_ones(_mm(cb, wv_ref[...])).astype(BF16)
    for h in range(A_HEADS):
        sl = slice(h * LANE, (h + 1) * LANE)
        kh = _rms(kn[:, sl] + kr_blk, A_QK) * gkn
        if rope_tabs is not None:
            kh = _rope(kh, rope_tabs[0], rope_tabs[1])
        ka_o[:, sl] = kh.astype(BF16)


def _evenprep_kernel(*refs, rope):
    it = iter(refs)
    proj = next(it)
    if rope:
        cos_a, sin_a, cos_b, sin_b = (next(it)[...] for _ in range(4))
    gql, wuq, gqn, gkvl, wk, wv, gkn, gbq, gbk = (next(it) for _ in range(9))
    qa_o, ka_o, va_o, qb_o, kb_o, vb_o = (next(it) for _ in range(6))
    if not rope:
        ckv_o, kr_o, kbn_o, vbf_o = (next(it) for _ in range(4))

    cqn = _rms(proj[:, EV_CQ:EV_CQ + A_Q_LORA], A_Q_LORA) * gql[...]
    q = _mm(cqn.astype(BF16), wuq[...])
    for h in range(A_HEADS):
        sl = slice(h * LANE, (h + 1) * LANE)
        qh = _rms(q[:, sl], A_QK) * gqn[...]
        if rope:
            qh = _rope(qh, cos_a, sin_a)
        qa_o[:, sl] = (qh * (A_QK ** -0.5 * LOG2E)).astype(BF16)

    ckvn = _rms(proj[:, EV_CKV:EV_CKV + A_KV_LORA], A_KV_LORA) * gkvl[...]
    kr_blk = proj[:, EV_KR:EV_KR + LANE]
    _mla_kv(ckvn, kr_blk, wk, wv, gkn[...], (cos_a, sin_a) if rope else None, ka_o, va_o)

    for h in range(B_HEADS):
        sl = slice(h * LANE, (h + 1) * LANE)
        qh = _rms(proj[:, EV_QB + h * LANE:EV_QB + (h + 1) * LANE], B_HD) * gbq[...]
        if rope:
            qh = _rope(qh, cos_b, sin_b)
        qb_o[:, sl] = (qh * (B_HD ** -0.5 * LOG2E)).astype(BF16)
    for j in range(B_KV_HEADS):
        sl = slice(j * LANE, (j + 1) * LANE)
        kh = _rms(proj[:, EV_KB + j * LANE:EV_KB + (j + 1) * LANE], B_HD) * gbk[...]
        if not rope:
            kbn_o[0, j] = kh[:, :B_HD]
        else:
            kh = _rope(kh, cos_b, sin_b)
        kb_o[:, sl] = kh.astype(BF16)
    vb = proj[:, EV_VB:EV_VB + B_KV_HEADS * LANE]
    vb_o[...] = _with_ones(vb).astype(BF16)
    if not rope:
        for j in range(B_KV_HEADS):
            vbf_o[0, j] = vb[:, j * LANE:j * LANE + B_HD]
        ckv_o[...] = ckvn
        kr_o[...] = kr_blk


def _evenprep(proj, row_off, nrows, ew, rope_tabs):
    tm = 256
    rope = rope_tabs is not None
    off = row_off // tm
    full = lambda a: pl.BlockSpec(a.shape, lambda i: (0,) * a.ndim)
    rows = lambda w: pl.BlockSpec((tm, w), lambda i: (i, 0))
    ins = [proj]
    in_specs = [pl.BlockSpec((tm, EV_N), lambda i: (i + off, 0))]
    if rope:
        ins += list(rope_tabs)
        in_specs += [pl.BlockSpec((tm, LANE), lambda i: (i % (DEC_SEQ // tm), 0))] * 4
    ws = [ew["gql"], ew["wuq"], ew["gqn"], ew["gkvl"], ew["wk"], ew["wv"], ew["gkn"], ew["gbq"], ew["gbk"]]
    ins += ws
    in_specs += [full(a) for a in ws]
    hw = A_HEADS * LANE
    kvw = B_KV_HEADS * LANE
    out_shape = [jax.ShapeDtypeStruct((nrows, hw), BF16)] * 4 + [jax.ShapeDtypeStruct((nrows, kvw), BF16)] * 2
    out_specs = [rows(hw)] * 4 + [rows(kvw)] * 2
    if not rope:
        assert tm == SEQ
        cache_b = jax.ShapeDtypeStruct((nrows // SEQ, B_KV_HEADS, SEQ, B_HD), F32)
        cache_b_spec = pl.BlockSpec((1, B_KV_HEADS, SEQ, B_HD), lambda i: (i, 0, 0, 0))
        out_shape += [jax.ShapeDtypeStruct((nrows, A_KV_LORA), F32), jax.ShapeDtypeStruct((nrows, LANE), F32),
                      cache_b, cache_b]
        out_specs += [rows(A_KV_LORA), rows(LANE), cache_b_spec, cache_b_spec]
    return pl.pallas_call(
        functools.partial(_evenprep_kernel, rope=rope),
        grid=(nrows // tm,),
        in_specs=in_specs,
        out_specs=out_specs,
        out_shape=out_shape,
        compiler_params=_cp(("parallel",)),
        name="evenprep_rope" if rope else "evenprep",
    )(*ins)


def _ctxkv_kernel(ckv_ref, kr_ref, wk, wv, gkn, ka_o, va_o):
    _mla_kv(ckv_ref[...], kr_ref[...], wk, wv, gkn[...], None, ka_o, va_o)


def _ctxkv(ckvn, kr_blk, ew):
    n = ckvn.shape[0]
    hw = A_HEADS * LANE
    return pl.pallas_call(
        _ctxkv_kernel,
        out_shape=[jax.ShapeDtypeStruct((n, hw), BF16)] * 2,
        compiler_params=pltpu.CompilerParams(vmem_limit_bytes=VMEM_LIMIT),
        name="ctxkv",
    )(ckvn, kr_blk, ew["wk"], ew["wv"], ew["gkn"])


def _attn_kernel(*refs, hp, q_per_kv, has_ctx):
    if has_ctx:
        q_ref, kn_ref, vn_ref, kc_ref, vc_ref, _, o_ref = refs
    else:
        q_ref, kn_ref, vn_ref, _, o_ref = refs
    outs = []
    for h in range(hp):
        kv = slice(h // q_per_kv * LANE, (h // q_per_kv + 1) * LANE)
        q = q_ref[:, h * LANE:(h + 1) * LANE]
        lk = kn_ref.shape[0]
        ck = lk // ATT_CHUNKS if has_ctx else lk
        parts = [(kn_ref, vn_ref, k0, ck) for k0 in range(0, lk, ck)]
        if has_ctx:
            parts.append((kc_ref, vc_ref, 0, kc_ref.shape[0]))
        m = acc = None
        for k_ref, v_ref, k0, n in parts:
            s = _mm_nt(q, k_ref[k0:k0 + n, kv])
            m_c = jnp.max(s, axis=-1, keepdims=True)
            m_new = m_c if m is None else jnp.maximum(m, m_c)
            p = jnp.exp2(s - m_new)
            a_c = _mm(p.astype(BF16), v_ref[k0:k0 + n, kv])
            acc = a_c if m is None else jnp.exp2(m - m_new) * acc + a_c
            m = m_new
        outs.append(acc * (1.0 / acc[:, A_V:A_V + 1]))
    for p2 in range(hp // 2):
        pair = jnp.concatenate([outs[2 * p2][:, :A_V], outs[2 * p2 + 1][:, :A_V]], axis=-1)
        o_ref[:, p2 * LANE:(p2 + 1) * LANE] = pair.astype(BF16)


def _attn(q, kn, vn, ctx, into, *, nb, lq, lk, nheads, hp, q_per_kv, tq):
    ngroups = nheads // hp
    kvw = hp // q_per_kv * LANE
    nq = lq // tq
    ins = [q, kn, vn]
    in_specs = [pl.BlockSpec((tq, hp * LANE), lambda b, g, i: (b * nq + i, g)),
                pl.BlockSpec((lk, kvw), lambda b, g, i: (b, g)),
                pl.BlockSpec((lk, kvw), lambda b, g, i: (b, g))]
    row0 = 0
    if ctx is not None:
        ins += list(ctx)
        in_specs += [pl.BlockSpec((PAST_LEN, kvw), lambda b, g, i: (b, g))] * 2
        row0 = TP // tq
    ins.append(into)
    in_specs.append(pl.BlockSpec(memory_space=pl.ANY))
    aliases = {len(ins) - 1: 0}
    return pl.pallas_call(
        functools.partial(_attn_kernel, hp=hp, q_per_kv=q_per_kv, has_ctx=ctx is not None),
        grid=(nb, ngroups, nq),
        in_specs=in_specs,
        out_specs=pl.BlockSpec((tq, hp * A_V), lambda b, g, i: (row0 + b * nq + i, g)),
        out_shape=jax.ShapeDtypeStruct((T_ALL, nheads * A_V), BF16),
        input_output_aliases=aliases,
        compiler_params=_cp(("parallel", "parallel", "arbitrary")),
        name="attn_ctx" if ctx is not None else "attn",
    )(*ins)


def _outproj_kernel(*refs, glu, read_x, nx):
    x = read_x(refs[:nx])
    if glu:
        mod_ref, a_ref, b_ref, wa_ref, wb_ref, wg_ref, o_ref = refs[nx:]
        yd = jax.nn.gelu(b_ref[...])
        b = (yd * jax.nn.sigmoid(_mm(yd.astype(BF16), wg_ref[...]))).astype(BF16)
    else:
        mod_ref, a_ref, b_ref, wa_ref, wb_ref, o_ref = refs[nx:]
        b = b_ref[...]
    acc = _mm(a_ref[...], wa_ref[...]) + _mm(b, wb_ref[...])
    o_ref[...] = x + mod_ref[0, 2:3, :] * acc


def _outproj(x, mod, a, b, wa, wb, wg=None, tm=512):
    xs, x_specs, read_x = _token_specs(x, tm)
    t = sum(v.shape[0] for v in xs)
    d = xs[0].shape[1]
    full = lambda v: pl.BlockSpec(v.shape, lambda i: (0,) * v.ndim)
    rows = lambda w: pl.BlockSpec((tm, w), lambda i: (i, 0))
    ins = xs + [mod, a, b, wa, wb]
    in_specs = x_specs + [pl.BlockSpec((1, 8, d), lambda i: (_cond_idx(i, tm), 0, 0)), rows(a.shape[1]),
                          rows(b.shape[1]), full(wa), full(wb)]
    if wg is not None:
        ins.append(wg)
        in_specs.append(full(wg))
    return pl.pallas_call(
        functools.partial(_outproj_kernel, glu=wg is not None, read_x=read_x, nx=len(xs)),
        grid=(t // tm,),
        in_specs=in_specs,
        out_specs=rows(d),
        out_shape=jax.ShapeDtypeStruct((t, d), F32),
        compiler_params=_cp(("parallel",)),
        name="outproj_glu" if wg is not None else "outproj",
    )(*ins)


FF_SPLIT = (0, 768, FF_EXPERT)


MOE_BLK = 256
MOE_TAIL = 128


def _swiglu(h, w1, w3, w2):
    y = None
    for k in range(len(FF_SPLIT) - 1):
        sl = slice(FF_SPLIT[k], FF_SPLIT[k + 1])
        act = (_silu(_mm(h, w1[:, sl])) * _mm(h, w3[:, sl])).astype(BF16)
        part = _mm(act, w2[sl, :])
        y = part if y is None else y + part
    return y


def _ffn_kernel(*refs, routed, nj, split):
    o2_ref = None
    if routed and split:
        (x_ref, mod_ref, g_ref, rwh_ref, rwl_ref, rb_ref, w1_ref, w3_ref, w2_ref, o_ref, o2_ref,
         h_sc, acc_sc, comb_sc, rank_sc, rank_t_sc, cnt_sc) = refs
    elif routed:
        (x_ref, mod_ref, g_ref, rwh_ref, rwl_ref, rb_ref, w1_ref, w3_ref, w2_ref, o_ref,
         h_sc, acc_sc, comb_sc, rank_sc, rank_t_sc, cnt_sc) = refs
    else:
        x_ref, mod_ref, g_ref, w1_ref, w3_ref, w2_ref, o_ref, h_sc, acc_sc = refs
    j = pl.program_id(1)
    tm = x_ref.shape[0]

    @pl.when(j == 0)
    def _():
        h = _modulate(x_ref[...], g_ref[...], mod_ref, 3, 4)
        h_hi = h.astype(BF16)
        h_sc[...] = h_hi
        acc_sc[...] = jnp.zeros(acc_sc.shape, F32)
        if routed:
            h_lo = (h - h_hi.astype(F32)).astype(BF16)
            logits = _mm(h_hi, rwh_ref[...]) + _mm(h_lo, rwh_ref[...]) + _mm(h_hi, rwl_ref[...]) + rb_ref[...]
            lane = lax.broadcasted_iota(jnp.int32, logits.shape, 1)
            lg = jnp.where(lane < N_EXPERTS, logits, -jnp.inf)
            m1 = jnp.max(lg, axis=-1, keepdims=True)
            i1 = jnp.min(jnp.where(lg == m1, lane, LANE), axis=-1, keepdims=True)
            lg2 = jnp.where(lane == i1, -jnp.inf, lg)
            m2 = jnp.max(lg2, axis=-1, keepdims=True)
            i2 = jnp.min(jnp.where(lg2 == m2, lane, LANE), axis=-1, keepdims=True)
            e2 = jnp.exp(m2 - m1)
            inv = 1.0 / (1.0 + e2)
            comb_sc[...] = jnp.where(lane == i1, inv, 0.0) + jnp.where(lane == i2, e2 * inv, 0.0)
            sel = jnp.where(lane == i1, 1.0, 0.0) + jnp.where(lane == i2, 1.0, 0.0)
            ti = lax.broadcasted_iota(jnp.int32, (tm, tm), 0)
            si = lax.broadcasted_iota(jnp.int32, (tm, tm), 1)
            before = _mm((si < ti).astype(BF16), sel.astype(BF16))
            rank = jnp.where(sel > 0.0, before, -1.0)
            rank_sc[...] = rank
            cnt_sc[...] = jnp.broadcast_to(before[tm - 1:tm, :] + sel[tm - 1:tm, :], cnt_sc.shape)
            for k in range(tm // LANE):
                rank_t_sc[:, k * LANE:(k + 1) * LANE] = rank[k * LANE:(k + 1) * LANE, :].T

    if routed:
        lane = lax.broadcasted_iota(jnp.int32, comb_sc.shape, 1)
        lane8 = lax.broadcasted_iota(jnp.int32, cnt_sc.shape, 1)
        n = jnp.sum(jnp.where(lane8 == j, cnt_sc[...], 0.0)[0:1, :]).astype(jnp.int32)
        shift = int(math.log2(MOE_BLK))
        nfull = lax.shift_right_logical(n, shift)
        rem = n - lax.shift_left(nfull, shift)
        gate = jnp.sum(jnp.where(lane == j, comb_sc[...], 0.0), axis=-1, keepdims=True)
        rank_col = jnp.sum(jnp.where(lane == j, rank_sc[...], 0.0), axis=-1, keepdims=True)
        rank_row = rank_t_sc[pl.ds(j, 1), :]

        def run(r0, rows):
            r0 = r0.astype(F32)
            r_iota = lax.broadcasted_iota(jnp.int32, (rows, tm), 0).astype(F32)
            c_iota = lax.broadcasted_iota(jnp.int32, (tm, rows), 1).astype(F32)
            pick = jnp.where(rank_row - r0 == r_iota, 1.0, 0.0).astype(BF16)
            hb = _mm(pick, h_sc[...]).astype(BF16)
            yb = _swiglu(hb, w1_ref.at[0, 0], w3_ref.at[0, 0], w2_ref.at[0, 0]).astype(BF16)
            place = jnp.where(rank_col - r0 == c_iota, 1.0, 0.0).astype(BF16)
            acc_sc[...] += gate * _mm(place, yb)

        def block(b, carry):
            run(b * MOE_BLK, MOE_BLK)
            return carry
        lax.fori_loop(0, nfull + (rem > MOE_TAIL).astype(jnp.int32), block, 0)

        @pl.when((rem > 0) & (rem <= MOE_TAIL))
        def _():
            run(nfull * MOE_BLK, MOE_TAIL)
    else:
        acc_sc[...] += _swiglu(h_sc[...], w1_ref.at[0], w3_ref.at[0], w2_ref.at[0])

    @pl.when(j == nj - 1)
    def _():
        y = x_ref[...] + mod_ref[0, 5:6, :] * acc_sc[...]
        if o2_ref is None:
            o_ref[...] = y
        else:
            first = pl.program_id(0) < TP // tm

            @pl.when(first)
            def _():
                o_ref[...] = y

            @pl.when(jnp.logical_not(first))
            def _():
                o2_ref[...] = y


def _ffn(x, mod, g, w1, w3, w2, li, router=None, tm=1024, split=False):
    t, d = x.shape
    routed = router is not None
    tf = FF_EXPERT
    nj = w1.shape[1] if routed else w1.shape[2] // tf
    ins = [x, mod, g.reshape(1, d)]
    in_specs = [pl.BlockSpec((tm, d), lambda i, j: (i, 0)),
                pl.BlockSpec((1, 8, d), lambda i, j: (_cond_idx(i, tm), 0, 0)),
                pl.BlockSpec((1, d), lambda i, j: (0, 0))]
    scratch = [pltpu.VMEM((tm, d), BF16), pltpu.VMEM((tm, d), F32)]
    if routed:
        ins += list(router)
        in_specs += [pl.BlockSpec((d, LANE), lambda i, j: (0, 0))] * 2 + [pl.BlockSpec((1, LANE), lambda i, j: (0, 0))]
        in_specs += ([pl.BlockSpec((1, 1, d, tf), lambda i, j: (li, j, 0, 0))] * 2
                     + [pl.BlockSpec((1, 1, tf, d), lambda i, j: (li, j, 0, 0))])
        scratch += [pltpu.VMEM((tm, LANE), F32), pltpu.VMEM((tm, LANE), F32), pltpu.VMEM((LANE, tm), F32),
                    pltpu.VMEM((8, LANE), F32)]
    else:
        in_specs += ([pl.BlockSpec((1, d, tf), lambda i, j: (li, 0, j))] * 2
                     + [pl.BlockSpec((1, tf, d), lambda i, j: (li, j, 0))])
    ins += [w1, w3, w2]
    if split:
        assert routed
        n0 = TP // tm
        out_specs = [pl.BlockSpec((tm, d), lambda i, j: (jnp.minimum(i, n0 - 1), 0)),
                     pl.BlockSpec((tm, d), lambda i, j: (jnp.maximum(i - n0, 0), 0))]
        out_shape = [jax.ShapeDtypeStruct((TP, d), F32), jax.ShapeDtypeStruct((TS, d), F32)]
    else:
        out_specs = pl.BlockSpec((tm, d), lambda i, j: (i, 0))
        out_shape = jax.ShapeDtypeStruct((t, d), F32)
    return pl.pallas_call(
        functools.partial(_ffn_kernel, routed=routed, nj=nj, split=split),
        grid=(t // tm, nj),
        in_specs=in_specs,
        out_specs=out_specs,
        out_shape=out_shape,
        scratch_shapes=scratch,
        compiler_params=_cp(("arbitrary" if split else "parallel", "arbitrary")),
        name=("moe_split" if split else "moe") if routed else "ffn",
    )(*ins)


CONV_ROWS = 4096


def _conv_kernel(x_ref, w_ref, b_ref, o_ref):
    i = pl.program_id(0)
    seq = jnp.where(i < TP // CONV_ROWS, SEQ, DEC_SEQ)
    x = x_ref[...]
    pos = lax.broadcasted_iota(jnp.int32, x.shape, 0) & (seq - 1)
    prev = jnp.where(pos == 0, 0.0, pltpu.roll(x, 1, 0))
    nxt = jnp.where(pos == seq - 1, 0.0, pltpu.roll(x, CONV_ROWS - 1, 0))
    y = prev * w_ref[0:1, :] + x * w_ref[1:2, :] + nxt * w_ref[2:3, :] + b_ref[...]
    o_ref[...] = _silu(y)


def _conv(proj, w, b):
    tc = 256
    t = proj.shape[0]
    return pl.pallas_call(
        _conv_kernel,
        grid=(t // CONV_ROWS, C_CONV_DIM // tc),
        in_specs=[pl.BlockSpec((CONV_ROWS, tc), lambda i, j: (i, j + OD_XBC // tc)),
                  pl.BlockSpec((3, tc), lambda i, j: (0, j)),
                  pl.BlockSpec((1, tc), lambda i, j: (0, j))],
        out_specs=pl.BlockSpec((CONV_ROWS, tc), lambda i, j: (i, j)),
        out_shape=jax.ShapeDtypeStruct((t, C_CONV_DIM), F32),
        compiler_params=_cp(("parallel", "parallel")),
        name="conv",
    )(proj, w, b.reshape(1, C_CONV_DIM))


Q = SSD_CHUNK
NCH = T_ALL // Q
NCH_P = TP // Q
NC_P = SEQ // Q
NC_S = DEC_SEQ // Q
SSD_PAR = 2


def _chunk_seq(s):
    is_p = s < NCH_P
    nc = jnp.where(is_p, NC_P, NC_S)
    seq = jnp.where(is_p, s // NC_P, BATCH + (s - NCH_P) // NC_S)
    c = jnp.where(is_p, s % NC_P, (s - NCH_P) % NC_S)
    return seq, c, nc, s - c


def _chunk_mirror(s):
    _, c, nc, start = _chunk_seq(s)
    return start + nc - 1 - c


def _ssd_dt(dt_raw, dt_bias, a_row):
    z = dt_raw + dt_bias
    dt = jnp.maximum(z, 0.0) + jnp.log(1.0 + jnp.exp(-jnp.abs(z)))
    la = dt * a_row
    r = lax.broadcasted_iota(jnp.int32, (Q, Q), 0)
    cc = lax.broadcasted_iota(jnp.int32, (Q, Q), 1)
    tri = (cc <= r).astype(BF16)
    hi, lo = _split_bf16(la)
    return dt, la, _mm(tri, hi) + _mm(tri, lo)


def _expand(v, e32_ref):
    hi, lo = _split_bf16(v)
    return _mm(hi, e32_ref[...]) + _mm(lo, e32_ref[...])


def _ssd1_kernel(xf_ref, xb_ref, df_ref, db_ref, dtb_ref, a_ref, e32_ref, h0_ref, hf_o, hb_o, fin_o, hf_sc, hb_sc):
    s = SSD_PAR * pl.program_id(0)
    _, c, nc, _ = _chunk_seq(s)

    @pl.when(c == 0)
    def _():
        hf_sc[...] = h0_ref[0, 0]
        hb_sc[...] = h0_ref[0, 1]

    lane = lax.broadcasted_iota(jnp.int32, (Q, LANE), 1)
    is_f = lane < C_HEADS
    is_f8 = lax.broadcasted_iota(jnp.int32, (8, LANE), 1) < C_HEADS
    for k in range(SSD_PAR):
        rf = slice(k * Q, (k + 1) * Q)
        rb = slice((SSD_PAR - 1 - k) * Q, (SSD_PAR - k) * Q)
        hf_o[k] = hf_sc[...].astype(BF16)
        hb_o[SSD_PAR - 1 - k] = hb_sc[...].astype(BF16)
        dt_f, _, ac_f = _ssd_dt(df_ref[rf, :], dtb_ref[...], a_ref[...])
        tot_f = ac_f[Q - 1:Q, :]
        dt_b, la_b, ac_b = _ssd_dt(db_ref[rb, :], dtb_ref[...], a_ref[...])
        tot_b = ac_b[Q - 1:Q, :]
        w = jnp.where(is_f, jnp.exp(tot_f - ac_f) * dt_f, jnp.exp(ac_b - la_b) * dt_b)
        dec = jnp.where(is_f8, jnp.exp(jnp.broadcast_to(tot_f, (8, LANE))),
                        jnp.exp(jnp.broadcast_to(tot_b, (8, LANE))))
        w_x = _expand(w, e32_ref)
        dec_x = _expand(dec, e32_ref)[0:1, :]
        for d, (x_ref, r, st) in enumerate(((xf_ref, rf, hf_sc), (xb_ref, rb, hb_sc))):
            xw = (x_ref[r, 0:C_INNER] * w_x[:, d * C_INNER:(d + 1) * C_INNER]).astype(BF16)
            for g in range(C_GROUPS):
                bm_t = x_ref[r, C_INNER + g * C_STATE:C_INNER + (g + 1) * C_STATE].T.astype(BF16)
                cs = slice(g * C_HPG * C_HD, (g + 1) * C_HPG * C_HD)
                dcs = slice(d * C_INNER + g * C_HPG * C_HD, d * C_INNER + (g + 1) * C_HPG * C_HD)
                st[:, cs] = st[:, cs] * dec_x[:, dcs] + _mm(bm_t, xw[:, cs])

    @pl.when((c == nc - SSD_PAR) & (s < NCH_P))
    def _():
        for d, st in enumerate((hf_sc, hb_sc)):
            for k in range(C_INNER // LANE):
                fin_o[0, d, k * LANE:(k + 1) * LANE, :] = st[:, k * LANE:(k + 1) * LANE].T


def _ssd1(xa, proj, dtb, a_row, e32, h0):
    dcol = OD_DT // LANE
    rows = SSD_PAR * Q
    first = lambda i: SSD_PAR * i
    mirror = lambda i: _chunk_mirror(SSD_PAR * i + SSD_PAR - 1) // SSD_PAR
    h0_idx = lambda i: (jnp.where(first(i) < NCH_P, 0, 1 + (first(i) - NCH_P) // NC_S), 0, 0, 0)
    return pl.pallas_call(
        _ssd1_kernel,
        grid=(NCH // SSD_PAR,),
        in_specs=[pl.BlockSpec((rows, C_CONV_DIM), lambda i: (i, 0)),
                  pl.BlockSpec((rows, C_CONV_DIM), lambda i: (mirror(i), 0)),
                  pl.BlockSpec((rows, LANE), lambda i: (i, dcol)),
                  pl.BlockSpec((rows, LANE), lambda i: (mirror(i), dcol)),
                  pl.BlockSpec((1, LANE), lambda i: (0, 0)),
                  pl.BlockSpec((1, LANE), lambda i: (0, 0)),
                  pl.BlockSpec((LANE, 2 * C_INNER), lambda i: (0, 0)),
                  pl.BlockSpec((1, 2, C_STATE, C_INNER), h0_idx)],
        out_specs=[pl.BlockSpec((SSD_PAR, C_STATE, C_INNER), lambda i: (i, 0, 0)),
                   pl.BlockSpec((SSD_PAR, C_STATE, C_INNER), lambda i: (mirror(i), 0, 0)),
                   pl.BlockSpec((1, 2, C_INNER, C_STATE),
                                lambda i: (jnp.minimum(_chunk_seq(first(i))[0], BATCH - 1), 0, 0, 0))],
        out_shape=[jax.ShapeDtypeStruct((NCH, C_STATE, C_INNER), BF16),
                   jax.ShapeDtypeStruct((NCH, C_STATE, C_INNER), BF16),
                   jax.ShapeDtypeStruct((BATCH, 2, C_INNER, C_STATE), F32)],
        scratch_shapes=[pltpu.VMEM((C_STATE, C_INNER), F32)] * 2,
        compiler_params=_cp(("arbitrary",)),
        name="ssd_states",
    )(xa, xa, proj, proj, dtb, a_row, e32, h0)


def _ssd2_kernel(xa_ref, d_ref, z_ref, hf_ref, hb_ref, dtb_ref, a_ref, e32_ref, dsk_ref, gn_ref, o_ref, y_sc):
    for k in range(SSD_PAR):
        _ssd2_chunk(k, xa_ref, d_ref, z_ref, hf_ref, hb_ref, dtb_ref, a_ref, e32_ref, dsk_ref, gn_ref, o_ref, y_sc)


def _ssd2_chunk(k, xa_ref, d_ref, z_ref, hf_ref, hb_ref, dtb_ref, a_ref, e32_ref, dsk_ref, gn_ref, o_ref, y_sc):
    r = slice(k * Q, (k + 1) * Q)
    dt, la, acum = _ssd_dt(d_ref[r, :], dtb_ref[...], a_ref[...])
    tot = acum[Q - 1:Q, :]
    excl = acum - la
    lane = lax.broadcasted_iota(jnp.int32, (Q, LANE), 1)
    is_f = lane < C_HEADS
    logdt = jnp.log(dt)
    col = jnp.where(is_f, acum, excl)
    row_t = jnp.where(is_f, acum - logdt, excl + logdt).T
    dec_x = _expand(jnp.where(is_f, jnp.exp(acum), jnp.exp(tot - excl)), e32_ref)
    x = xa_ref[r, 0:C_INNER]
    xb16 = x.astype(BF16)
    ti = lax.broadcasted_iota(jnp.int32, (Q, Q), 0)
    si = lax.broadcasted_iota(jnp.int32, (Q, Q), 1)
    low = si <= ti
    upp = si >= ti
    lane_q = lax.broadcasted_iota(jnp.int32, (Q, LANE), 1)
    gw = C_HPG * C_HD
    for g in range(C_GROUPS):
        bm = xa_ref[r, C_INNER + g * C_STATE:C_INNER + (g + 1) * C_STATE].astype(BF16)
        cm = xa_ref[r, C_INNER + (C_GROUPS + g) * C_STATE:C_INNER + (C_GROUPS + g + 1) * C_STATE].astype(BF16)
        gmat = _mm_nt(cm, bm)
        cs = slice(g * gw, (g + 1) * gw)
        ys = (_mm(cm, hf_ref[k, :, cs]) * dec_x[:, cs]
              + _mm(cm, hb_ref[k, :, cs]) * dec_x[:, C_INNER + g * gw:C_INNER + (g + 1) * gw])
        for pr in range(C_HPG // 2):
            xpair = xb16[:, g * gw + pr * LANE:g * gw + (pr + 1) * LANE]
            res = []
            for e in range(2):
                h = g * C_HPG + 2 * pr + e
                hb_ = C_HEADS + h
                m_f = jnp.where(low, jnp.exp(col[:, h:h + 1] - row_t[h:h + 1, :]), 0.0)
                m_b = jnp.where(upp, jnp.exp(row_t[hb_:hb_ + 1, :] - col[:, hb_:hb_ + 1]), 0.0)
                res.append(_mm((gmat * (m_f + m_b)).astype(BF16), xpair))
            y_sc[r, g * gw + pr * LANE:g * gw + (pr + 1) * LANE] = (
                jnp.where(lane_q < C_HD, res[0], res[1]) + ys[:, pr * LANE:(pr + 1) * LANE])
    y = y_sc[r, :] + dsk_ref[...] * x
    y = y * _silu(z_ref[r, :])
    y = y * lax.rsqrt(jnp.mean(y * y, axis=-1, keepdims=True) + EPS) * gn_ref[...]
    o_ref[r, :] = y.astype(BF16)


def _ssd2(xa, proj, hf, hb, dtb, a_row, e32, dsk, gn):
    dcol = OD_DT // LANE
    rows = SSD_PAR * Q
    one = lambda w: pl.BlockSpec((1, w), lambda s: (0, 0))
    return pl.pallas_call(
        _ssd2_kernel,
        grid=(NCH // SSD_PAR,),
        in_specs=[pl.BlockSpec((rows, C_CONV_DIM), lambda s: (s, 0)),
                  pl.BlockSpec((rows, LANE), lambda s: (s, dcol)),
                  pl.BlockSpec((rows, C_INNER), lambda s: (s, 0)),
                  pl.BlockSpec((SSD_PAR, C_STATE, C_INNER), lambda s: (s, 0, 0)),
                  pl.BlockSpec((SSD_PAR, C_STATE, C_INNER), lambda s: (s, 0, 0)),
                  one(LANE), one(LANE),
                  pl.BlockSpec((LANE, 2 * C_INNER), lambda s: (0, 0)),
                  one(C_INNER), one(C_INNER)],
        out_specs=pl.BlockSpec((rows, C_INNER), lambda s: (s, 0)),
        out_shape=jax.ShapeDtypeStruct((T_ALL, C_INNER), BF16),
        scratch_shapes=[pltpu.VMEM((rows, C_INNER), F32)],
        compiler_params=_cp(("parallel",)),
        name="ssd_out",
    )(xa, proj, proj, hf, hb, dtb, a_row, e32, dsk, gn)


S5_UNROLL = 4
S5_MM_STEPS = 128


def _cmul(ar, ai, br, bi):
    return ar * br - ai * bi, ar * bi + ai * br


def _s5_kernel(u_ref, wx_ref, cw_ref, a_ref, dsk_ref, h0_ref, y_ref, st_ref, xs, *, nseg):
    ct = pl.program_id(0)
    quarter = ct % 4
    nslab = S5_STEPS // S5_MM_STEPS
    rows = S5_MM_STEPS * S5_CHAINS

    def fill(k, carry):
        r0 = pl.multiple_of(k * S5_MM_STEPS, S5_MM_STEPS)
        ub = u_ref[pl.ds(r0, S5_MM_STEPS)].reshape(rows, LANE).astype(BF16)
        xs[pl.ds(r0, S5_MM_STEPS)] = _mm(ub, wx_ref[0]).reshape(S5_MM_STEPS, S5_CHAINS, 4 * LANE)
        return carry
    lax.fori_loop(0, nslab, fill, 0)

    shape = (S5_CHAINS, LANE)
    a = a_ref[0]
    afr, afi, abr, abi = (jnp.broadcast_to(a[k:k + 1, :], shape) for k in range(4))

    def scan(init, store):
        def body(i, carry):
            hfr, hfi, hbr, hbi = carry
            j = S5_STEPS - 1 - i
            pr, pi = _cmul(afr, afi, hfr, hfi)
            nfr = pr + xs[i, :, 0:LANE]
            nfi = pi + xs[i, :, LANE:2 * LANE]
            pr, pi = _cmul(abr, abi, hbr, hbi)
            nbr = pr + xs[j, :, 2 * LANE:3 * LANE]
            nbi = pi + xs[j, :, 3 * LANE:4 * LANE]
            if store:
                xs[i, :, 0:LANE] = nfr
                xs[i, :, LANE:2 * LANE] = nfi
                xs[j, :, 2 * LANE:3 * LANE] = nbr
                xs[j, :, 3 * LANE:4 * LANE] = nbi
            return nfr, nfi, nbr, nbi
        return lax.fori_loop(0, S5_STEPS, body, init, unroll=S5_UNROLL)

    zero = jnp.zeros(shape, F32)
    if nseg == 1:
        init = (zero, zero, zero, zero)
    else:
        efr, efi, ebr, ebi = scan((zero, zero, zero, zero), store=False)
        pfr, pfi, pbr, pbi = afr, afi, abr, abi
        for _ in range(int(math.log2(S5_STEPS))):
            pfr, pfi = _cmul(pfr, pfi, pfr, pfi)
            pbr, pbi = _cmul(pbr, pbi, pbr, pbi)
        row = lax.broadcasted_iota(jnp.int32, shape, 0) % nseg
        first = row == 0
        last = row == nseg - 1
        h0 = h0_ref[...]
        cfr, cfi, cbr, cbi = zero, zero, zero, zero
        dn = lambda v: pltpu.roll(v, 1, 0)
        up = lambda v: pltpu.roll(v, S5_CHAINS - 1, 0)
        for _ in range(nseg):
            pr, pi = _cmul(pfr, pfi, dn(cfr), dn(cfi))
            cfr = jnp.where(first, h0[0], pr + dn(efr))
            cfi = jnp.where(first, h0[1], pi + dn(efi))
            pr, pi = _cmul(pbr, pbi, up(cbr), up(cbi))
            cbr = jnp.where(last, h0[2], pr + up(ebr))
            cbi = jnp.where(last, h0[3], pi + up(ebi))
        init = (cfr, cfi, cbr, cbi)
    fin = scan(init, store=True)
    for k in range(4):
        st_ref[k] = fin[k]

    def emit(k, carry):
        r0 = pl.multiple_of(k * S5_MM_STEPS, S5_MM_STEPS)
        blk = xs[pl.ds(r0, S5_MM_STEPS)]
        s_re = blk[:, :, 0:LANE] + blk[:, :, 2 * LANE:3 * LANE]
        s_im = blk[:, :, LANE:2 * LANE] + blk[:, :, 3 * LANE:4 * LANE]
        s2 = jnp.concatenate([s_re, s_im], axis=-1).reshape(rows, 2 * LANE).astype(BF16)
        yv = _mm(s2, cw_ref[0]).reshape(S5_MM_STEPS, S5_CHAINS, LANE)

        @pl.when(quarter == 0)
        def _():
            y_ref[pl.ds(r0, S5_MM_STEPS)] = yv + dsk_ref[...] * u_ref[pl.ds(r0, S5_MM_STEPS)]

        @pl.when(quarter != 0)
        def _():
            y_ref[pl.ds(r0, S5_MM_STEPS)] += yv
        return carry
    lax.fori_loop(0, nslab, emit, 0)


def _s5(u_perm, sw, h0, nseg):
    nct = D_GROUPS // 2
    return pl.pallas_call(
        functools.partial(_s5_kernel, nseg=nseg),
        grid=(nct,),
        in_specs=[pl.BlockSpec((S5_STEPS, S5_CHAINS, LANE), lambda ct: (0, 0, ct // 4)),
                  pl.BlockSpec((1, LANE, 4 * LANE), lambda ct: (ct, 0, 0)),
                  pl.BlockSpec((1, 2 * LANE, LANE), lambda ct: (ct, 0, 0)),
                  pl.BlockSpec((1, 8, LANE), lambda ct: (ct, 0, 0)),
                  pl.BlockSpec((1, LANE), lambda ct: (0, ct // 4)),
                  pl.BlockSpec((4, S5_CHAINS, LANE), lambda ct: (0, 0, ct))],
        out_specs=[pl.BlockSpec((S5_STEPS, S5_CHAINS, LANE), lambda ct: (0, 0, ct // 4)),
                   pl.BlockSpec((4, S5_CHAINS, LANE), lambda ct: (0, 0, ct))],
        out_shape=[jax.ShapeDtypeStruct((S5_STEPS, S5_CHAINS, D_WIDTH), F32),
                   jax.ShapeDtypeStruct((4, S5_CHAINS, nct * LANE), F32)],
        scratch_shapes=[pltpu.VMEM((S5_STEPS, S5_CHAINS, 4 * LANE), F32)],
        compiler_params=_cp(("arbitrary",)),
        name="s5_seg" if nseg > 1 else "s5",
    )(u_perm, sw["wx"], sw["cw"], sw["a"], sw["dsk"], h0)


def _pad_last(a, n):
    return jnp.pad(a, [(0, 0)] * (a.ndim - 1) + [(0, n - a.shape[-1])])


def _even_weights(w_in, q_lora_g, w_uq, kv_lora_g, w_ukv, q_norm_g, k_norm_g, bq_g, bk_g):
    d = w_in.shape[0]
    s0, s1, s2, s3, s4 = 384, 640, 672, 1184, 1312
    kr = jnp.pad(w_in[:, s1:s2], ((0, 0), (A_NOPE, LANE - A_QK)))
    heads = lambda w, n: _pad_last(w.reshape(d, n, B_HD), LANE).reshape(d, n * LANE)
    w_perm = jnp.concatenate([w_in[:, :s0], w_in[:, s0:s1], kr, heads(w_in[:, s2:s3], B_HEADS),
                              heads(w_in[:, s3:s4], B_KV_HEADS), heads(w_in[:, s4:], B_KV_HEADS)], axis=1)
    row = lambda g: _pad_last(g.reshape(1, -1), LANE)
    return {
        "w_in": w_perm.astype(BF16),
        "gql": q_lora_g.reshape(1, -1),
        "wuq": _pad_last(w_uq, LANE).reshape(A_Q_LORA, A_HEADS * LANE).astype(BF16),
        "gqn": row(q_norm_g),
        "gkvl": kv_lora_g.reshape(1, -1),
        "wk": _pad_last(w_ukv[:, :, :A_NOPE], LANE).reshape(A_KV_LORA, A_HEADS * LANE).astype(BF16),
        "wv": _pad_last(w_ukv[:, :, A_NOPE:], LANE).reshape(A_KV_LORA, A_HEADS * LANE).astype(BF16),
        "gkn": row(k_norm_g),
        "gbq": row(bq_g),
        "gbk": row(bk_g),
    }


def _rope_tables():
    rows = DEC_SEQ // GRID_W

    def angles(dim):
        half = dim // 2
        inv = ROPE_THETA ** (-jnp.arange(0, half, 2, dtype=F32) / half)
        r = jnp.repeat(jnp.arange(rows, dtype=F32), GRID_W)
        col = jnp.tile(jnp.arange(GRID_W, dtype=F32), rows)
        return jnp.concatenate([r[:, None] * inv, col[:, None] * inv], axis=-1)

    def lanes(dim, start):
        ang = angles(dim)
        cos = jnp.repeat(jnp.cos(ang), 2, axis=-1)
        sin = jnp.repeat(jnp.sin(ang), 2, axis=-1) * jnp.tile(jnp.array([-1.0, 1.0], F32), dim // 2)
        pad = ((0, 0), (start, LANE - start - dim))
        return jnp.pad(cos, pad, constant_values=1.0), jnp.pad(sin, pad)

    cos_a, sin_a = lanes(A_ROPE, A_NOPE)
    cos_b, sin_b = lanes(B_HD, 0)
    return cos_a, sin_a, cos_b, sin_b


def _odd_weights(w_in, dt_bias, a_log, c_d_skip):
    w_perm = jnp.concatenate([w_in[:, :C_INNER], w_in[:, C_INNER:C_INNER + C_CONV_DIM],
                              w_in[:, C_INNER + C_CONV_DIM + 2 * C_HEADS:],
                              _pad_last(w_in[:, C_INNER + C_CONV_DIM:C_INNER + C_CONV_DIM + 2 * C_HEADS], LANE)], axis=1)
    lane = jnp.arange(LANE)
    col = jnp.arange(2 * C_INNER)
    e32 = ((lane[:, None] < 2 * C_HEADS) & (lane[:, None] == col[None, :] // C_HD)).astype(BF16)
    return {
        "w_in": w_perm.astype(BF16),
        "dtb": _pad_last(dt_bias.reshape(1, -1), LANE),
        "a_row": _pad_last(-jnp.exp(a_log).reshape(1, -1), LANE),
        "e32": e32,
        "dsk": jnp.repeat(c_d_skip, C_HD).reshape(1, C_INNER),
    }


def _s5_weights(lam_re, lam_im, log_dt, b_re, b_im, c_re, c_im, d_skip):
    nct = D_GROUPS // 2
    dt = jnp.exp(log_dt)[..., None]
    mag = jnp.exp(lam_re * dt)
    ar, ai = mag * jnp.cos(lam_im * dt), mag * jnp.sin(lam_im * dt)
    den = lam_re * lam_re + lam_im * lam_im
    kr = ((ar - 1.0) * lam_re + ai * lam_im) / den
    ki = (ai * lam_re - (ar - 1.0) * lam_im) / den
    xr = kr[..., None] * b_re[None] - ki[..., None] * b_im[None]
    xi = kr[..., None] * b_im[None] + ki[..., None] * b_re[None]
    xc = jnp.stack([xr[0], xi[0], xr[1], xi[1]]).reshape(4, nct, 2, D_STATE, D_GROUP_SIZE)
    quarter = (jnp.arange(nct)[:, None] % 4 == jnp.arange(4)[None, :]).astype(F32)
    eye2 = jnp.eye(2, dtype=F32)
    wx = jnp.einsum("tq,gh,ktgnj->tqgjkhn", quarter, eye2, xc).reshape(nct, LANE, 4 * LANE)
    cc = jnp.stack([c_re, -c_im]).reshape(2, nct, 2, D_GROUP_SIZE, D_STATE)
    cw = jnp.einsum("tq,gh,rtgcn->trhnqgc", quarter, eye2, cc).reshape(nct, 2 * LANE, LANE)
    planes = jnp.stack([ar[0], ai[0], ar[1], ai[1]]).reshape(4, nct, LANE).transpose(1, 0, 2)
    return {"wx": wx.astype(BF16), "cw": cw.astype(BF16), "a": jnp.pad(planes, ((0, 0), (0, 4), (0, 0))),
            "dsk": d_skip.reshape(1, D_WIDTH)}


def _even_layer(x, mod, norm1_g, norm2_g, ew, w_out, ffn_w, caches, rope_tabs):
    c_ckv, c_krope, c_kb, c_vb = caches
    proj = _modmm(x, mod, norm1_g, ew["w_in"])
    qa_p, ka_p, va_p, qb_p, kb_p, vb_p, ckv_c, kr_c, kbn_c, vb_c = _evenprep(proj, 0, TP, ew, None)
    qa_s, ka_s, va_s, qb_s, kb_s, vb_s = _evenprep(proj, TP, TS, ew, rope_tabs)
    n_ctx = DEC_BATCH * PAST_LEN
    kr_blk = jnp.pad(c_krope.reshape(n_ctx, A_ROPE), ((0, 0), (A_NOPE, LANE - A_QK)))
    ka_c, va_c = _ctxkv(c_ckv.reshape(n_ctx, A_KV_LORA), kr_blk, ew)
    heads_b = lambda a: _pad_last(a.transpose(0, 2, 1, 3), LANE).reshape(n_ctx, B_KV_HEADS * LANE)
    ones_col = (jnp.arange(B_KV_HEADS * LANE) % LANE == A_V).astype(F32)
    gqa = B_HEADS // B_KV_HEADS
    oa = jnp.zeros((T_ALL, A_HEADS * A_V), BF16)
    ob = jnp.zeros((T_ALL, B_HEADS * A_V), BF16)
    oa = _attn(qa_p, ka_p, va_p, None, oa, nb=BATCH, lq=SEQ, lk=SEQ, nheads=A_HEADS, hp=8, q_per_kv=1, tq=SEQ)
    ob = _attn(qb_p, kb_p, vb_p, None, ob, nb=BATCH, lq=SEQ, lk=SEQ, nheads=B_HEADS, hp=4, q_per_kv=gqa, tq=SEQ)
    oa = _attn(qa_s, ka_s, va_s, (ka_c, va_c), oa, nb=DEC_BATCH, lq=DEC_SEQ, lk=DEC_SEQ, nheads=A_HEADS, hp=4,
               q_per_kv=1, tq=ATT_TQ)
    ctx_b = (heads_b(c_kb).astype(BF16), (heads_b(c_vb) + ones_col).astype(BF16))
    ob = _attn(qb_s, kb_s, vb_s, ctx_b, ob, nb=DEC_BATCH, lq=DEC_SEQ, lk=DEC_SEQ, nheads=B_HEADS, hp=8,
               q_per_kv=gqa, tq=ATT_TQ)
    x = _outproj(x, mod, oa, ob, w_out[:A_HEADS * A_V], w_out[A_HEADS * A_V:])
    x = _ffn(x, mod, norm2_g, *ffn_w)
    new_cache = (ckv_c.reshape(BATCH, SEQ, A_KV_LORA), kr_c[:, A_NOPE:A_QK].reshape(BATCH, SEQ, A_ROPE),
                 kbn_c, vb_c)
    return x, new_cache


def _odd_layer(x, mod, norm1_g, norm2_g, ow, sw, conv_w, conv_b, c_norm_g, w_glu, w_out, moe_w, states, last):
    st_c, st_re, st_im = states
    proj = _modmm(x, mod, norm1_g, ow["w_in"])
    xa = _conv(proj, conv_w, conv_b)
    h0 = st_c.transpose(0, 1, 4, 2, 3).reshape(DEC_BATCH, 2, C_STATE, C_INNER)
    h0 = jnp.concatenate([jnp.zeros((1,) + h0.shape[1:], F32), h0], axis=0)
    hf, hb, fin = _ssd1(xa, proj, ow["dtb"], ow["a_row"], ow["e32"], h0)
    yc = _ssd2(xa, proj, hf, hb, ow["dtb"], ow["a_row"], ow["e32"], ow["dsk"], c_norm_g.reshape(1, C_INNER))
    u = proj[:, OD_U:OD_U + D_WIDTH]
    u_p = u[:TP].reshape(BATCH, SEQ, D_WIDTH).transpose(1, 0, 2)
    u_s = u[TP:].reshape(DEC_BATCH * S5_SEG, S5_STEPS, D_WIDTH).transpose(1, 0, 2)
    zeros_h0 = jnp.zeros((4, S5_CHAINS, D_GROUPS * D_STATE), F32)
    h0_s = jnp.stack([st_re[:, 0], st_im[:, 0], st_re[:, 1], st_im[:, 1]]).reshape(4, DEC_BATCH, D_GROUPS * D_STATE)
    h0_s = jnp.repeat(h0_s, S5_SEG, axis=1)
    yd_p, fin_d = _s5(u_p, sw, zeros_h0, 1)
    yd_s, _ = _s5(u_s, sw, h0_s, S5_SEG)
    yd = jnp.concatenate([yd_p.transpose(1, 0, 2).reshape(TP, D_WIDTH),
                          yd_s.transpose(1, 0, 2).reshape(TS, D_WIDTH)], axis=0)
    x = _outproj(x, mod, yc, yd, w_out[:C_INNER], w_out[C_INNER:], wg=w_glu)
    x = _ffn(x, mod, norm2_g, *moe_w[:4], router=moe_w[4], split=last)
    new_c = fin.reshape(BATCH, 2, C_HEADS, C_HD, C_STATE)
    fin_d = fin_d.reshape(4, BATCH, D_GROUPS, D_STATE)
    new_re = jnp.stack([fin_d[0], fin_d[2]], axis=1)
    new_im = jnp.stack([fin_d[1], fin_d[3]], axis=1)
    return x, (new_c, new_re, new_im)


def kernel(x_prompt, x_sample, cache_a_ckv, cache_a_krope, cache_b_k, cache_b_v, state_c_ssm, state_d_re, state_d_im, c, c_ctx, w_mod, b_mod, norm1_g, norm2_g, ev_w_in, ev_a_q_lora_g, ev_a_w_uq, ev_a_kv_lora_g, ev_a_w_ukv, ev_a_q_norm_g, ev_a_k_norm_g, ev_b_q_norm_g, ev_b_k_norm_g, ev_w_out, ffn_w1, ffn_w3, ffn_w2, od_w_in, od_conv_w, od_conv_b, od_dt_bias, od_a_log, od_c_d_skip, od_c_norm_g, od_lam_re, od_lam_im, od_log_dt, od_b_re, od_b_im, od_c_re, od_c_im, od_d_skip, od_w_glu, od_w_out, moe_router_w, moe_router_b, moe_w1, moe_w3, moe_w2):
    x = (x_prompt.reshape(TP, D_MODEL), x_sample.reshape(TS, D_MODEL))
    cond8 = jnp.zeros((8, D_MODEL), F32).at[0].set(c_ctx).at[1:1 + DEC_BATCH].set(c)
    mods = _adaln(cond8, w_mod, b_mod)[:, :1 + DEC_BATCH].reshape(DEPTH, 1 + DEC_BATCH, 6, D_MODEL)
    mods = jnp.pad(mods, ((0, 0), (0, 0), (0, 2), (0, 0)))
    rope_tabs = _rope_tables()
    ffn_bf = (ffn_w1.astype(BF16), ffn_w3.astype(BF16), ffn_w2.astype(BF16))
    moe_bf = (moe_w1.astype(BF16), moe_w3.astype(BF16), moe_w2.astype(BF16))
    ev_new, od_new = [], []
    for layer in range(DEPTH):
        mod = mods[layer]
        if layer % 2 == 0:
            e = layer // 2
            ew = _even_weights(ev_w_in[e], ev_a_q_lora_g[e], ev_a_w_uq[e], ev_a_kv_lora_g[e], ev_a_w_ukv[e],
                               ev_a_q_norm_g[e], ev_a_k_norm_g[e], ev_b_q_norm_g[e], ev_b_k_norm_g[e])
            ffn_w = ffn_bf + (e,)
            x, new = _even_layer(x, mod, norm1_g[layer], norm2_g[layer], ew, ev_w_out[e].astype(BF16), ffn_w,
                                 (cache_a_ckv[:, e], cache_a_krope[:, e], cache_b_k[:, e], cache_b_v[:, e]),
                                 rope_tabs)
            ev_new.append(new)
        else:
            o = layer // 2
            ow = _odd_weights(od_w_in[o], od_dt_bias[o], od_a_log[o], od_c_d_skip[o])
            sw = _s5_weights(od_lam_re[o], od_lam_im[o], od_log_dt[o], od_b_re[o], od_b_im[o], od_c_re[o],
                             od_c_im[o], od_d_skip[o])
            rw_hi = _pad_last(moe_router_w[o], LANE).astype(BF16)
            rw_lo = (_pad_last(moe_router_w[o], LANE) - rw_hi.astype(F32)).astype(BF16)
            moe_w = moe_bf + (o, (rw_hi, rw_lo, _pad_last(moe_router_b[o].reshape(1, -1), LANE)))
            x, new = _odd_layer(x, mod, norm1_g[layer], norm2_g[layer], ow, sw, od_conv_w[o], od_conv_b[o],
                                od_c_norm_g[o], od_w_glu[o].astype(BF16), od_w_out[o].astype(BF16), moe_w,
                                (state_c_ssm[:, o], state_d_re[:, o], state_d_im[:, o]), last=layer == DEPTH - 1)
            od_new.append(new)
    stack = lambda items, k: jnp.stack([it[k] for it in items], axis=1)
    y_p, y_s = x
    return (y_p.reshape(BATCH, SEQ, D_MODEL), y_s.reshape(DEC_BATCH, DEC_SEQ, D_MODEL),
            stack(ev_new, 0), stack(ev_new, 1), stack(ev_new, 2), stack(ev_new, 3),
            stack(od_new, 0), stack(od_new, 1), stack(od_new, 2))
```

```python
import functools
import math

import jax
import jax.numpy as jnp
from jax import lax
from jax.experimental import pallas as pl
from jax.experimental.pallas import tpu as pltpu

F32 = jnp.float32
BF16 = jnp.bfloat16

D_MODEL = 1024
BATCH = 32
SEQ = 256
DEPTH = 4
DEC_BATCH = 2
DEC_SEQ = 4096
PAST_LEN = 256
GRID_W = 64
ROPE_THETA = 10000.0
EPS = 1e-6
A_HEADS = 8
A_NOPE = 64
A_ROPE = 32
A_QK = A_NOPE + A_ROPE
A_V = 64
A_Q_LORA = 384
A_KV_LORA = 256
B_HEADS = 8
B_KV_HEADS = 2
B_HD = 64
C_HEADS = 16
C_HD = 64
C_GROUPS = 2
C_HPG = C_HEADS // C_GROUPS
C_STATE = 128
C_INNER = C_HEADS * C_HD
C_CONV_DIM = C_INNER + 2 * C_GROUPS * C_STATE
SSD_CHUNK = 128
D_GROUP_SIZE = 16
D_GROUPS = 32
D_WIDTH = D_GROUPS * D_GROUP_SIZE
D_STATE = 64
FF_DENSE = 2816
N_EXPERTS = 8
FF_EXPERT = 1408

TP = BATCH * SEQ
TS = DEC_BATCH * DEC_SEQ
T_ALL = TP + TS
LANE = 128
LOG2E = 1.4426950408889634
VMEM_LIMIT = 56 * 1024 * 1024

EV_CQ = 0
EV_CKV = 384
EV_KR = 640
EV_QB = 768
EV_KB = EV_QB + B_HEADS * LANE
EV_VB = EV_KB + B_KV_HEADS * LANE
EV_N = EV_VB + B_KV_HEADS * LANE
OD_Z = 0
OD_XBC = C_INNER
OD_U = OD_XBC + C_CONV_DIM
OD_DT = OD_U + D_WIDTH
OD_N = OD_DT + LANE
S5_STEPS = 256
S5_CHAINS = 32
S5_SEG = DEC_SEQ // S5_STEPS
ATT_TQ = 256
ATT_CHUNKS = 2


def _cp(sem):
    return pltpu.CompilerParams(dimension_semantics=sem, vmem_limit_bytes=VMEM_LIMIT)


def _cond_idx(i, tm):
    t0 = i * tm
    return jnp.where(t0 < TP, 0, 1 + (t0 - TP) // DEC_SEQ)


def _mm(a, b):
    return jnp.dot(a, b, preferred_element_type=F32)


def _mm_nt(a, b):
    return lax.dot_general(a, b, (((1,), (1,)), ((), ())), preferred_element_type=F32)


def _split_bf16(v):
    hi = v.astype(BF16)
    lo = (v - hi.astype(F32)).astype(BF16)
    return hi, lo


def _silu(v):
    return v * jax.nn.sigmoid(v)


def _rms(v, n):
    return v * lax.rsqrt(jnp.sum(v * v, axis=-1, keepdims=True) * (1.0 / n) + EPS)


def _adaln_kernel(c_ref, w_ref, b_ref, o_ref):
    s = _silu(c_ref[...])
    o_ref[0] = _mm(s.astype(BF16), w_ref[0].astype(BF16)) + b_ref[0]


def _adaln(cond8, w_mod, b_mod):
    depth, d, n6 = w_mod.shape
    tn = 1536
    return pl.pallas_call(
        _adaln_kernel,
        grid=(depth, n6 // tn),
        in_specs=[pl.BlockSpec((8, d), lambda l, j: (0, 0)),
                  pl.BlockSpec((1, d, tn), lambda l, j: (l, 0, j)),
                  pl.BlockSpec((1, 1, tn), lambda l, j: (l, 0, j))],
        out_specs=pl.BlockSpec((1, 8, tn), lambda l, j: (l, 0, j)),
        out_shape=jax.ShapeDtypeStruct((depth, 8, n6), F32),
        compiler_params=_cp(("parallel", "parallel")),
        name="adaln",
    )(cond8, w_mod, b_mod.reshape(depth, 1, n6))


def _modulate(x, g, mod_ref, shift_row, scale_row):
    y = x * lax.rsqrt(jnp.mean(x * x, axis=-1, keepdims=True) + EPS) * g
    return y * (1.0 + mod_ref[0, scale_row:scale_row + 1, :]) + mod_ref[0, shift_row:shift_row + 1, :]


def _token_specs(x, tm):
    if not isinstance(x, tuple):
        return [x], [pl.BlockSpec((tm, x.shape[1]), lambda i, *_: (i, 0))], lambda refs: refs[0][...]
    n0 = x[0].shape[0] // tm
    n1 = x[1].shape[0] // tm
    specs = [pl.BlockSpec((tm, x[0].shape[1]), lambda i, *_: (jnp.minimum(i, n0 - 1), 0)),
             pl.BlockSpec((tm, x[1].shape[1]), lambda i, *_: (jnp.clip(i - n0, 0, n1 - 1), 0))]
    return list(x), specs, lambda refs: jnp.where(pl.program_id(0) < n0, refs[0][...], refs[1][...])


def _modmm_kernel(*refs, read_x, nx):
    mod_ref, g_ref, w_ref, o_ref = refs[nx:]
    h = _modulate(read_x(refs[:nx]), g_ref[...], mod_ref, 0, 1)
    o_ref[...] = _mm(h.astype(BF16), w_ref[...])


def _modmm(x, mod, g, w, tm=512):
    xs, x_specs, read_x = _token_specs(x, tm)
    t = sum(a.shape[0] for a in xs)
    d = xs[0].shape[1]
    n = w.shape[1]
    return pl.pallas_call(
        functools.partial(_modmm_kernel, read_x=read_x, nx=len(xs)),
        grid=(t // tm,),
        in_specs=x_specs + [pl.BlockSpec((1, 8, d), lambda i: (_cond_idx(i, tm), 0, 0)),
                            pl.BlockSpec((1, d), lambda i: (0, 0)),
                            pl.BlockSpec((d, n), lambda i: (0, 0))],
        out_specs=pl.BlockSpec((tm, n), lambda i: (i, 0)),
        out_shape=jax.ShapeDtypeStruct((t, n), F32),
        compiler_params=_cp(("parallel",)),
        name="modmm",
    )(*xs, mod, g.reshape(1, d), w)


def _swap_pairs(v):
    lane = lax.broadcasted_iota(jnp.int32, v.shape, 1)
    nxt = pltpu.roll(v, LANE - 1, 1)
    prv = pltpu.roll(v, 1, 1)
    return jnp.where((lane & 1) == 0, nxt, prv)


def _rope(v, cos, sin):
    return v * cos + _swap_pairs(v) * sin


def _with_ones(v):
    lane = lax.broadcasted_iota(jnp.int32, v.shape, 1)
    return jnp.where((lane & (LANE - 1)) == A_V, 1.0, v)


def _mla_kv(ckvn, kr_blk, wk_ref, wv_ref, gkn, rope_tabs, ka_o, va_o):
    cb = ckvn.astype(BF16)
    kn = _mm(cb, wk_ref[...])
    va_o[...] = _with_ones(_mm(cb, wv_ref[...])).astype(BF16)
    if rope_tabs is not None:
        cos, sin = rope_tabs
        swapped = _swap_pairs(kr_blk * gkn) * sin
    for h in range(A_HEADS):
        sl = slice(h * LANE, (h + 1) * LANE)
        kh = kn[:, sl] + kr_blk
        r = lax.rsqrt(jnp.sum(kh * kh, axis=-1, keepdims=True) * (1.0 / A_QK) + EPS)
        kh = kh * r * gkn
        if rope_tabs is not None:
            kh = kh * cos + r * swapped
        ka_o[:, sl] = kh.astype(BF16)


def _evenprep_kernel(*refs, rope):
    it = iter(refs)
    proj = next(it)
    if rope:
        cos_a, sin_a, cos_b, sin_b = (next(it)[...] for _ in range(4))
    gql, wuq, gqn, gkvl, wk, wv, gkn, gbq, gbk = (next(it) for _ in range(9))
    qa_o, ka_o, va_o, qb_o, kb_o, vb_o = (next(it) for _ in range(6))
    if not rope:
        ckv_o, kr_o, kbn_o, vbf_o = (next(it) for _ in range(4))

    cqn = _rms(proj[:, EV_CQ:EV_CQ + A_Q_LORA], A_Q_LORA) * gql[...]
    q = _mm(cqn.astype(BF16), wuq[...])
    for h in range(A_HEADS):
        sl = slice(h * LANE, (h + 1) * LANE)
        qh = _rms(q[:, sl], A_QK) * gqn[...]
        if rope:
            qh = _rope(qh, cos_a, sin_a)
        qa_o[:, sl] = (qh * (A_QK ** -0.5 * LOG2E)).astype(BF16)

    ckvn = _rms(proj[:, EV_CKV:EV_CKV + A_KV_LORA], A_KV_LORA) * gkvl[...]
    kr_blk = proj[:, EV_KR:EV_KR + LANE]
    _mla_kv(ckvn, kr_blk, wk, wv, gkn[...], (cos_a, sin_a) if rope else None, ka_o, va_o)

    for h in range(B_HEADS):
        sl = slice(h * LANE, (h + 1) * LANE)
        qh = _rms(proj[:, EV_QB + h * LANE:EV_QB + (h + 1) * LANE], B_HD) * gbq[...]
        if rope:
            qh = _rope(qh, cos_b, sin_b)
        qb_o[:, sl] = (qh * (B_HD ** -0.5 * LOG2E)).astype(BF16)
    for j in range(B_KV_HEADS):
        sl = slice(j * LANE, (j + 1) * LANE)
        kh = _rms(proj[:, EV_KB + j * LANE:EV_KB + (j + 1) * LANE], B_HD) * gbk[...]
        if not rope:
            kbn_o[0, j] = kh[:, :B_HD]
        else:
            kh = _rope(kh, cos_b, sin_b)
        kb_o[:, sl] = kh.astype(BF16)
    vb = proj[:, EV_VB:EV_VB + B_KV_HEADS * LANE]
    vb_o[...] = _with_ones(vb).astype(BF16)
    if not rope:
        for j in range(B_KV_HEADS):
            vbf_o[0, j] = vb[:, j * LANE:j * LANE + B_HD]
        ckv_o[...] = ckvn
        kr_o[...] = kr_blk


def _evenprep(proj, row_off, nrows, ew, rope_tabs):
    tm = 256
    rope = rope_tabs is not None
    off = row_off // tm
    full = lambda a: pl.BlockSpec(a.shape, lambda i: (0,) * a.ndim)
    rows = lambda w: pl.BlockSpec((tm, w), lambda i: (i, 0))
    ins = [proj]
    in_specs = [pl.BlockSpec((tm, EV_N), lambda i: (i + off, 0))]
    if rope:
        ins += list(rope_tabs)
        in_specs += [pl.BlockSpec((tm, LANE), lambda i: (i % (DEC_SEQ // tm), 0))] * 4
    ws = [ew["gql"], ew["wuq"], ew["gqn"], ew["gkvl"], ew["wk"], ew["wv"], ew["gkn"], ew["gbq"], ew["gbk"]]
    ins += ws
    in_specs += [full(a) for a in ws]
    hw = A_HEADS * LANE
    kvw = B_KV_HEADS * LANE
    out_shape = [jax.ShapeDtypeStruct((nrows, hw), BF16)] * 4 + [jax.ShapeDtypeStruct((nrows, kvw), BF16)] * 2
    out_specs = [rows(hw)] * 4 + [rows(kvw)] * 2
    if not rope:
        assert tm == SEQ
        cache_b = jax.ShapeDtypeStruct((nrows // SEQ, B_KV_HEADS, SEQ, B_HD), F32)
        cache_b_spec = pl.BlockSpec((1, B_KV_HEADS, SEQ, B_HD), lambda i: (i, 0, 0, 0))
        out_shape += [jax.ShapeDtypeStruct((nrows, A_KV_LORA), F32), jax.ShapeDtypeStruct((nrows, LANE), F32),
                      cache_b, cache_b]
        out_specs += [rows(A_KV_LORA), rows(LANE), cache_b_spec, cache_b_spec]
    return pl.pallas_call(
        functools.partial(_evenprep_kernel, rope=rope),
        grid=(nrows // tm,),
        in_specs=in_specs,
        out_specs=out_specs,
        out_shape=out_shape,
        compiler_params=_cp(("parallel",)),
        name="evenprep_rope" if rope else "evenprep",
    )(*ins)


def _ctxkv_kernel(ckv_ref, kr_ref, wk, wv, gkn, ka_o, va_o):
    _mla_kv(ckv_ref[...], kr_ref[...], wk, wv, gkn[...], None, ka_o, va_o)


def _ctxkv(ckvn, kr_blk, ew):
    n = ckvn.shape[0]
    hw = A_HEADS * LANE
    return pl.pallas_call(
        _ctxkv_kernel,
        out_shape=[jax.ShapeDtypeStruct((n, hw), BF16)] * 2,
        compiler_params=pltpu.CompilerParams(vmem_limit_bytes=VMEM_LIMIT),
        name="ctxkv",
    )(ckvn, kr_blk, ew["wk"], ew["wv"], ew["gkn"])


def _attn_kernel(*refs, hp, q_per_kv, has_ctx):
    if has_ctx:
        q_ref, kn_ref, vn_ref, kc_ref, vc_ref, _, o_ref = refs
    else:
        q_ref, kn_ref, vn_ref, _, o_ref = refs
    outs = []
    for h in range(hp):
        kv = slice(h // q_per_kv * LANE, (h // q_per_kv + 1) * LANE)
        q = q_ref[:, h * LANE:(h + 1) * LANE]
        lk = kn_ref.shape[0]
        ck = lk // ATT_CHUNKS if has_ctx else lk
        parts = [(kn_ref, vn_ref, k0, ck) for k0 in range(0, lk, ck)]
        if has_ctx:
            parts.append((kc_ref, vc_ref, 0, kc_ref.shape[0]))
        m = acc = None
        for k_ref, v_ref, k0, n in parts:
            s = _mm_nt(q, k_ref[k0:k0 + n, kv])
            m_c = jnp.max(s, axis=-1, keepdims=True)
            m_new = m_c if m is None else jnp.maximum(m, m_c)
            p = jnp.exp2(s - m_new)
            a_c = _mm(p.astype(BF16), v_ref[k0:k0 + n, kv])
            acc = a_c if m is None else jnp.exp2(m - m_new) * acc + a_c
            m = m_new
        outs.append(acc * (1.0 / acc[:, A_V:A_V + 1]))
    for p2 in range(hp // 2):
        pair = jnp.concatenate([outs[2 * p2][:, :A_V], outs[2 * p2 + 1][:, :A_V]], axis=-1)
        o_ref[:, p2 * LANE:(p2 + 1) * LANE] = pair.astype(BF16)


def _attn(q, kn, vn, ctx, into, *, nb, lq, lk, nheads, hp, q_per_kv, tq):
    ngroups = nheads // hp
    kvw = hp // q_per_kv * LANE
    nq = lq // tq
    ins = [q, kn, vn]
    in_specs = [pl.BlockSpec((tq, hp * LANE), lambda b, g, i: (b * nq + i, g)),
                pl.BlockSpec((lk, kvw), lambda b, g, i: (b, g)),
                pl.BlockSpec((lk, kvw), lambda b, g, i: (b, g))]
    row0 = 0
    if ctx is not None:
        ins += list(ctx)
        in_specs += [pl.BlockSpec((PAST_LEN, kvw), lambda b, g, i: (b, g))] * 2
        row0 = TP // tq
    ins.append(into)
    in_specs.append(pl.BlockSpec(memory_space=pl.ANY))
    aliases = {len(ins) - 1: 0}
    return pl.pallas_call(
        functools.partial(_attn_kernel, hp=hp, q_per_kv=q_per_kv, has_ctx=ctx is not None),
        grid=(nb, ngroups, nq),
        in_specs=in_specs,
        out_specs=pl.BlockSpec((tq, hp * A_V), lambda b, g, i: (row0 + b * nq + i, g)),
        out_shape=jax.ShapeDtypeStruct((T_ALL, nheads * A_V), BF16),
        input_output_aliases=aliases,
        compiler_params=_cp(("parallel", "parallel", "arbitrary")),
        name="attn_ctx" if ctx is not None else "attn",
    )(*ins)


def _outproj_kernel(*refs, glu, read_x, nx):
    x = read_x(refs[:nx])
    if glu:
        mod_ref, a_ref, b_ref, wa_ref, wb_ref, wg_ref, o_ref = refs[nx:]
        yd = jax.nn.gelu(b_ref[...])
        b = (yd * jax.nn.sigmoid(_mm(yd.astype(BF16), wg_ref[...]))).astype(BF16)
    else:
        mod_ref, a_ref, b_ref, wa_ref, wb_ref, o_ref = refs[nx:]
        b = b_ref[...]
    acc = _mm(a_ref[...], wa_ref[...]) + _mm(b, wb_ref[...])
    o_ref[...] = x + mod_ref[0, 2:3, :] * acc


def _outproj(x, mod, a, b, wa, wb, wg=None, tm=512):
    xs, x_specs, read_x = _token_specs(x, tm)
    t = sum(v.shape[0] for v in xs)
    d = xs[0].shape[1]
    full = lambda v: pl.BlockSpec(v.shape, lambda i: (0,) * v.ndim)
    rows = lambda w: pl.BlockSpec((tm, w), lambda i: (i, 0))
    ins = xs + [mod, a, b, wa, wb]
    in_specs = x_specs + [pl.BlockSpec((1, 8, d), lambda i: (_cond_idx(i, tm), 0, 0)), rows(a.shape[1]),
                          rows(b.shape[1]), full(wa), full(wb)]
    if wg is not None:
        ins.append(wg)
        in_specs.append(full(wg))
    return pl.pallas_call(
        functools.partial(_outproj_kernel, glu=wg is not None, read_x=read_x, nx=len(xs)),
        grid=(t // tm,),
        in_specs=in_specs,
        out_specs=rows(d),
        out_shape=jax.ShapeDtypeStruct((t, d), F32),
        compiler_params=_cp(("parallel",)),
        name="outproj_glu" if wg is not None else "outproj",
    )(*ins)


FF_SPLIT = (0, 768, FF_EXPERT)


MOE_BLK = 256
MOE_TAIL = 128


def _swiglu(h, w1, w3, w2):
    y = None
    for k in range(len(FF_SPLIT) - 1):
        sl = slice(FF_SPLIT[k], FF_SPLIT[k + 1])
        act = (_silu(_mm(h, w1[:, sl])) * _mm(h, w3[:, sl])).astype(BF16)
        part = _mm(act, w2[sl, :])
        y = part if y is None else y + part
    return y


def _ffn_kernel(*refs, routed, nj, split):
    o2_ref = None
    if routed and split:
        (x_ref, mod_ref, g_ref, rwh_ref, rwl_ref, rb_ref, w1_ref, w3_ref, w2_ref, o_ref, o2_ref,
         h_sc, acc_sc, comb_sc, rank_sc, rank_t_sc, cnt_sc) = refs
    elif routed:
        (x_ref, mod_ref, g_ref, rwh_ref, rwl_ref, rb_ref, w1_ref, w3_ref, w2_ref, o_ref,
         h_sc, acc_sc, comb_sc, rank_sc, rank_t_sc, cnt_sc) = refs
    else:
        x_ref, mod_ref, g_ref, w1_ref, w3_ref, w2_ref, o_ref, h_sc, acc_sc = refs
    j = pl.program_id(1)
    tm = x_ref.shape[0]

    @pl.when(j == 0)
    def _():
        h = _modulate(x_ref[...], g_ref[...], mod_ref, 3, 4)
        h_hi = h.astype(BF16)
        h_sc[...] = h_hi
        acc_sc[...] = jnp.zeros(acc_sc.shape, F32)
        if routed:
            h_lo = (h - h_hi.astype(F32)).astype(BF16)
            logits = _mm(h_hi, rwh_ref[...]) + _mm(h_lo, rwh_ref[...]) + _mm(h_hi, rwl_ref[...]) + rb_ref[...]
            lane = lax.broadcasted_iota(jnp.int32, logits.shape, 1)
            lg = jnp.where(lane < N_EXPERTS, logits, -jnp.inf)
            m1 = jnp.max(lg, axis=-1, keepdims=True)
            i1 = jnp.min(jnp.where(lg == m1, lane, LANE), axis=-1, keepdims=True)
            lg2 = jnp.where(lane == i1, -jnp.inf, lg)
            m2 = jnp.max(lg2, axis=-1, keepdims=True)
            i2 = jnp.min(jnp.where(lg2 == m2, lane, LANE), axis=-1, keepdims=True)
            e2 = jnp.exp(m2 - m1)
            inv = 1.0 / (1.0 + e2)
            comb_sc[...] = jnp.where(lane == i1, inv, 0.0) + jnp.where(lane == i2, e2 * inv, 0.0)
            sel = jnp.where(lane == i1, 1.0, 0.0) + jnp.where(lane == i2, 1.0, 0.0)
            ti = lax.broadcasted_iota(jnp.int32, (tm, tm), 0)
            si = lax.broadcasted_iota(jnp.int32, (tm, tm), 1)
            before = _mm((si < ti).astype(BF16), sel.astype(BF16))
            rank = jnp.where(sel > 0.0, before, -1.0)
            rank_sc[...] = rank
            cnt_sc[...] = jnp.broadcast_to(before[tm - 1:tm, :] + sel[tm - 1:tm, :], cnt_sc.shape)
            for k in range(tm // LANE):
                rank_t_sc[:, k * LANE:(k + 1) * LANE] = rank[k * LANE:(k + 1) * LANE, :].T

    if routed:
        lane = lax.broadcasted_iota(jnp.int32, comb_sc.shape, 1)
        lane8 = lax.broadcasted_iota(jnp.int32, cnt_sc.shape, 1)
        n = jnp.sum(jnp.where(lane8 == j, cnt_sc[...], 0.0)[0:1, :]).astype(jnp.int32)
        shift = int(math.log2(MOE_BLK))
        nfull = lax.shift_right_logical(n, shift)
        rem = n - lax.shift_left(nfull, shift)
        gate = jnp.sum(jnp.where(lane == j, comb_sc[...], 0.0), axis=-1, keepdims=True)
        rank_col = jnp.sum(jnp.where(lane == j, rank_sc[...], 0.0), axis=-1, keepdims=True)
        rank_row = rank_t_sc[pl.ds(j, 1), :]

        def run(r0, rows):
            r0 = r0.astype(F32)
            r_iota = lax.broadcasted_iota(jnp.int32, (rows, tm), 0).astype(F32)
            c_iota = lax.broadcasted_iota(jnp.int32, (tm, rows), 1).astype(F32)
            pick = jnp.where(rank_row - r0 == r_iota, 1.0, 0.0).astype(BF16)
            hb = _mm(pick, h_sc[...]).astype(BF16)
            yb = _swiglu(hb, w1_ref.at[0, 0], w3_ref.at[0, 0], w2_ref.at[0, 0]).astype(BF16)
            place = jnp.where(rank_col - r0 == c_iota, 1.0, 0.0).astype(BF16)
            acc_sc[...] += gate * _mm(place, yb)

        def block(b, carry):
            run(b * MOE_BLK, MOE_BLK)
            return carry
        lax.fori_loop(0, nfull + (rem > MOE_TAIL).astype(jnp.int32), block, 0)

        @pl.when((rem > 0) & (rem <= MOE_TAIL))
        def _():
            run(nfull * MOE_BLK, MOE_TAIL)
    else:
        acc_sc[...] += _swiglu(h_sc[...], w1_ref.at[0], w3_ref.at[0], w2_ref.at[0])

    @pl.when(j == nj - 1)
    def _():
        y = x_ref[...] + mod_ref[0, 5:6, :] * acc_sc[...]
        if o2_ref is None:
            o_ref[...] = y
        else:
            first = pl.program_id(0) < TP // tm

            @pl.when(first)
            def _():
                o_ref[...] = y

            @pl.when(jnp.logical_not(first))
            def _():
                o2_ref[...] = y


def _ffn(x, mod, g, w1, w3, w2, li, router=None, tm=1024, split=False):
    t, d = x.shape
    routed = router is not None
    tf = FF_EXPERT
    nj = w1.shape[1] if routed else w1.shape[2] // tf
    ins = [x, mod, g.reshape(1, d)]
    in_specs = [pl.BlockSpec((tm, d), lambda i, j: (i, 0)),
                pl.BlockSpec((1, 8, d), lambda i, j: (_cond_idx(i, tm), 0, 0)),
                pl.BlockSpec((1, d), lambda i, j: (0, 0))]
    scratch = [pltpu.VMEM((tm, d), BF16), pltpu.VMEM((tm, d), F32)]
    if routed:
        ins += list(router)
        in_specs += [pl.BlockSpec((d, LANE), lambda i, j: (0, 0))] * 2 + [pl.BlockSpec((1, LANE), lambda i, j: (0, 0))]
        in_specs += ([pl.BlockSpec((1, 1, d, tf), lambda i, j: (li, j, 0, 0))] * 2
                     + [pl.BlockSpec((1, 1, tf, d), lambda i, j: (li, j, 0, 0))])
        scratch += [pltpu.VMEM((tm, LANE), F32), pltpu.VMEM((tm, LANE), F32), pltpu.VMEM((LANE, tm), F32),
                    pltpu.VMEM((8, LANE), F32)]
    else:
        in_specs += ([pl.BlockSpec((1, d, tf), lambda i, j: (li, 0, j))] * 2
                     + [pl.BlockSpec((1, tf, d), lambda i, j: (li, j, 0))])
    ins += [w1, w3, w2]
    if split:
        assert routed
        n0 = TP // tm
        out_specs = [pl.BlockSpec((tm, d), lambda i, j: (jnp.minimum(i, n0 - 1), 0)),
                     pl.BlockSpec((tm, d), lambda i, j: (jnp.maximum(i - n0, 0), 0))]
        out_shape = [jax.ShapeDtypeStruct((TP, d), F32), jax.ShapeDtypeStruct((TS, d), F32)]
    else:
        out_specs = pl.BlockSpec((tm, d), lambda i, j: (i, 0))
        out_shape = jax.ShapeDtypeStruct((t, d), F32)
    return pl.pallas_call(
        functools.partial(_ffn_kernel, routed=routed, nj=nj, split=split),
        grid=(t // tm, nj),
        in_specs=in_specs,
        out_specs=out_specs,
        out_shape=out_shape,
        scratch_shapes=scratch,
        compiler_params=_cp(("arbitrary" if split else "parallel", "arbitrary")),
        name=("moe_split" if split else "moe") if routed else "ffn",
    )(*ins)


CONV_ROWS = 4096


def _conv_kernel(x_ref, w_ref, b_ref, o_ref):
    i = pl.program_id(0)
    seq = jnp.where(i < TP // CONV_ROWS, SEQ, DEC_SEQ)
    x = x_ref[...]
    pos = lax.broadcasted_iota(jnp.int32, x.shape, 0) & (seq - 1)
    prev = jnp.where(pos == 0, 0.0, pltpu.roll(x, 1, 0))
    nxt = jnp.where(pos == seq - 1, 0.0, pltpu.roll(x, CONV_ROWS - 1, 0))
    y = prev * w_ref[0:1, :] + x * w_ref[1:2, :] + nxt * w_ref[2:3, :] + b_ref[...]
    o_ref[...] = _silu(y)


def _conv(proj, w, b):
    tc = 256
    t = proj.shape[0]
    return pl.pallas_call(
        _conv_kernel,
        grid=(t // CONV_ROWS, C_CONV_DIM // tc),
        in_specs=[pl.BlockSpec((CONV_ROWS, tc), lambda i, j: (i, j + OD_XBC // tc)),
                  pl.BlockSpec((3, tc), lambda i, j: (0, j)),
                  pl.BlockSpec((1, tc), lambda i, j: (0, j))],
        out_specs=pl.BlockSpec((CONV_ROWS, tc), lambda i, j: (i, j)),
        out_shape=jax.ShapeDtypeStruct((t, C_CONV_DIM), F32),
        compiler_params=_cp(("parallel", "parallel")),
        name="conv",
    )(proj, w, b.reshape(1, C_CONV_DIM))


Q = SSD_CHUNK
NCH = T_ALL // Q
NCH_P = TP // Q
NC_P = SEQ // Q
NC_S = DEC_SEQ // Q
SSD_PAR = 2


def _chunk_seq(s):
    is_p = s < NCH_P
    nc = jnp.where(is_p, NC_P, NC_S)
    seq = jnp.where(is_p, s // NC_P, BATCH + (s - NCH_P) // NC_S)
    c = jnp.where(is_p, s % NC_P, (s - NCH_P) % NC_S)
    return seq, c, nc, s - c


def _chunk_mirror(s):
    _, c, nc, start = _chunk_seq(s)
    return start + nc - 1 - c


def _ssd_dt(dt_raw, dt_bias, a_row):
    z = dt_raw + dt_bias
    dt = jnp.maximum(z, 0.0) + jnp.log(1.0 + jnp.exp(-jnp.abs(z)))
    la = dt * a_row
    r = lax.broadcasted_iota(jnp.int32, (Q, Q), 0)
    cc = lax.broadcasted_iota(jnp.int32, (Q, Q), 1)
    tri = (cc <= r).astype(BF16)
    hi, lo = _split_bf16(la)
    return dt, la, _mm(tri, hi) + _mm(tri, lo)


def _expand(v, e32_ref):
    hi, lo = _split_bf16(v)
    return _mm(hi, e32_ref[...]) + _mm(lo, e32_ref[...])


def _ssd1_kernel(xf_ref, xb_ref, df_ref, db_ref, dtb_ref, a_ref, e32_ref, h0_ref, hf_o, hb_o, fin_o, hf_sc, hb_sc):
    s = SSD_PAR * pl.program_id(0)
    _, c, nc, _ = _chunk_seq(s)

    @pl.when(c == 0)
    def _():
        hf_sc[...] = h0_ref[0, 0]
        hb_sc[...] = h0_ref[0, 1]

    lane = lax.broadcasted_iota(jnp.int32, (Q, LANE), 1)
    is_f = lane < C_HEADS
    is_f8 = lax.broadcasted_iota(jnp.int32, (8, LANE), 1) < C_HEADS
    for k in range(SSD_PAR):
        rf = slice(k * Q, (k + 1) * Q)
        rb = slice((SSD_PAR - 1 - k) * Q, (SSD_PAR - k) * Q)
        hf_o[k] = hf_sc[...].astype(BF16)
        hb_o[SSD_PAR - 1 - k] = hb_sc[...].astype(BF16)
        dt_f, _, ac_f = _ssd_dt(df_ref[rf, :], dtb_ref[...], a_ref[...])
        tot_f = ac_f[Q - 1:Q, :]
        dt_b, la_b, ac_b = _ssd_dt(db_ref[rb, :], dtb_ref[...], a_ref[...])
        tot_b = ac_b[Q - 1:Q, :]
        w = jnp.where(is_f, jnp.exp(tot_f - ac_f) * dt_f, jnp.exp(ac_b - la_b) * dt_b)
        dec = jnp.where(is_f8, jnp.exp(jnp.broadcast_to(tot_f, (8, LANE))),
                        jnp.exp(jnp.broadcast_to(tot_b, (8, LANE))))
        w_x = _expand(w, e32_ref)
        dec_x = _expand(dec, e32_ref)[0:1, :]
        for d, (x_ref, r, st) in enumerate(((xf_ref, rf, hf_sc), (xb_ref, rb, hb_sc))):
            xw = (x_ref[r, 0:C_INNER] * w_x[:, d * C_INNER:(d + 1) * C_INNER]).astype(BF16)
            for g in range(C_GROUPS):
                bm_t = x_ref[r, C_INNER + g * C_STATE:C_INNER + (g + 1) * C_STATE].T.astype(BF16)
                cs = slice(g * C_HPG * C_HD, (g + 1) * C_HPG * C_HD)
                dcs = slice(d * C_INNER + g * C_HPG * C_HD, d * C_INNER + (g + 1) * C_HPG * C_HD)
                st[:, cs] = st[:, cs] * dec_x[:, dcs] + _mm(bm_t, xw[:, cs])

    @pl.when((c == nc - SSD_PAR) & (s < NCH_P))
    def _():
        for d, st in enumerate((hf_sc, hb_sc)):
            for k in range(C_INNER // LANE):
                fin_o[0, d, k * LANE:(k + 1) * LANE, :] = st[:, k * LANE:(k + 1) * LANE].T


def _ssd1(xa, proj, dtb, a_row, e32, h0):
    dcol = OD_DT // LANE
    rows = SSD_PAR * Q
    first = lambda i: SSD_PAR * i
    mirror = lambda i: _chunk_mirror(SSD_PAR * i + SSD_PAR - 1) // SSD_PAR
    h0_idx = lambda i: (jnp.where(first(i) < NCH_P, 0, 1 + (first(i) - NCH_P) // NC_S), 0, 0, 0)
    return pl.pallas_call(
        _ssd1_kernel,
        grid=(NCH // SSD_PAR,),
        in_specs=[pl.BlockSpec((rows, C_CONV_DIM), lambda i: (i, 0)),
                  pl.BlockSpec((rows, C_CONV_DIM), lambda i: (mirror(i), 0)),
                  pl.BlockSpec((rows, LANE), lambda i: (i, dcol)),
                  pl.BlockSpec((rows, LANE), lambda i: (mirror(i), dcol)),
                  pl.BlockSpec((1, LANE), lambda i: (0, 0)),
                  pl.BlockSpec((1, LANE), lambda i: (0, 0)),
                  pl.BlockSpec((LANE, 2 * C_INNER), lambda i: (0, 0)),
                  pl.BlockSpec((1, 2, C_STATE, C_INNER), h0_idx)],
        out_specs=[pl.BlockSpec((SSD_PAR, C_STATE, C_INNER), lambda i: (i, 0, 0)),
                   pl.BlockSpec((SSD_PAR, C_STATE, C_INNER), lambda i: (mirror(i), 0, 0)),
                   pl.BlockSpec((1, 2, C_INNER, C_STATE),
                                lambda i: (jnp.minimum(_chunk_seq(first(i))[0], BATCH - 1), 0, 0, 0))],
        out_shape=[jax.ShapeDtypeStruct((NCH, C_STATE, C_INNER), BF16),
                   jax.ShapeDtypeStruct((NCH, C_STATE, C_INNER), BF16),
                   jax.ShapeDtypeStruct((BATCH, 2, C_INNER, C_STATE), F32)],
        scratch_shapes=[pltpu.VMEM((C_STATE, C_INNER), F32)] * 2,
        compiler_params=_cp(("arbitrary",)),
        name="ssd_states",
    )(xa, xa, proj, proj, dtb, a_row, e32, h0)


def _ssd2_kernel(xa_ref, d_ref, z_ref, hf_ref, hb_ref, dtb_ref, a_ref, e32_ref, dsk_ref, gn_ref, o_ref, y_sc):
    for k in range(SSD_PAR):
        _ssd2_chunk(k, xa_ref, d_ref, z_ref, hf_ref, hb_ref, dtb_ref, a_ref, e32_ref, dsk_ref, gn_ref, o_ref, y_sc)


def _ssd2_chunk(k, xa_ref, d_ref, z_ref, hf_ref, hb_ref, dtb_ref, a_ref, e32_ref, dsk_ref, gn_ref, o_ref, y_sc):
    r = slice(k * Q, (k + 1) * Q)
    dt, la, acum = _ssd_dt(d_ref[r, :], dtb_ref[...], a_ref[...])
    tot = acum[Q - 1:Q, :]
    excl = acum - la
    lane = lax.broadcasted_iota(jnp.int32, (Q, LANE), 1)
    is_f = lane < C_HEADS
    logdt = jnp.log(dt)
    col = jnp.where(is_f, acum, excl)
    row_t = jnp.where(is_f, acum - logdt, excl + logdt).T
    dec_x = _expand(jnp.where(is_f, jnp.exp(acum), jnp.exp(tot - excl)), e32_ref)
    x = xa_ref[r, 0:C_INNER]
    xb16 = x.astype(BF16)
    ti = lax.broadcasted_iota(jnp.int32, (Q, Q), 0)
    si = lax.broadcasted_iota(jnp.int32, (Q, Q), 1)
    low = si <= ti
    upp = si >= ti
    lane_q = lax.broadcasted_iota(jnp.int32, (Q, LANE), 1)
    gw = C_HPG * C_HD
    for g in range(C_GROUPS):
        bm = xa_ref[r, C_INNER + g * C_STATE:C_INNER + (g + 1) * C_STATE].astype(BF16)
        cm = xa_ref[r, C_INNER + (C_GROUPS + g) * C_STATE:C_INNER + (C_GROUPS + g + 1) * C_STATE].astype(BF16)
        gmat = _mm_nt(cm, bm)
        cs = slice(g * gw, (g + 1) * gw)
        ys = (_mm(cm, hf_ref[k, :, cs]) * dec_x[:, cs]
              + _mm(cm, hb_ref[k, :, cs]) * dec_x[:, C_INNER + g * gw:C_INNER + (g + 1) * gw])
        for pr in range(C_HPG // 2):
            xpair = xb16[:, g * gw + pr * LANE:g * gw + (pr + 1) * LANE]
            res = []
            for e in range(2):
                h = g * C_HPG + 2 * pr + e
                hb_ = C_HEADS + h
                m_f = jnp.where(low, jnp.exp(col[:, h:h + 1] - row_t[h:h + 1, :]), 0.0)
                m_b = jnp.where(upp, jnp.exp(row_t[hb_:hb_ + 1, :] - col[:, hb_:hb_ + 1]), 0.0)
                res.append(_mm((gmat * (m_f + m_b)).astype(BF16), xpair))
            y_sc[r, g * gw + pr * LANE:g * gw + (pr + 1) * LANE] = (
                jnp.where(lane_q < C_HD, res[0], res[1]) + ys[:, pr * LANE:(pr + 1) * LANE])
    y = y_sc[r, :] + dsk_ref[...] * x
    y = y * _silu(z_ref[r, :])
    y = y * lax.rsqrt(jnp.mean(y * y, axis=-1, keepdims=True) + EPS) * gn_ref[...]
    o_ref[r, :] = y.astype(BF16)


def _ssd2(xa, proj, hf, hb, dtb, a_row, e32, dsk, gn):
    dcol = OD_DT // LANE
    rows = SSD_PAR * Q
    one = lambda w: pl.BlockSpec((1, w), lambda s: (0, 0))
    return pl.pallas_call(
        _ssd2_kernel,
        grid=(NCH // SSD_PAR,),
        in_specs=[pl.BlockSpec((rows, C_CONV_DIM), lambda s: (s, 0)),
                  pl.BlockSpec((rows, LANE), lambda s: (s, dcol)),
                  pl.BlockSpec((rows, C_INNER), lambda s: (s, 0)),
                  pl.BlockSpec((SSD_PAR, C_STATE, C_INNER), lambda s: (s, 0, 0)),
                  pl.BlockSpec((SSD_PAR, C_STATE, C_INNER), lambda s: (s, 0, 0)),
                  one(LANE), one(LANE),
                  pl.BlockSpec((LANE, 2 * C_INNER), lambda s: (0, 0)),
                  one(C_INNER), one(C_INNER)],
        out_specs=pl.BlockSpec((rows, C_INNER), lambda s: (s, 0)),
        out_shape=jax.ShapeDtypeStruct((T_ALL, C_INNER), BF16),
        scratch_shapes=[pltpu.VMEM((rows, C_INNER), F32)],
        compiler_params=_cp(("parallel",)),
        name="ssd_out",
    )(xa, proj, proj, hf, hb, dtb, a_row, e32, dsk, gn)


S5_UNROLL = 8
S5_MM_STEPS = 128


def _cmul(ar, ai, br, bi):
    return ar * br - ai * bi, ar * bi + ai * br


def _s5_kernel(u_ref, wx_ref, cw_ref, a_ref, dsk_ref, h0_ref, y_ref, st_ref, xs, *, nseg):
    ct = pl.program_id(0)
    quarter = ct % 4
    nslab = S5_STEPS // S5_MM_STEPS
    rows = S5_MM_STEPS * S5_CHAINS

    def fill(k, carry):
        r0 = pl.multiple_of(k * S5_MM_STEPS, S5_MM_STEPS)
        ub = u_ref[pl.ds(r0, S5_MM_STEPS)].reshape(rows, LANE).astype(BF16)
        xs[pl.ds(r0, S5_MM_STEPS)] = _mm(ub, wx_ref[0]).reshape(S5_MM_STEPS, S5_CHAINS, 4 * LANE)
        return carry
    lax.fori_loop(0, nslab, fill, 0)

    shape = (S5_CHAINS, LANE)
    a = a_ref[0]
    afr, afi, abr, abi = (jnp.broadcast_to(a[k:k + 1, :], shape) for k in range(4))

    def scan(init, store):
        def body(i, carry):
            hfr, hfi, hbr, hbi = carry
            j = S5_STEPS - 1 - i
            pr, pi = _cmul(afr, afi, hfr, hfi)
            nfr = pr + xs[i, :, 0:LANE]
            nfi = pi + xs[i, :, LANE:2 * LANE]
            pr, pi = _cmul(abr, abi, hbr, hbi)
            nbr = pr + xs[j, :, 2 * LANE:3 * LANE]
            nbi = pi + xs[j, :, 3 * LANE:4 * LANE]
            if store:
                xs[i, :, 0:LANE] = nfr
                xs[i, :, LANE:2 * LANE] = nfi
                xs[j, :, 2 * LANE:3 * LANE] = nbr
                xs[j, :, 3 * LANE:4 * LANE] = nbi
            return nfr, nfi, nbr, nbi
        return lax.fori_loop(0, S5_STEPS, body, init, unroll=S5_UNROLL)

    zero = jnp.zeros(shape, F32)
    if nseg == 1:
        init = (zero, zero, zero, zero)
    else:
        efr, efi, ebr, ebi = scan((zero, zero, zero, zero), store=False)
        pfr, pfi, pbr, pbi = afr, afi, abr, abi
        for _ in range(int(math.log2(S5_STEPS))):
            pfr, pfi = _cmul(pfr, pfi, pfr, pfi)
            pbr, pbi = _cmul(pbr, pbi, pbr, pbi)
        row = lax.broadcasted_iota(jnp.int32, shape, 0) % nseg
        first = row == 0
        last = row == nseg - 1
        h0 = h0_ref[...]
        cfr, cfi, cbr, cbi = zero, zero, zero, zero
        dn = lambda v: pltpu.roll(v, 1, 0)
        up = lambda v: pltpu.roll(v, S5_CHAINS - 1, 0)
        for _ in range(nseg):
            pr, pi = _cmul(pfr, pfi, dn(cfr), dn(cfi))
            cfr = jnp.where(first, h0[0], pr + dn(efr))
            cfi = jnp.where(first, h0[1], pi + dn(efi))
            pr, pi = _cmul(pbr, pbi, up(cbr), up(cbi))
            cbr = jnp.where(last, h0[2], pr + up(ebr))
            cbi = jnp.where(last, h0[3], pi + up(ebi))
        init = (cfr, cfi, cbr, cbi)
    fin = scan(init, store=True)
    for k in range(4):
        st_ref[k] = fin[k]

    def emit(k, carry):
        r0 = pl.multiple_of(k * S5_MM_STEPS, S5_MM_STEPS)
        blk = xs[pl.ds(r0, S5_MM_STEPS)]
        s_re = blk[:, :, 0:LANE] + blk[:, :, 2 * LANE:3 * LANE]
        s_im = blk[:, :, LANE:2 * LANE] + blk[:, :, 3 * LANE:4 * LANE]
        s2 = jnp.concatenate([s_re, s_im], axis=-1).reshape(rows, 2 * LANE).astype(BF16)
        yv = _mm(s2, cw_ref[0]).reshape(S5_MM_STEPS, S5_CHAINS, LANE)

        @pl.when(quarter == 0)
        def _():
            y_ref[pl.ds(r0, S5_MM_STEPS)] = yv + dsk_ref[...] * u_ref[pl.ds(r0, S5_MM_STEPS)]

        @pl.when(quarter != 0)
        def _():
            y_ref[pl.ds(r0, S5_MM_STEPS)] += yv
        return carry
    lax.fori_loop(0, nslab, emit, 0)


def _s5(u_perm, sw, h0, nseg):
    nct = D_GROUPS // 2
    return pl.pallas_call(
        functools.partial(_s5_kernel, nseg=nseg),
        grid=(nct,),
        in_specs=[pl.BlockSpec((S5_STEPS, S5_CHAINS, LANE), lambda ct: (0, 0, ct // 4)),
                  pl.BlockSpec((1, LANE, 4 * LANE), lambda ct: (ct, 0, 0)),
                  pl.BlockSpec((1, 2 * LANE, LANE), lambda ct: (ct, 0, 0)),
                  pl.BlockSpec((1, 8, LANE), lambda ct: (ct, 0, 0)),
                  pl.BlockSpec((1, LANE), lambda ct: (0, ct // 4)),
                  pl.BlockSpec((4, S5_CHAINS, LANE), lambda ct: (0, 0, ct))],
        out_specs=[pl.BlockSpec((S5_STEPS, S5_CHAINS, LANE), lambda ct: (0, 0, ct // 4)),
                   pl.BlockSpec((4, S5_CHAINS, LANE), lambda ct: (0, 0, ct))],
        out_shape=[jax.ShapeDtypeStruct((S5_STEPS, S5_CHAINS, D_WIDTH), F32),
                   jax.ShapeDtypeStruct((4, S5_CHAINS, nct * LANE), F32)],
        scratch_shapes=[pltpu.VMEM((S5_STEPS, S5_CHAINS, 4 * LANE), F32)],
        compiler_params=_cp(("arbitrary",)),
        name="s5_seg" if nseg > 1 else "s5",
    )(u_perm, sw["wx"], sw["cw"], sw["a"], sw["dsk"], h0)


def _pad_last(a, n):
    return jnp.pad(a, [(0, 0)] * (a.ndim - 1) + [(0, n - a.shape[-1])])


def _even_weights(w_in, q_lora_g, w_uq, kv_lora_g, w_ukv, q_norm_g, k_norm_g, bq_g, bk_g):
    d = w_in.shape[0]
    s0, s1, s2, s3, s4 = 384, 640, 672, 1184, 1312
    kr = jnp.pad(w_in[:, s1:s2], ((0, 0), (A_NOPE, LANE - A_QK)))
    heads = lambda w, n: _pad_last(w.reshape(d, n, B_HD), LANE).reshape(d, n * LANE)
    w_perm = jnp.concatenate([w_in[:, :s0], w_in[:, s0:s1], kr, heads(w_in[:, s2:s3], B_HEADS),
                              heads(w_in[:, s3:s4], B_KV_HEADS), heads(w_in[:, s4:], B_KV_HEADS)], axis=1)
    row = lambda g: _pad_last(g.reshape(1, -1), LANE)
    return {
        "w_in": w_perm.astype(BF16),
        "gql": q_lora_g.reshape(1, -1),
        "wuq": _pad_last(w_uq, LANE).reshape(A_Q_LORA, A_HEADS * LANE).astype(BF16),
        "gqn": row(q_norm_g),
        "gkvl": kv_lora_g.reshape(1, -1),
        "wk": _pad_last(w_ukv[:, :, :A_NOPE], LANE).reshape(A_KV_LORA, A_HEADS * LANE).astype(BF16),
        "wv": _pad_last(w_ukv[:, :, A_NOPE:], LANE).reshape(A_KV_LORA, A_HEADS * LANE).astype(BF16),
        "gkn": row(k_norm_g),
        "gbq": row(bq_g),
        "gbk": row(bk_g),
    }


def _rope_tables():
    rows = DEC_SEQ // GRID_W

    def angles(dim):
        half = dim // 2
        inv = ROPE_THETA ** (-jnp.arange(0, half, 2, dtype=F32) / half)
        r = jnp.repeat(jnp.arange(rows, dtype=F32), GRID_W)
        col = jnp.tile(jnp.arange(GRID_W, dtype=F32), rows)
        return jnp.concatenate([r[:, None] * inv, col[:, None] * inv], axis=-1)

    def lanes(dim, start):
        ang = angles(dim)
        cos = jnp.repeat(jnp.cos(ang), 2, axis=-1)
        sin = jnp.repeat(jnp.sin(ang), 2, axis=-1) * jnp.tile(jnp.array([-1.0, 1.0], F32), dim // 2)
        pad = ((0, 0), (start, LANE - start - dim))
        return jnp.pad(cos, pad, constant_values=1.0), jnp.pad(sin, pad)

    cos_a, sin_a = lanes(A_ROPE, A_NOPE)
    cos_b, sin_b = lanes(B_HD, 0)
    return cos_a, sin_a, cos_b, sin_b


def _odd_weights(w_in, dt_bias, a_log, c_d_skip):
    w_perm = jnp.concatenate([w_in[:, :C_INNER], w_in[:, C_INNER:C_INNER + C_CONV_DIM],
                              w_in[:, C_INNER + C_CONV_DIM + 2 * C_HEADS:],
                              _pad_last(w_in[:, C_INNER + C_CONV_DIM:C_INNER + C_CONV_DIM + 2 * C_HEADS], LANE)], axis=1)
    lane = jnp.arange(LANE)
    col = jnp.arange(2 * C_INNER)
    e32 = ((lane[:, None] < 2 * C_HEADS) & (lane[:, None] == col[None, :] // C_HD)).astype(BF16)
    return {
        "w_in": w_perm.astype(BF16),
        "dtb": _pad_last(dt_bias.reshape(1, -1), LANE),
        "a_row": _pad_last(-jnp.exp(a_log).reshape(1, -1), LANE),
        "e32": e32,
        "dsk": jnp.repeat(c_d_skip, C_HD).reshape(1, C_INNER),
    }


def _s5_weights(lam_re, lam_im, log_dt, b_re, b_im, c_re, c_im, d_skip):
    nct = D_GROUPS // 2
    dt = jnp.exp(log_dt)[..., None]
    mag = jnp.exp(lam_re * dt)
    ar, ai = mag * jnp.cos(lam_im * dt), mag * jnp.sin(lam_im * dt)
    den = lam_re * lam_re + lam_im * lam_im
    kr = ((ar - 1.0) * lam_re + ai * lam_im) / den
    ki = (ai * lam_re - (ar - 1.0) * lam_im) / den
    xr = kr[..., None] * b_re[None] - ki[..., None] * b_im[None]
    xi = kr[..., None] * b_im[None] + ki[..., None] * b_re[None]
    xc = jnp.stack([xr[0], xi[0], xr[1], xi[1]]).reshape(4, nct, 2, D_STATE, D_GROUP_SIZE)
    quarter = (jnp.arange(nct)[:, None] % 4 == jnp.arange(4)[None, :]).astype(F32)
    eye2 = jnp.eye(2, dtype=F32)
    wx = jnp.einsum("tq,gh,ktgnj->tqgjkhn", quarter, eye2, xc).reshape(nct, LANE, 4 * LANE)
    cc = jnp.stack([c_re, -c_im]).reshape(2, nct, 2, D_GROUP_SIZE, D_STATE)
    cw = jnp.einsum("tq,gh,rtgcn->trhnqgc", quarter, eye2, cc).reshape(nct, 2 * LANE, LANE)
    planes = jnp.stack([ar[0], ai[0], ar[1], ai[1]]).reshape(4, nct, LANE).transpose(1, 0, 2)
    return {"wx": wx.astype(BF16), "cw": cw.astype(BF16), "a": jnp.pad(planes, ((0, 0), (0, 4), (0, 0))),
            "dsk": d_skip.reshape(1, D_WIDTH)}


def _even_layer(x, mod, norm1_g, norm2_g, ew, w_out, ffn_w, caches, rope_tabs):
    c_ckv, c_krope, c_kb, c_vb = caches
    proj = _modmm(x, mod, norm1_g, ew["w_in"])
    qa_p, ka_p, va_p, qb_p, kb_p, vb_p, ckv_c, kr_c, kbn_c, vb_c = _evenprep(proj, 0, TP, ew, None)
    qa_s, ka_s, va_s, qb_s, kb_s, vb_s = _evenprep(proj, TP, TS, ew, rope_tabs)
    n_ctx = DEC_BATCH * PAST_LEN
    kr_blk = jnp.pad(c_krope.reshape(n_ctx, A_ROPE), ((0, 0), (A_NOPE, LANE - A_QK)))
    ka_c, va_c = _ctxkv(c_ckv.reshape(n_ctx, A_KV_LORA), kr_blk, ew)
    heads_b = lambda a: _pad_last(a.transpose(0, 2, 1, 3), LANE).reshape(n_ctx, B_KV_HEADS * LANE)
    ones_col = (jnp.arange(B_KV_HEADS * LANE) % LANE == A_V).astype(F32)
    gqa = B_HEADS // B_KV_HEADS
    oa = jnp.zeros((T_ALL, A_HEADS * A_V), BF16)
    ob = jnp.zeros((T_ALL, B_HEADS * A_V), BF16)
    oa = _attn(qa_p, ka_p, va_p, None, oa, nb=BATCH, lq=SEQ, lk=SEQ, nheads=A_HEADS, hp=8, q_per_kv=1, tq=SEQ)
    ob = _attn(qb_p, kb_p, vb_p, None, ob, nb=BATCH, lq=SEQ, lk=SEQ, nheads=B_HEADS, hp=4, q_per_kv=gqa, tq=SEQ)
    oa = _attn(qa_s, ka_s, va_s, (ka_c, va_c), oa, nb=DEC_BATCH, lq=DEC_SEQ, lk=DEC_SEQ, nheads=A_HEADS, hp=4,
               q_per_kv=1, tq=ATT_TQ)
    ctx_b = (heads_b(c_kb).astype(BF16), (heads_b(c_vb) + ones_col).astype(BF16))
    ob = _attn(qb_s, kb_s, vb_s, ctx_b, ob, nb=DEC_BATCH, lq=DEC_SEQ, lk=DEC_SEQ, nheads=B_HEADS, hp=8,
               q_per_kv=gqa, tq=ATT_TQ)
    x = _outproj(x, mod, oa, ob, w_out[:A_HEADS * A_V], w_out[A_HEADS * A_V:])
    x = _ffn(x, mod, norm2_g, *ffn_w)
    new_cache = (ckv_c.reshape(BATCH, SEQ, A_KV_LORA), kr_c[:, A_NOPE:A_QK].reshape(BATCH, SEQ, A_ROPE),
                 kbn_c, vb_c)
    return x, new_cache


def _odd_layer(x, mod, norm1_g, norm2_g, ow, sw, conv_w, conv_b, c_norm_g, w_glu, w_out, moe_w, states, last):
    st_c, st_re, st_im = states
    proj = _modmm(x, mod, norm1_g, ow["w_in"])
    xa = _conv(proj, conv_w, conv_b)
    h0 = st_c.transpose(0, 1, 4, 2, 3).reshape(DEC_BATCH, 2, C_STATE, C_INNER)
    h0 = jnp.concatenate([jnp.zeros((1,) + h0.shape[1:], F32), h0], axis=0)
    hf, hb, fin = _ssd1(xa, proj, ow["dtb"], ow["a_row"], ow["e32"], h0)
    yc = _ssd2(xa, proj, hf, hb, ow["dtb"], ow["a_row"], ow["e32"], ow["dsk"], c_norm_g.reshape(1, C_INNER))
    u = proj[:, OD_U:OD_U + D_WIDTH]
    u_p = u[:TP].reshape(BATCH, SEQ, D_WIDTH).transpose(1, 0, 2)
    u_s = u[TP:].reshape(DEC_BATCH * S5_SEG, S5_STEPS, D_WIDTH).transpose(1, 0, 2)
    zeros_h0 = jnp.zeros((4, S5_CHAINS, D_GROUPS * D_STATE), F32)
    h0_s = jnp.stack([st_re[:, 0], st_im[:, 0], st_re[:, 1], st_im[:, 1]]).reshape(4, DEC_BATCH, D_GROUPS * D_STATE)
    h0_s = jnp.repeat(h0_s, S5_SEG, axis=1)
    yd_p, fin_d = _s5(u_p, sw, zeros_h0, 1)
    yd_s, _ = _s5(u_s, sw, h0_s, S5_SEG)
    yd = jnp.concatenate([yd_p.transpose(1, 0, 2).reshape(TP, D_WIDTH),
                          yd_s.transpose(1, 0, 2).reshape(TS, D_WIDTH)], axis=0)
    x = _outproj(x, mod, yc, yd, w_out[:C_INNER], w_out[C_INNER:], wg=w_glu)
    x = _ffn(x, mod, norm2_g, *moe_w[:4], router=moe_w[4], split=last)
    new_c = fin.reshape(BATCH, 2, C_HEADS, C_HD, C_STATE)
    fin_d = fin_d.reshape(4, BATCH, D_GROUPS, D_STATE)
    new_re = jnp.stack([fin_d[0], fin_d[2]], axis=1)
    new_im = jnp.stack([fin_d[1], fin_d[3]], axis=1)
    return x, (new_c, new_re, new_im)


def kernel(x_prompt, x_sample, cache_a_ckv, cache_a_krope, cache_b_k, cache_b_v, state_c_ssm, state_d_re, state_d_im, c, c_ctx, w_mod, b_mod, norm1_g, norm2_g, ev_w_in, ev_a_q_lora_g, ev_a_w_uq, ev_a_kv_lora_g, ev_a_w_ukv, ev_a_q_norm_g, ev_a_k_norm_g, ev_b_q_norm_g, ev_b_k_norm_g, ev_w_out, ffn_w1, ffn_w3, ffn_w2, od_w_in, od_conv_w, od_conv_b, od_dt_bias, od_a_log, od_c_d_skip, od_c_norm_g, od_lam_re, od_lam_im, od_log_dt, od_b_re, od_b_im, od_c_re, od_c_im, od_d_skip, od_w_glu, od_w_out, moe_router_w, moe_router_b, moe_w1, moe_w3, moe_w2):
    x = (x_prompt.reshape(TP, D_MODEL), x_sample.reshape(TS, D_MODEL))
    cond8 = jnp.zeros((8, D_MODEL), F32).at[0].set(c_ctx).at[1:1 + DEC_BATCH].set(c)
    mods = _adaln(cond8, w_mod, b_mod)[:, :1 + DEC_BATCH].reshape(DEPTH, 1 + DEC_BATCH, 6, D_MODEL)
    mods = jnp.pad(mods, ((0, 0), (0, 0), (0, 2), (0, 0)))
    rope_tabs = _rope_tables()
    ffn_bf = (ffn_w1.astype(BF16), ffn_w3.astype(BF16), ffn_w2.astype(BF16))
    moe_bf = (moe_w1.astype(BF16), moe_w3.astype(BF16), moe_w2.astype(BF16))
    ev_new, od_new = [], []
    for layer in range(DEPTH):
        mod = mods[layer]
        if layer % 2 == 0:
            e = layer // 2
            ew = _even_weights(ev_w_in[e], ev_a_q_lora_g[e], ev_a_w_uq[e], ev_a_kv_lora_g[e], ev_a_w_ukv[e],
                               ev_a_q_norm_g[e], ev_a_k_norm_g[e], ev_b_q_norm_g[e], ev_b_k_norm_g[e])
            ffn_w = ffn_bf + (e,)
            x, new = _even_layer(x, mod, norm1_g[layer], norm2_g[layer], ew, ev_w_out[e].astype(BF16), ffn_w,
                                 (cache_a_ckv[:, e], cache_a_krope[:, e], cache_b_k[:, e], cache_b_v[:, e]),
                                 rope_tabs)
            ev_new.append(new)
        else:
            o = layer // 2
            ow = _odd_weights(od_w_in[o], od_dt_bias[o], od_a_log[o], od_c_d_skip[o])
            sw = _s5_weights(od_lam_re[o], od_lam_im[o], od_log_dt[o], od_b_re[o], od_b_im[o], od_c_re[o],
                             od_c_im[o], od_d_skip[o])
            rw_hi = _pad_last(moe_router_w[o], LANE).astype(BF16)
            rw_lo = (_pad_last(moe_router_w[o], LANE) - rw_hi.astype(F32)).astype(BF16)
            moe_w = moe_bf + (o, (rw_hi, rw_lo, _pad_last(moe_router_b[o].reshape(1, -1), LANE)))
            x, new = _odd_layer(x, mod, norm1_g[layer], norm2_g[layer], ow, sw, od_conv_w[o], od_conv_b[o],
                                od_c_norm_g[o], od_w_glu[o].astype(BF16), od_w_out[o].astype(BF16), moe_w,
                                (state_c_ssm[:, o], state_d_re[:, o], state_d_im[:, o]), last=layer == DEPTH - 1)
            od_new.append(new)
    stack = lambda items, k: jnp.stack([it[k] for it in items], axis=1)
    y_p, y_s = x
    return (y_p.reshape(BATCH, SEQ, D_MODEL), y_s.reshape(DEC_BATCH, DEC_SEQ, D_MODEL),
            stack(ev_new, 0), stack(ev_new, 1), stack(ev_new, 2), stack(ev_new, 3),
            stack(od_new, 0), stack(od_new, 1), stack(od_new, 2))
```

```python
import functools
import math

import jax
import jax.numpy as jnp
from jax import lax
from jax.experimental import pallas as pl
from jax.experimental.pallas import tpu as pltpu

F32 = jnp.float32
BF16 = jnp.bfloat16

D_MODEL = 1024
BATCH = 32
SEQ = 256
DEPTH = 4
DEC_BATCH = 2
DEC_SEQ = 4096
PAST_LEN = 256
GRID_W = 64
ROPE_THETA = 10000.0
EPS = 1e-6
A_HEADS = 8
A_NOPE = 64
A_ROPE = 32
A_QK = A_NOPE + A_ROPE
A_V = 64
A_Q_LORA = 384
A_KV_LORA = 256
B_HEADS = 8
B_KV_HEADS = 2
B_HD = 64
C_HEADS = 16
C_HD = 64
C_GROUPS = 2
C_HPG = C_HEADS // C_GROUPS
C_STATE = 128
C_INNER = C_HEADS * C_HD
C_CONV_DIM = C_INNER + 2 * C_GROUPS * C_STATE
SSD_CHUNK = 128
D_GROUP_SIZE = 16
D_GROUPS = 32
D_WIDTH = D_GROUPS * D_GROUP_SIZE
D_STATE = 64
FF_DENSE = 2816
N_EXPERTS = 8
FF_EXPERT = 1408

TP = BATCH * SEQ
TS = DEC_BATCH * DEC_SEQ
T_ALL = TP + TS
LANE = 128
LOG2E = 1.4426950408889634
VMEM_LIMIT = 56 * 1024 * 1024

EV_CQ = 0
EV_CKV = 384
EV_KR = 640
EV_QB = 768
EV_KB = EV_QB + B_HEADS * LANE
EV_VB = EV_KB + B_KV_HEADS * LANE
EV_N = EV_VB + B_KV_HEADS * LANE
OD_Z = 0
OD_XBC = C_INNER
OD_U = OD_XBC + C_CONV_DIM
OD_DT = OD_U + D_WIDTH
OD_N = OD_DT + LANE
S5_STEPS = 256
S5_CHAINS = 32
S5_SEG = DEC_SEQ // S5_STEPS
ATT_TQ = 256
ATT_CHUNKS = 2


def _cp(sem):
    return pltpu.CompilerParams(dimension_semantics=sem, vmem_limit_bytes=VMEM_LIMIT)


def _cond_idx(i, tm):
    t0 = i * tm
    return jnp.where(t0 < TP, 0, 1 + (t0 - TP) // DEC_SEQ)


def _mm(a, b):
    return jnp.dot(a, b, preferred_element_type=F32)


def _mm_nt(a, b):
    return lax.dot_general(a, b, (((1,), (1,)), ((), ())), preferred_element_type=F32)


def _split_bf16(v):
    hi = v.astype(BF16)
    lo = (v - hi.astype(F32)).astype(BF16)
    return hi, lo


def _silu(v):
    return v * jax.nn.sigmoid(v)


def _rms(v, n):
    return v * lax.rsqrt(jnp.sum(v * v, axis=-1, keepdims=True) * (1.0 / n) + EPS)


def _adaln_kernel(c_ref, w_ref, b_ref, o_ref):
    s = _silu(c_ref[...])
    o_ref[0] = _mm(s.astype(BF16), w_ref[0].astype(BF16)) + b_ref[0]


def _adaln(cond8, w_mod, b_mod):
    depth, d, n6 = w_mod.shape
    tn = 1536
    return pl.pallas_call(
        _adaln_kernel,
        grid=(depth, n6 // tn),
        in_specs=[pl.BlockSpec((8, d), lambda l, j: (0, 0)),
                  pl.BlockSpec((1, d, tn), lambda l, j: (l, 0, j)),
                  pl.BlockSpec((1, 1, tn), lambda l, j: (l, 0, j))],
        out_specs=pl.BlockSpec((1, 8, tn), lambda l, j: (l, 0, j)),
        out_shape=jax.ShapeDtypeStruct((depth, 8, n6), F32),
        compiler_params=_cp(("parallel", "parallel")),
        name="adaln",
    )(cond8, w_mod, b_mod.reshape(depth, 1, n6))


def _modulate(x, g, mod_ref, shift_row, scale_row):
    y = x * lax.rsqrt(jnp.mean(x * x, axis=-1, keepdims=True) + EPS) * g
    return y * (1.0 + mod_ref[0, scale_row:scale_row + 1, :]) + mod_ref[0, shift_row:shift_row + 1, :]


def _token_specs(x, tm):
    if not isinstance(x, tuple):
        return [x], [pl.BlockSpec((tm, x.shape[1]), lambda i, *_: (i, 0))], lambda refs: refs[0][...]
    n0 = x[0].shape[0] // tm
    n1 = x[1].shape[0] // tm
    specs = [pl.BlockSpec((tm, x[0].shape[1]), lambda i, *_: (jnp.minimum(i, n0 - 1), 0)),
             pl.BlockSpec((tm, x[1].shape[1]), lambda i, *_: (jnp.clip(i - n0, 0, n1 - 1), 0))]
    return list(x), specs, lambda refs: jnp.where(pl.program_id(0) < n0, refs[0][...], refs[1][...])


def _modmm_kernel(*refs, read_x, nx):
    mod_ref, g_ref, w_ref, o_ref = refs[nx:]
    h = _modulate(read_x(refs[:nx]), g_ref[...], mod_ref, 0, 1)
    o_ref[...] = _mm(h.astype(BF16), w_ref[...])


def _modmm(x, mod, g, w, tm=512):
    xs, x_specs, read_x = _token_specs(x, tm)
    t = sum(a.shape[0] for a in xs)
    d = xs[0].shape[1]
    n = w.shape[1]
    return pl.pallas_call(
        functools.partial(_modmm_kernel, read_x=read_x, nx=len(xs)),
        grid=(t // tm,),
        in_specs=x_specs + [pl.BlockSpec((1, 8, d), lambda i: (_cond_idx(i, tm), 0, 0)),
                            pl.BlockSpec((1, d), lambda i: (0, 0)),
                            pl.BlockSpec((d, n), lambda i: (0, 0))],
        out_specs=pl.BlockSpec((tm, n), lambda i: (i, 0)),
        out_shape=jax.ShapeDtypeStruct((t, n), F32),
        compiler_params=_cp(("parallel",)),
        name="modmm",
    )(*xs, mod, g.reshape(1, d), w)


def _swap_pairs(v):
    r = lax.broadcasted_iota(jnp.int32, (LANE, LANE), 0)
    c = lax.broadcasted_iota(jnp.int32, (LANE, LANE), 1)
    perm = ((r ^ 1) == c).astype(BF16)
    hi, lo = _split_bf16(v)
    return _mm(hi, perm) + _mm(lo, perm)


def _rope(v, cos, sin):
    return v * cos + _swap_pairs(v) * sin


def _with_ones(v):
    lane = lax.broadcasted_iota(jnp.int32, v.shape, 1)
    return jnp.where((lane & (LANE - 1)) == A_V, 1.0, v)


def _mla_kv(ckvn, kr_blk, wk_ref, wv_ref, gkn, rope_tabs, ka_o, va_o):
    cb = ckvn.astype(BF16)
    kn = _mm(cb, wk_ref[...])
    va_o[...] = _with_ones(_mm(cb, wv_ref[...])).astype(BF16)
    if rope_tabs is not None:
        cos, sin = rope_tabs
        swapped = _swap_pairs(kr_blk * gkn) * sin
    for h in range(A_HEADS):
        sl = slice(h * LANE, (h + 1) * LANE)
        kh = kn[:, sl] + kr_blk
        r = lax.rsqrt(jnp.sum(kh * kh, axis=-1, keepdims=True) * (1.0 / A_QK) + EPS)
        kh = kh * r * gkn
        if rope_tabs is not None:
            kh = kh * cos + r * swapped
        ka_o[:, sl] = kh.astype(BF16)


def _evenprep_kernel(*refs, rope):
    it = iter(refs)
    proj = next(it)
    if rope:
        cos_a, sin_a, cos_b, sin_b = (next(it)[...] for _ in range(4))
    gql, wuq, gqn, gkvl, wk, wv, gkn, gbq, gbk = (next(it) for _ in range(9))
    qa_o, ka_o, va_o, qb_o, kb_o, vb_o = (next(it) for _ in range(6))
    if not rope:
        ckv_o, kr_o, kbn_o, vbf_o = (next(it) for _ in range(4))

    cqn = _rms(proj[:, EV_CQ:EV_CQ + A_Q_LORA], A_Q_LORA) * gql[...]
    q = _mm(cqn.astype(BF16), wuq[...])
    for h in range(A_HEADS):
        sl = slice(h * LANE, (h + 1) * LANE)
        qh = _rms(q[:, sl], A_QK) * gqn[...]
        if rope:
            qh = _rope(qh, cos_a, sin_a)
        qa_o[:, sl] = (qh * (A_QK ** -0.5 * LOG2E)).astype(BF16)

    ckvn = _rms(proj[:, EV_CKV:EV_CKV + A_KV_LORA], A_KV_LORA) * gkvl[...]
    kr_blk = proj[:, EV_KR:EV_KR + LANE]
    _mla_kv(ckvn, kr_blk, wk, wv, gkn[...], (cos_a, sin_a) if rope else None, ka_o, va_o)

    for h in range(B_HEADS):
        sl = slice(h * LANE, (h + 1) * LANE)
        qh = _rms(proj[:, EV_QB + h * LANE:EV_QB + (h + 1) * LANE], B_HD) * gbq[...]
        if rope:
            qh = _rope(qh, cos_b, sin_b)
        qb_o[:, sl] = (qh * (B_HD ** -0.5 * LOG2E)).astype(BF16)
    for j in range(B_KV_HEADS):
        sl = slice(j * LANE, (j + 1) * LANE)
        kh = _rms(proj[:, EV_KB + j * LANE:EV_KB + (j + 1) * LANE], B_HD) * gbk[...]
        if not rope:
            kbn_o[0, j] = kh[:, :B_HD]
        else:
            kh = _rope(kh, cos_b, sin_b)
        kb_o[:, sl] = kh.astype(BF16)
    vb = proj[:, EV_VB:EV_VB + B_KV_HEADS * LANE]
    vb_o[...] = _with_ones(vb).astype(BF16)
    if not rope:
        for j in range(B_KV_HEADS):
            vbf_o[0, j] = vb[:, j * LANE:j * LANE + B_HD]
        ckv_o[...] = ckvn
        kr_o[...] = kr_blk


def _evenprep(proj, row_off, nrows, ew, rope_tabs):
    tm = 256
    rope = rope_tabs is not None
    off = row_off // tm
    full = lambda a: pl.BlockSpec(a.shape, lambda i: (0,) * a.ndim)
    rows = lambda w: pl.BlockSpec((tm, w), lambda i: (i, 0))
    ins = [proj]
    in_specs = [pl.BlockSpec((tm, EV_N), lambda i: (i + off, 0))]
    if rope:
        ins += list(rope_tabs)
        in_specs += [pl.BlockSpec((tm, LANE), lambda i: (i % (DEC_SEQ // tm), 0))] * 4
    ws = [ew["gql"], ew["wuq"], ew["gqn"], ew["gkvl"], ew["wk"], ew["wv"], ew["gkn"], ew["gbq"], ew["gbk"]]
    ins += ws
    in_specs += [full(a) for a in ws]
    hw = A_HEADS * LANE
    kvw = B_KV_HEADS * LANE
    out_shape = [jax.ShapeDtypeStruct((nrows, hw), BF16)] * 4 + [jax.ShapeDtypeStruct((nrows, kvw), BF16)] * 2
    out_specs = [rows(hw)] * 4 + [rows(kvw)] * 2
    if not rope:
        assert tm == SEQ
        cache_b = jax.ShapeDtypeStruct((nrows // SEQ, B_KV_HEADS, SEQ, B_HD), F32)
        cache_b_spec = pl.BlockSpec((1, B_KV_HEADS, SEQ, B_HD), lambda i: (i, 0, 0, 0))
        out_shape += [jax.ShapeDtypeStruct((nrows, A_KV_LORA), F32), jax.ShapeDtypeStruct((nrows, LANE), F32),
                      cache_b, cache_b]
        out_specs += [rows(A_KV_LORA), rows(LANE), cache_b_spec, cache_b_spec]
    return pl.pallas_call(
        functools.partial(_evenprep_kernel, rope=rope),
        grid=(nrows // tm,),
        in_specs=in_specs,
        out_specs=out_specs,
        out_shape=out_shape,
        compiler_params=_cp(("parallel",)),
        name="evenprep_rope" if rope else "evenprep",
    )(*ins)


def _ctxkv_kernel(ckv_ref, kr_ref, wk, wv, gkn, ka_o, va_o):
    _mla_kv(ckv_ref[...], kr_ref[...], wk, wv, gkn[...], None, ka_o, va_o)


def _ctxkv(ckvn, kr_blk, ew):
    n = ckvn.shape[0]
    hw = A_HEADS * LANE
    return pl.pallas_call(
        _ctxkv_kernel,
        out_shape=[jax.ShapeDtypeStruct((n, hw), BF16)] * 2,
        compiler_params=pltpu.CompilerParams(vmem_limit_bytes=VMEM_LIMIT),
        name="ctxkv",
    )(ckvn, kr_blk, ew["wk"], ew["wv"], ew["gkn"])


def _attn_kernel(*refs, hp, q_per_kv, has_ctx):
    if has_ctx:
        q_ref, kn_ref, vn_ref, kc_ref, vc_ref, _, o_ref = refs
    else:
        q_ref, kn_ref, vn_ref, _, o_ref = refs
    outs = []
    for h in range(hp):
        kv = slice(h // q_per_kv * LANE, (h // q_per_kv + 1) * LANE)
        q = q_ref[:, h * LANE:(h + 1) * LANE]
        lk = kn_ref.shape[0]
        ck = lk // ATT_CHUNKS if has_ctx else lk
        parts = [(kn_ref, vn_ref, k0, ck) for k0 in range(0, lk, ck)]
        if has_ctx:
            parts.append((kc_ref, vc_ref, 0, kc_ref.shape[0]))
        m = acc = None
        for k_ref, v_ref, k0, n in parts:
            s = _mm_nt(q, k_ref[k0:k0 + n, kv])
            m_c = jnp.max(s, axis=-1, keepdims=True)
            m_new = m_c if m is None else jnp.maximum(m, m_c)
            p = jnp.exp2(s - m_new)
            a_c = _mm(p.astype(BF16), v_ref[k0:k0 + n, kv])
            acc = a_c if m is None else jnp.exp2(m - m_new) * acc + a_c
            m = m_new
        outs.append(acc * (1.0 / acc[:, A_V:A_V + 1]))
    for p2 in range(hp // 2):
        pair = jnp.concatenate([outs[2 * p2][:, :A_V], outs[2 * p2 + 1][:, :A_V]], axis=-1)
        o_ref[:, p2 * LANE:(p2 + 1) * LANE] = pair.astype(BF16)


def _attn(q, kn, vn, ctx, into, *, nb, lq, lk, nheads, hp, q_per_kv, tq):
    ngroups = nheads // hp
    kvw = hp // q_per_kv * LANE
    nq = lq // tq
    ins = [q, kn, vn]
    in_specs = [pl.BlockSpec((tq, hp * LANE), lambda b, g, i: (b * nq + i, g)),
                pl.BlockSpec((lk, kvw), lambda b, g, i: (b, g)),
                pl.BlockSpec((lk, kvw), lambda b, g, i: (b, g))]
    row0 = 0
    if ctx is not None:
        ins += list(ctx)
        in_specs += [pl.BlockSpec((PAST_LEN, kvw), lambda b, g, i: (b, g))] * 2
        row0 = TP // tq
    ins.append(into)
    in_specs.append(pl.BlockSpec(memory_space=pl.ANY))
    aliases = {len(ins) - 1: 0}
    return pl.pallas_call(
        functools.partial(_attn_kernel, hp=hp, q_per_kv=q_per_kv, has_ctx=ctx is not None),
        grid=(nb, ngroups, nq),
        in_specs=in_specs,
        out_specs=pl.BlockSpec((tq, hp * A_V), lambda b, g, i: (row0 + b * nq + i, g)),
        out_shape=jax.ShapeDtypeStruct((T_ALL, nheads * A_V), BF16),
        input_output_aliases=aliases,
        compiler_params=_cp(("parallel", "parallel", "arbitrary")),
        name="attn_ctx" if ctx is not None else "attn",
    )(*ins)


def _outproj_kernel(*refs, glu, read_x, nx):
    x = read_x(refs[:nx])
    if glu:
        mod_ref, a_ref, b_ref, wa_ref, wb_ref, wg_ref, o_ref = refs[nx:]
        yd = jax.nn.gelu(b_ref[...])
        b = (yd * jax.nn.sigmoid(_mm(yd.astype(BF16), wg_ref[...]))).astype(BF16)
    else:
        mod_ref, a_ref, b_ref, wa_ref, wb_ref, o_ref = refs[nx:]
        b = b_ref[...]
    acc = _mm(a_ref[...], wa_ref[...]) + _mm(b, wb_ref[...])
    o_ref[...] = x + mod_ref[0, 2:3, :] * acc


def _outproj(x, mod, a, b, wa, wb, wg=None, tm=512):
    xs, x_specs, read_x = _token_specs(x, tm)
    t = sum(v.shape[0] for v in xs)
    d = xs[0].shape[1]
    full = lambda v: pl.BlockSpec(v.shape, lambda i: (0,) * v.ndim)
    rows = lambda w: pl.BlockSpec((tm, w), lambda i: (i, 0))
    ins = xs + [mod, a, b, wa, wb]
    in_specs = x_specs + [pl.BlockSpec((1, 8, d), lambda i: (_cond_idx(i, tm), 0, 0)), rows(a.shape[1]),
                          rows(b.shape[1]), full(wa), full(wb)]
    if wg is not None:
        ins.append(wg)
        in_specs.append(full(wg))
    return pl.pallas_call(
        functools.partial(_outproj_kernel, glu=wg is not None, read_x=read_x, nx=len(xs)),
        grid=(t // tm,),
        in_specs=in_specs,
        out_specs=rows(d),
        out_shape=jax.ShapeDtypeStruct((t, d), F32),
        compiler_params=_cp(("parallel",)),
        name="outproj_glu" if wg is not None else "outproj",
    )(*ins)


FF_SPLIT = (0, 768, FF_EXPERT)


MOE_BLK = 256
MOE_TAIL = 128


def _swiglu(h, w1, w3, w2):
    y = None
    for k in range(len(FF_SPLIT) - 1):
        sl = slice(FF_SPLIT[k], FF_SPLIT[k + 1])
        act = (_silu(_mm(h, w1[:, sl])) * _mm(h, w3[:, sl])).astype(BF16)
        part = _mm(act, w2[sl, :])
        y = part if y is None else y + part
    return y


def _ffn_kernel(*refs, routed, nj, split):
    o2_ref = None
    if routed and split:
        (x_ref, mod_ref, g_ref, rwh_ref, rwl_ref, rb_ref, w1_ref, w3_ref, w2_ref, o_ref, o2_ref,
         h_sc, acc_sc, comb_sc, rank_sc, rank_t_sc, cnt_sc) = refs
    elif routed:
        (x_ref, mod_ref, g_ref, rwh_ref, rwl_ref, rb_ref, w1_ref, w3_ref, w2_ref, o_ref,
         h_sc, acc_sc, comb_sc, rank_sc, rank_t_sc, cnt_sc) = refs
    else:
        x_ref, mod_ref, g_ref, w1_ref, w3_ref, w2_ref, o_ref, h_sc, acc_sc = refs
    j = pl.program_id(1)
    tm = x_ref.shape[0]

    @pl.when(j == 0)
    def _():
        h = _modulate(x_ref[...], g_ref[...], mod_ref, 3, 4)
        h_hi = h.astype(BF16)
        h_sc[...] = h_hi
        acc_sc[...] = jnp.zeros(acc_sc.shape, F32)
        if routed:
            h_lo = (h - h_hi.astype(F32)).astype(BF16)
            logits = _mm(h_hi, rwh_ref[...]) + _mm(h_lo, rwh_ref[...]) + _mm(h_hi, rwl_ref[...]) + rb_ref[...]
            lane = lax.broadcasted_iota(jnp.int32, logits.shape, 1)
            lg = jnp.where(lane < N_EXPERTS, logits, -jnp.inf)
            m1 = jnp.max(lg, axis=-1, keepdims=True)
            i1 = jnp.min(jnp.where(lg == m1, lane, LANE), axis=-1, keepdims=True)
            lg2 = jnp.where(lane == i1, -jnp.inf, lg)
            m2 = jnp.max(lg2, axis=-1, keepdims=True)
            i2 = jnp.min(jnp.where(lg2 == m2, lane, LANE), axis=-1, keepdims=True)
            e2 = jnp.exp(m2 - m1)
            inv = 1.0 / (1.0 + e2)
            comb_sc[...] = jnp.where(lane == i1, inv, 0.0) + jnp.where(lane == i2, e2 * inv, 0.0)
            sel = jnp.where(lane == i1, 1.0, 0.0) + jnp.where(lane == i2, 1.0, 0.0)
            ti = lax.broadcasted_iota(jnp.int32, (tm, tm), 0)
            si = lax.broadcasted_iota(jnp.int32, (tm, tm), 1)
            before = _mm((si < ti).astype(BF16), sel.astype(BF16))
            rank = jnp.where(sel > 0.0, before, -1.0)
            rank_sc[...] = rank
            cnt_sc[...] = jnp.broadcast_to(before[tm - 1:tm, :] + sel[tm - 1:tm, :], cnt_sc.shape)
            for k in range(tm // LANE):
                rank_t_sc[:, k * LANE:(k + 1) * LANE] = rank[k * LANE:(k + 1) * LANE, :].T

    if routed:
        lane = lax.broadcasted_iota(jnp.int32, comb_sc.shape, 1)
        lane8 = lax.broadcasted_iota(jnp.int32, cnt_sc.shape, 1)
        n = jnp.sum(jnp.where(lane8 == j, cnt_sc[...], 0.0)[0:1, :]).astype(jnp.int32)
        shift = int(math.log2(MOE_BLK))
        nfull = lax.shift_right_logical(n, shift)
        rem = n - lax.shift_left(nfull, shift)
        gate = jnp.sum(jnp.where(lane == j, comb_sc[...], 0.0), axis=-1, keepdims=True)
        rank_col = jnp.sum(jnp.where(lane == j, rank_sc[...], 0.0), axis=-1, keepdims=True)
        rank_row = rank_t_sc[pl.ds(j, 1), :]

        def run(r0, rows):
            r0 = r0.astype(F32)
            r_iota = lax.broadcasted_iota(jnp.int32, (rows, tm), 0).astype(F32)
            c_iota = lax.broadcasted_iota(jnp.int32, (tm, rows), 1).astype(F32)
            pick = jnp.where(rank_row - r0 == r_iota, 1.0, 0.0).astype(BF16)
            hb = _mm(pick, h_sc[...]).astype(BF16)
            yb = _swiglu(hb, w1_ref.at[0, 0], w3_ref.at[0, 0], w2_ref.at[0, 0]).astype(BF16)
            place = jnp.where(rank_col - r0 == c_iota, 1.0, 0.0).astype(BF16)
            acc_sc[...] += gate * _mm(place, yb)

        def block(b, carry):
            run(b * MOE_BLK, MOE_BLK)
            return carry
        lax.fori_loop(0, nfull + (rem > MOE_TAIL).astype(jnp.int32), block, 0)

        @pl.when((rem > 0) & (rem <= MOE_TAIL))
        def _():
            run(nfull * MOE_BLK, MOE_TAIL)
    else:
        acc_sc[...] += _swiglu(h_sc[...], w1_ref.at[0], w3_ref.at[0], w2_ref.at[0])

    @pl.when(j == nj - 1)
    def _():
        y = x_ref[...] + mod_ref[0, 5:6, :] * acc_sc[...]
        if o2_ref is None:
            o_ref[...] = y
        else:
            first = pl.program_id(0) < TP // tm

            @pl.when(first)
            def _():
                o_ref[...] = y

            @pl.when(jnp.logical_not(first))
            def _():
                o2_ref[...] = y


def _ffn(x, mod, g, w1, w3, w2, li, router=None, tm=1024, split=False):
    t, d = x.shape
    routed = router is not None
    tf = FF_EXPERT
    nj = w1.shape[1] if routed else w1.shape[2] // tf
    ins = [x, mod, g.reshape(1, d)]
    in_specs = [pl.BlockSpec((tm, d), lambda i, j: (i, 0)),
                pl.BlockSpec((1, 8, d), lambda i, j: (_cond_idx(i, tm), 0, 0)),
                pl.BlockSpec((1, d), lambda i, j: (0, 0))]
    scratch = [pltpu.VMEM((tm, d), BF16), pltpu.VMEM((tm, d), F32)]
    if routed:
        ins += list(router)
        in_specs += [pl.BlockSpec((d, LANE), lambda i, j: (0, 0))] * 2 + [pl.BlockSpec((1, LANE), lambda i, j: (0, 0))]
        in_specs += ([pl.BlockSpec((1, 1, d, tf), lambda i, j: (li, j, 0, 0))] * 2
                     + [pl.BlockSpec((1, 1, tf, d), lambda i, j: (li, j, 0, 0))])
        scratch += [pltpu.VMEM((tm, LANE), F32), pltpu.VMEM((tm, LANE), F32), pltpu.VMEM((LANE, tm), F32),
                    pltpu.VMEM((8, LANE), F32)]
    else:
        in_specs += ([pl.BlockSpec((1, d, tf), lambda i, j: (li, 0, j))] * 2
                     + [pl.BlockSpec((1, tf, d), lambda i, j: (li, j, 0))])
    ins += [w1, w3, w2]
    if split:
        assert routed
        n0 = TP // tm
        out_specs = [pl.BlockSpec((tm, d), lambda i, j: (jnp.minimum(i, n0 - 1), 0)),
                     pl.BlockSpec((tm, d), lambda i, j: (jnp.maximum(i - n0, 0), 0))]
        out_shape = [jax.ShapeDtypeStruct((TP, d), F32), jax.ShapeDtypeStruct((TS, d), F32)]
    else:
        out_specs = pl.BlockSpec((tm, d), lambda i, j: (i, 0))
        out_shape = jax.ShapeDtypeStruct((t, d), F32)
    return pl.pallas_call(
        functools.partial(_ffn_kernel, routed=routed, nj=nj, split=split),
        grid=(t // tm, nj),
        in_specs=in_specs,
        out_specs=out_specs,
        out_shape=out_shape,
        scratch_shapes=scratch,
        compiler_params=_cp(("arbitrary" if split else "parallel", "arbitrary")),
        name=("moe_split" if split else "moe") if routed else "ffn",
    )(*ins)


CONV_ROWS = 4096


def _conv_kernel(x_ref, w_ref, b_ref, o_ref):
    i = pl.program_id(0)
    seq = jnp.where(i < TP // CONV_ROWS, SEQ, DEC_SEQ)
    x = x_ref[...]
    pos = lax.broadcasted_iota(jnp.int32, x.shape, 0) & (seq - 1)
    prev = jnp.where(pos == 0, 0.0, pltpu.roll(x, 1, 0))
    nxt = jnp.where(pos == seq - 1, 0.0, pltpu.roll(x, CONV_ROWS - 1, 0))
    y = prev * w_ref[0:1, :] + x * w_ref[1:2, :] + nxt * w_ref[2:3, :] + b_ref[...]
    o_ref[...] = _silu(y)


def _conv(proj, w, b):
    tc = 256
    t = proj.shape[0]
    return pl.pallas_call(
        _conv_kernel,
        grid=(t // CONV_ROWS, C_CONV_DIM // tc),
        in_specs=[pl.BlockSpec((CONV_ROWS, tc), lambda i, j: (i, j + OD_XBC // tc)),
                  pl.BlockSpec((3, tc), lambda i, j: (0, j)),
                  pl.BlockSpec((1, tc), lambda i, j: (0, j))],
        out_specs=pl.BlockSpec((CONV_ROWS, tc), lambda i, j: (i, j)),
        out_shape=jax.ShapeDtypeStruct((t, C_CONV_DIM), F32),
        compiler_params=_cp(("parallel", "parallel")),
        name="conv",
    )(proj, w, b.reshape(1, C_CONV_DIM))


Q = SSD_CHUNK
NCH = T_ALL // Q
NCH_P = TP // Q
NC_P = SEQ // Q
NC_S = DEC_SEQ // Q
SSD_PAR = 2


def _chunk_seq(s):
    is_p = s < NCH_P
    nc = jnp.where(is_p, NC_P, NC_S)
    seq = jnp.where(is_p, s // NC_P, BATCH + (s - NCH_P) // NC_S)
    c = jnp.where(is_p, s % NC_P, (s - NCH_P) % NC_S)
    return seq, c, nc, s - c


def _chunk_mirror(s):
    _, c, nc, start = _chunk_seq(s)
    return start + nc - 1 - c


def _ssd_dt(dt_raw, dt_bias, a_row):
    z = dt_raw + dt_bias
    dt = jnp.maximum(z, 0.0) + jnp.log(1.0 + jnp.exp(-jnp.abs(z)))
    la = dt * a_row
    r = lax.broadcasted_iota(jnp.int32, (Q, Q), 0)
    cc = lax.broadcasted_iota(jnp.int32, (Q, Q), 1)
    tri = (cc <= r).astype(BF16)
    hi, lo = _split_bf16(la)
    return dt, la, _mm(tri, hi) + _mm(tri, lo)


def _expand(v, e32_ref):
    hi, lo = _split_bf16(v)
    return _mm(hi, e32_ref[...]) + _mm(lo, e32_ref[...])


def _ssd1_kernel(xf_ref, xb_ref, df_ref, db_ref, dtb_ref, a_ref, e32_ref, h0_ref, hf_o, hb_o, fin_o, hf_sc, hb_sc):
    s = SSD_PAR * pl.program_id(0)
    _, c, nc, _ = _chunk_seq(s)

    @pl.when(c == 0)
    def _():
        hf_sc[...] = h0_ref[0, 0]
        hb_sc[...] = h0_ref[0, 1]

    lane = lax.broadcasted_iota(jnp.int32, (Q, LANE), 1)
    is_f = lane < C_HEADS
    is_f8 = lax.broadcasted_iota(jnp.int32, (8, LANE), 1) < C_HEADS
    for k in range(SSD_PAR):
        rf = slice(k * Q, (k + 1) * Q)
        rb = slice((SSD_PAR - 1 - k) * Q, (SSD_PAR - k) * Q)
        hf_o[k] = hf_sc[...].astype(BF16)
        hb_o[SSD_PAR - 1 - k] = hb_sc[...].astype(BF16)
        dt_f, _, ac_f = _ssd_dt(df_ref[rf, :], dtb_ref[...], a_ref[...])
        tot_f = ac_f[Q - 1:Q, :]
        dt_b, la_b, ac_b = _ssd_dt(db_ref[rb, :], dtb_ref[...], a_ref[...])
        tot_b = ac_b[Q - 1:Q, :]
        w = jnp.where(is_f, jnp.exp(tot_f - ac_f) * dt_f, jnp.exp(ac_b - la_b) * dt_b)
        dec = jnp.where(is_f8, jnp.exp(jnp.broadcast_to(tot_f, (8, LANE))),
                        jnp.exp(jnp.broadcast_to(tot_b, (8, LANE))))
        w_x = _expand(w, e32_ref)
        dec_x = _expand(dec, e32_ref)[0:1, :]
        for d, (x_ref, r, st) in enumerate(((xf_ref, rf, hf_sc), (xb_ref, rb, hb_sc))):
            xw = (x_ref[r, 0:C_INNER] * w_x[:, d * C_INNER:(d + 1) * C_INNER]).astype(BF16)
            for g in range(C_GROUPS):
                bm_t = x_ref[r, C_INNER + g * C_STATE:C_INNER + (g + 1) * C_STATE].T.astype(BF16)
                cs = slice(g * C_HPG * C_HD, (g + 1) * C_HPG * C_HD)
                dcs = slice(d * C_INNER + g * C_HPG * C_HD, d * C_INNER + (g + 1) * C_HPG * C_HD)
                st[:, cs] = st[:, cs] * dec_x[:, dcs] + _mm(bm_t, xw[:, cs])

    @pl.when((c == nc - SSD_PAR) & (s < NCH_P))
    def _():
        for d, st in enumerate((hf_sc, hb_sc)):
            for k in range(C_INNER // LANE):
                fin_o[0, d, k * LANE:(k + 1) * LANE, :] = st[:, k * LANE:(k + 1) * LANE].T


def _ssd1(xa, proj, dtb, a_row, e32, h0):
    dcol = OD_DT // LANE
    rows = SSD_PAR * Q
    first = lambda i: SSD_PAR * i
    mirror = lambda i: _chunk_mirror(SSD_PAR * i + SSD_PAR - 1) // SSD_PAR
    h0_idx = lambda i: (jnp.where(first(i) < NCH_P, 0, 1 + (first(i) - NCH_P) // NC_S), 0, 0, 0)
    return pl.pallas_call(
        _ssd1_kernel,
        grid=(NCH // SSD_PAR,),
        in_specs=[pl.BlockSpec((rows, C_CONV_DIM), lambda i: (i, 0)),
                  pl.BlockSpec((rows, C_CONV_DIM), lambda i: (mirror(i), 0)),
                  pl.BlockSpec((rows, LANE), lambda i: (i, dcol)),
                  pl.BlockSpec((rows, LANE), lambda i: (mirror(i), dcol)),
                  pl.BlockSpec((1, LANE), lambda i: (0, 0)),
                  pl.BlockSpec((1, LANE), lambda i: (0, 0)),
                  pl.BlockSpec((LANE, 2 * C_INNER), lambda i: (0, 0)),
                  pl.BlockSpec((1, 2, C_STATE, C_INNER), h0_idx)],
        out_specs=[pl.BlockSpec((SSD_PAR, C_STATE, C_INNER), lambda i: (i, 0, 0)),
                   pl.BlockSpec((SSD_PAR, C_STATE, C_INNER), lambda i: (mirror(i), 0, 0)),
                   pl.BlockSpec((1, 2, C_INNER, C_STATE),
                                lambda i: (jnp.minimum(_chunk_seq(first(i))[0], BATCH - 1), 0, 0, 0))],
        out_shape=[jax.ShapeDtypeStruct((NCH, C_STATE, C_INNER), BF16),
                   jax.ShapeDtypeStruct((NCH, C_STATE, C_INNER), BF16),
                   jax.ShapeDtypeStruct((BATCH, 2, C_INNER, C_STATE), F32)],
        scratch_shapes=[pltpu.VMEM((C_STATE, C_INNER), F32)] * 2,
        compiler_params=_cp(("arbitrary",)),
        name="ssd_states",
    )(xa, xa, proj, proj, dtb, a_row, e32, h0)


def _ssd2_kernel(xa_ref, d_ref, z_ref, hf_ref, hb_ref, dtb_ref, a_ref, e32_ref, dsk_ref, gn_ref, o_ref, y_sc):
    for k in range(SSD_PAR):
        _ssd2_chunk(k, xa_ref, d_ref, z_ref, hf_ref, hb_ref, dtb_ref, a_ref, e32_ref, dsk_ref, gn_ref, o_ref, y_sc)


def _ssd2_chunk(k, xa_ref, d_ref, z_ref, hf_ref, hb_ref, dtb_ref, a_ref, e32_ref, dsk_ref, gn_ref, o_ref, y_sc):
    r = slice(k * Q, (k + 1) * Q)
    dt, la, acum = _ssd_dt(d_ref[r, :], dtb_ref[...], a_ref[...])
    tot = acum[Q - 1:Q, :]
    excl = acum - la
    lane = lax.broadcasted_iota(jnp.int32, (Q, LANE), 1)
    is_f = lane < C_HEADS
    logdt = jnp.log(dt)
    col = jnp.where(is_f, acum, excl)
    row_t = jnp.where(is_f, acum - logdt, excl + logdt).T
    dec_x = _expand(jnp.where(is_f, jnp.exp(acum), jnp.exp(tot - excl)), e32_ref)
    x = xa_ref[r, 0:C_INNER]
    xb16 = x.astype(BF16)
    ti = lax.broadcasted_iota(jnp.int32, (Q, Q), 0)
    si = lax.broadcasted_iota(jnp.int32, (Q, Q), 1)
    low = si <= ti
    upp = si >= ti
    lane_q = lax.broadcasted_iota(jnp.int32, (Q, LANE), 1)
    gw = C_HPG * C_HD
    for g in range(C_GROUPS):
        bm = xa_ref[r, C_INNER + g * C_STATE:C_INNER + (g + 1) * C_STATE].astype(BF16)
        cm = xa_ref[r, C_INNER + (C_GROUPS + g) * C_STATE:C_INNER + (C_GROUPS + g + 1) * C_STATE].astype(BF16)
        gmat = _mm_nt(cm, bm)
        cs = slice(g * gw, (g + 1) * gw)
        ys = (_mm(cm, hf_ref[k, :, cs]) * dec_x[:, cs]
              + _mm(cm, hb_ref[k, :, cs]) * dec_x[:, C_INNER + g * gw:C_INNER + (g + 1) * gw])
        for pr in range(C_HPG // 2):
            xpair = xb16[:, g * gw + pr * LANE:g * gw + (pr + 1) * LANE]
            res = []
            for e in range(2):
                h = g * C_HPG + 2 * pr + e
                hb_ = C_HEADS + h
                m_f = jnp.where(low, jnp.exp(col[:, h:h + 1] - row_t[h:h + 1, :]), 0.0)
                m_b = jnp.where(upp, jnp.exp(row_t[hb_:hb_ + 1, :] - col[:, hb_:hb_ + 1]), 0.0)
                res.append(_mm((gmat * (m_f + m_b)).astype(BF16), xpair))
            y_sc[r, g * gw + pr * LANE:g * gw + (pr + 1) * LANE] = (
                jnp.where(lane_q < C_HD, res[0], res[1]) + ys[:, pr * LANE:(pr + 1) * LANE])
    y = y_sc[r, :] + dsk_ref[...] * x
    y = y * _silu(z_ref[r, :])
    y = y * lax.rsqrt(jnp.mean(y * y, axis=-1, keepdims=True) + EPS) * gn_ref[...]
    o_ref[r, :] = y.astype(BF16)


def _ssd2(xa, proj, hf, hb, dtb, a_row, e32, dsk, gn):
    dcol = OD_DT // LANE
    rows = SSD_PAR * Q
    one = lambda w: pl.BlockSpec((1, w), lambda s: (0, 0))
    return pl.pallas_call(
        _ssd2_kernel,
        grid=(NCH // SSD_PAR,),
        in_specs=[pl.BlockSpec((rows, C_CONV_DIM), lambda s: (s, 0)),
                  pl.BlockSpec((rows, LANE), lambda s: (s, dcol)),
                  pl.BlockSpec((rows, C_INNER), lambda s: (s, 0)),
                  pl.BlockSpec((SSD_PAR, C_STATE, C_INNER), lambda s: (s, 0, 0)),
                  pl.BlockSpec((SSD_PAR, C_STATE, C_INNER), lambda s: (s, 0, 0)),
                  one(LANE), one(LANE),
                  pl.BlockSpec((LANE, 2 * C_INNER), lambda s: (0, 0)),
                  one(C_INNER), one(C_INNER)],
        out_specs=pl.BlockSpec((rows, C_INNER), lambda s: (s, 0)),
        out_shape=jax.ShapeDtypeStruct((T_ALL, C_INNER), BF16),
        scratch_shapes=[pltpu.VMEM((rows, C_INNER), F32)],
        compiler_params=_cp(("parallel",)),
        name="ssd_out",
    )(xa, proj, proj, hf, hb, dtb, a_row, e32, dsk, gn)


S5_UNROLL = 8
S5_MM_STEPS = 128


def _cmul(ar, ai, br, bi):
    return ar * br - ai * bi, ar * bi + ai * br


def _s5_kernel(u_ref, wx_ref, cw_ref, a_ref, dsk_ref, h0_ref, y_ref, st_ref, xs, *, nseg):
    ct = pl.program_id(0)
    quarter = ct % 4
    nslab = S5_STEPS // S5_MM_STEPS
    rows = S5_MM_STEPS * S5_CHAINS

    def fill(k, carry):
        r0 = pl.multiple_of(k * S5_MM_STEPS, S5_MM_STEPS)
        ub = u_ref[pl.ds(r0, S5_MM_STEPS)].reshape(rows, LANE).astype(BF16)
        xs[pl.ds(r0, S5_MM_STEPS)] = _mm(ub, wx_ref[0]).reshape(S5_MM_STEPS, S5_CHAINS, 4 * LANE)
        return carry
    lax.fori_loop(0, nslab, fill, 0)

    shape = (S5_CHAINS, LANE)
    a = a_ref[0]
    afr, afi, abr, abi = (jnp.broadcast_to(a[k:k + 1, :], shape) for k in range(4))

    def scan(init, store):
        def body(i, carry):
            hfr, hfi, hbr, hbi = carry
            j = S5_STEPS - 1 - i
            pr, pi = _cmul(afr, afi, hfr, hfi)
            nfr = pr + xs[i, :, 0:LANE]
            nfi = pi + xs[i, :, LANE:2 * LANE]
            pr, pi = _cmul(abr, abi, hbr, hbi)
            nbr = pr + xs[j, :, 2 * LANE:3 * LANE]
            nbi = pi + xs[j, :, 3 * LANE:4 * LANE]
            if store:
                xs[i, :, 0:LANE] = nfr
                xs[i, :, LANE:2 * LANE] = nfi
                xs[j, :, 2 * LANE:3 * LANE] = nbr
                xs[j, :, 3 * LANE:4 * LANE] = nbi
            return nfr, nfi, nbr, nbi
        return lax.fori_loop(0, S5_STEPS, body, init, unroll=S5_UNROLL)

    zero = jnp.zeros(shape, F32)
    if nseg == 1:
        init = (zero, zero, zero, zero)
    else:
        efr, efi, ebr, ebi = scan((zero, zero, zero, zero), store=False)
        pfr, pfi, pbr, pbi = afr, afi, abr, abi
        for _ in range(int(math.log2(S5_STEPS))):
            pfr, pfi = _cmul(pfr, pfi, pfr, pfi)
            pbr, pbi = _cmul(pbr, pbi, pbr, pbi)
        row = lax.broadcasted_iota(jnp.int32, shape, 0) % nseg
        first = row == 0
        last = row == nseg - 1
        h0 = h0_ref[...]
        cfr, cfi, cbr, cbi = zero, zero, zero, zero
        dn = lambda v: pltpu.roll(v, 1, 0)
        up = lambda v: pltpu.roll(v, S5_CHAINS - 1, 0)
        for _ in range(nseg):
            pr, pi = _cmul(pfr, pfi, dn(cfr), dn(cfi))
            cfr = jnp.where(first, h0[0], pr + dn(efr))
            cfi = jnp.where(first, h0[1], pi + dn(efi))
            pr, pi = _cmul(pbr, pbi, up(cbr), up(cbi))
            cbr = jnp.where(last, h0[2], pr + up(ebr))
            cbi = jnp.where(last, h0[3], pi + up(ebi))
        init = (cfr, cfi, cbr, cbi)
    fin = scan(init, store=True)
    for k in range(4):
        st_ref[k] = fin[k]

    def emit(k, carry):
        r0 = pl.multiple_of(k * S5_MM_STEPS, S5_MM_STEPS)
        blk = xs[pl.ds(r0, S5_MM_STEPS)]
        s_re = blk[:, :, 0:LANE] + blk[:, :, 2 * LANE:3 * LANE]
        s_im = blk[:, :, LANE:2 * LANE] + blk[:, :, 3 * LANE:4 * LANE]
        s2 = jnp.concatenate([s_re, s_im], axis=-1).reshape(rows, 2 * LANE).astype(BF16)
        yv = _mm(s2, cw_ref[0]).reshape(S5_MM_STEPS, S5_CHAINS, LANE)

        @pl.when(quarter == 0)
        def _():
            y_ref[pl.ds(r0, S5_MM_STEPS)] = yv + dsk_ref[...] * u_ref[pl.ds(r0, S5_MM_STEPS)]

        @pl.when(quarter != 0)
        def _():
            y_ref[pl.ds(r0, S5_MM_STEPS)] += yv
        return carry
    lax.fori_loop(0, nslab, emit, 0)


def _s5(u_perm, sw, h0, nseg):
    nct = D_GROUPS // 2
    return pl.pallas_call(
        functools.partial(_s5_kernel, nseg=nseg),
        grid=(nct,),
        in_specs=[pl.BlockSpec((S5_STEPS, S5_CHAINS, LANE), lambda ct: (0, 0, ct // 4)),
                  pl.BlockSpec((1, LANE, 4 * LANE), lambda ct: (ct, 0, 0)),
                  pl.BlockSpec((1, 2 * LANE, LANE), lambda ct: (ct, 0, 0)),
                  pl.BlockSpec((1, 8, LANE), lambda ct: (ct, 0, 0)),
                  pl.BlockSpec((1, LANE), lambda ct: (0, ct // 4)),
                  pl.BlockSpec((4, S5_CHAINS, LANE), lambda ct: (0, 0, ct))],
        out_specs=[pl.BlockSpec((S5_STEPS, S5_CHAINS, LANE), lambda ct: (0, 0, ct // 4)),
                   pl.BlockSpec((4, S5_CHAINS, LANE), lambda ct: (0, 0, ct))],
        out_shape=[jax.ShapeDtypeStruct((S5_STEPS, S5_CHAINS, D_WIDTH), F32),
                   jax.ShapeDtypeStruct((4, S5_CHAINS, nct * LANE), F32)],
        scratch_shapes=[pltpu.VMEM((S5_STEPS, S5_CHAINS, 4 * LANE), F32)],
        compiler_params=_cp(("arbitrary",)),
        name="s5_seg" if nseg > 1 else "s5",
    )(u_perm, sw["wx"], sw["cw"], sw["a"], sw["dsk"], h0)


def _pad_last(a, n):
    return jnp.pad(a, [(0, 0)] * (a.ndim - 1) + [(0, n - a.shape[-1])])


def _even_weights(w_in, q_lora_g, w_uq, kv_lora_g, w_ukv, q_norm_g, k_norm_g, bq_g, bk_g):
    d = w_in.shape[0]
    s0, s1, s2, s3, s4 = 384, 640, 672, 1184, 1312
    kr = jnp.pad(w_in[:, s1:s2], ((0, 0), (A_NOPE, LANE - A_QK)))
    heads = lambda w, n: _pad_last(w.reshape(d, n, B_HD), LANE).reshape(d, n * LANE)
    w_perm = jnp.concatenate([w_in[:, :s0], w_in[:, s0:s1], kr, heads(w_in[:, s2:s3], B_HEADS),
                              heads(w_in[:, s3:s4], B_KV_HEADS), heads(w_in[:, s4:], B_KV_HEADS)], axis=1)
    row = lambda g: _pad_last(g.reshape(1, -1), LANE)
    return {
        "w_in": w_perm.astype(BF16),
        "gql": q_lora_g.reshape(1, -1),
        "wuq": _pad_last(w_uq, LANE).reshape(A_Q_LORA, A_HEADS * LANE).astype(BF16),
        "gqn": row(q_norm_g),
        "gkvl": kv_lora_g.reshape(1, -1),
        "wk": _pad_last(w_ukv[:, :, :A_NOPE], LANE).reshape(A_KV_LORA, A_HEADS * LANE).astype(BF16),
        "wv": _pad_last(w_ukv[:, :, A_NOPE:], LANE).reshape(A_KV_LORA, A_HEADS * LANE).astype(BF16),
        "gkn": row(k_norm_g),
        "gbq": row(bq_g),
        "gbk": row(bk_g),
    }


def _rope_tables():
    rows = DEC_SEQ // GRID_W

    def angles(dim):
        half = dim // 2
        inv = ROPE_THETA ** (-jnp.arange(0, half, 2, dtype=F32) / half)
        r = jnp.repeat(jnp.arange(rows, dtype=F32), GRID_W)
        col = jnp.tile(jnp.arange(GRID_W, dtype=F32), rows)
        return jnp.concatenate([r[:, None] * inv, col[:, None] * inv], axis=-1)

    def lanes(dim, start):
        ang = angles(dim)
        cos = jnp.repeat(jnp.cos(ang), 2, axis=-1)
        sin = jnp.repeat(jnp.sin(ang), 2, axis=-1) * jnp.tile(jnp.array([-1.0, 1.0], F32), dim // 2)
        pad = ((0, 0), (start, LANE - start - dim))
        return jnp.pad(cos, pad, constant_values=1.0), jnp.pad(sin, pad)

    cos_a, sin_a = lanes(A_ROPE, A_NOPE)
    cos_b, sin_b = lanes(B_HD, 0)
    return cos_a, sin_a, cos_b, sin_b


def _odd_weights(w_in, dt_bias, a_log, c_d_skip):
    w_perm = jnp.concatenate([w_in[:, :C_INNER], w_in[:, C_INNER:C_INNER + C_CONV_DIM],
                              w_in[:, C_INNER + C_CONV_DIM + 2 * C_HEADS:],
                              _pad_last(w_in[:, C_INNER + C_CONV_DIM:C_INNER + C_CONV_DIM + 2 * C_HEADS], LANE)], axis=1)
    lane = jnp.arange(LANE)
    col = jnp.arange(2 * C_INNER)
    e32 = ((lane[:, None] < 2 * C_HEADS) & (lane[:, None] == col[None, :] // C_HD)).astype(BF16)
    return {
        "w_in": w_perm.astype(BF16),
        "dtb": _pad_last(dt_bias.reshape(1, -1), LANE),
        "a_row": _pad_last(-jnp.exp(a_log).reshape(1, -1), LANE),
        "e32": e32,
        "dsk": jnp.repeat(c_d_skip, C_HD).reshape(1, C_INNER),
    }


def _s5_weights(lam_re, lam_im, log_dt, b_re, b_im, c_re, c_im, d_skip):
    nct = D_GROUPS // 2
    dt = jnp.exp(log_dt)[..., None]
    mag = jnp.exp(lam_re * dt)
    ar, ai = mag * jnp.cos(lam_im * dt), mag * jnp.sin(lam_im * dt)
    den = lam_re * lam_re + lam_im * lam_im
    kr = ((ar - 1.0) * lam_re + ai * lam_im) / den
    ki = (ai * lam_re - (ar - 1.0) * lam_im) / den
    xr = kr[..., None] * b_re[None] - ki[..., None] * b_im[None]
    xi = kr[..., None] * b_im[None] + ki[..., None] * b_re[None]
    xc = jnp.stack([xr[0], xi[0], xr[1], xi[1]]).reshape(4, nct, 2, D_STATE, D_GROUP_SIZE)
    quarter = (jnp.arange(nct)[:, None] % 4 == jnp.arange(4)[None, :]).astype(F32)
    eye2 = jnp.eye(2, dtype=F32)
    wx = jnp.einsum("tq,gh,ktgnj->tqgjkhn", quarter, eye2, xc).reshape(nct, LANE, 4 * LANE)
    cc = jnp.stack([c_re, -c_im]).reshape(2, nct, 2, D_GROUP_SIZE, D_STATE)
    cw = jnp.einsum("tq,gh,rtgcn->trhnqgc", quarter, eye2, cc).reshape(nct, 2 * LANE, LANE)
    planes = jnp.stack([ar[0], ai[0], ar[1], ai[1]]).reshape(4, nct, LANE).transpose(1, 0, 2)
    return {"wx": wx.astype(BF16), "cw": cw.astype(BF16), "a": jnp.pad(planes, ((0, 0), (0, 4), (0, 0))),
            "dsk": d_skip.reshape(1, D_WIDTH)}


def _even_layer(x, mod, norm1_g, norm2_g, ew, w_out, ffn_w, caches, rope_tabs):
    c_ckv, c_krope, c_kb, c_vb = caches
    proj = _modmm(x, mod, norm1_g, ew["w_in"])
    qa_p, ka_p, va_p, qb_p, kb_p, vb_p, ckv_c, kr_c, kbn_c, vb_c = _evenprep(proj, 0, TP, ew, None)
    qa_s, ka_s, va_s, qb_s, kb_s, vb_s = _evenprep(proj, TP, TS, ew, rope_tabs)
    n_ctx = DEC_BATCH * PAST_LEN
    kr_blk = jnp.pad(c_krope.reshape(n_ctx, A_ROPE), ((0, 0), (A_NOPE, LANE - A_QK)))
    ka_c, va_c = _ctxkv(c_ckv.reshape(n_ctx, A_KV_LORA), kr_blk, ew)
    heads_b = lambda a: _pad_last(a.transpose(0, 2, 1, 3), LANE).reshape(n_ctx, B_KV_HEADS * LANE)
    ones_col = (jnp.arange(B_KV_HEADS * LANE) % LANE == A_V).astype(F32)
    gqa = B_HEADS // B_KV_HEADS
    oa = jnp.zeros((T_ALL, A_HEADS * A_V), BF16)
    ob = jnp.zeros((T_ALL, B_HEADS * A_V), BF16)
    oa = _attn(qa_p, ka_p, va_p, None, oa, nb=BATCH, lq=SEQ, lk=SEQ, nheads=A_HEADS, hp=8, q_per_kv=1, tq=SEQ)
    ob = _attn(qb_p, kb_p, vb_p, None, ob, nb=BATCH, lq=SEQ, lk=SEQ, nheads=B_HEADS, hp=4, q_per_kv=gqa, tq=SEQ)
    oa = _attn(qa_s, ka_s, va_s, (ka_c, va_c), oa, nb=DEC_BATCH, lq=DEC_SEQ, lk=DEC_SEQ, nheads=A_HEADS, hp=4,
               q_per_kv=1, tq=ATT_TQ)
    ctx_b = (heads_b(c_kb).astype(BF16), (heads_b(c_vb) + ones_col).astype(BF16))
    ob = _attn(qb_s, kb_s, vb_s, ctx_b, ob, nb=DEC_BATCH, lq=DEC_SEQ, lk=DEC_SEQ, nheads=B_HEADS, hp=8,
               q_per_kv=gqa, tq=ATT_TQ)
    x = _outproj(x, mod, oa, ob, w_out[:A_HEADS * A_V], w_out[A_HEADS * A_V:])
    x = _ffn(x, mod, norm2_g, *ffn_w)
    new_cache = (ckv_c.reshape(BATCH, SEQ, A_KV_LORA), kr_c[:, A_NOPE:A_QK].reshape(BATCH, SEQ, A_ROPE),
                 kbn_c, vb_c)
    return x, new_cache


def _odd_layer(x, mod, norm1_g, norm2_g, ow, sw, conv_w, conv_b, c_norm_g, w_glu, w_out, moe_w, states, last):
    st_c, st_re, st_im = states
    proj = _modmm(x, mod, norm1_g, ow["w_in"])
    xa = _conv(proj, conv_w, conv_b)
    h0 = st_c.transpose(0, 1, 4, 2, 3).reshape(DEC_BATCH, 2, C_STATE, C_INNER)
    h0 = jnp.concatenate([jnp.zeros((1,) + h0.shape[1:], F32), h0], axis=0)
    hf, hb, fin = _ssd1(xa, proj, ow["dtb"], ow["a_row"], ow["e32"], h0)
    yc = _ssd2(xa, proj, hf, hb, ow["dtb"], ow["a_row"], ow["e32"], ow["dsk"], c_norm_g.reshape(1, C_INNER))
    u = proj[:, OD_U:OD_U + D_WIDTH]
    u_p = u[:TP].reshape(BATCH, SEQ, D_WIDTH).transpose(1, 0, 2)
    u_s = u[TP:].reshape(DEC_BATCH * S5_SEG, S5_STEPS, D_WIDTH).transpose(1, 0, 2)
    zeros_h0 = jnp.zeros((4, S5_CHAINS, D_GROUPS * D_STATE), F32)
    h0_s = jnp.stack([st_re[:, 0], st_im[:, 0], st_re[:, 1], st_im[:, 1]]).reshape(4, DEC_BATCH, D_GROUPS * D_STATE)
    h0_s = jnp.repeat(h0_s, S5_SEG, axis=1)
    yd_p, fin_d = _s5(u_p, sw, zeros_h0, 1)
    yd_s, _ = _s5(u_s, sw, h0_s, S5_SEG)
    yd = jnp.concatenate([yd_p.transpose(1, 0, 2).reshape(TP, D_WIDTH),
                          yd_s.transpose(1, 0, 2).reshape(TS, D_WIDTH)], axis=0)
    x = _outproj(x, mod, yc, yd, w_out[:C_INNER], w_out[C_INNER:], wg=w_glu)
    x = _ffn(x, mod, norm2_g, *moe_w[:4], router=moe_w[4], split=last)
    new_c = fin.reshape(BATCH, 2, C_HEADS, C_HD, C_STATE)
    fin_d = fin_d.reshape(4, BATCH, D_GROUPS, D_STATE)
    new_re = jnp.stack([fin_d[0], fin_d[2]], axis=1)
    new_im = jnp.stack([fin_d[1], fin_d[3]], axis=1)
    return x, (new_c, new_re, new_im)


def kernel(x_prompt, x_sample, cache_a_ckv, cache_a_krope, cache_b_k, cache_b_v, state_c_ssm, state_d_re, state_d_im, c, c_ctx, w_mod, b_mod, norm1_g, norm2_g, ev_w_in, ev_a_q_lora_g, ev_a_w_uq, ev_a_kv_lora_g, ev_a_w_ukv, ev_a_q_norm_g, ev_a_k_norm_g, ev_b_q_norm_g, ev_b_k_norm_g, ev_w_out, ffn_w1, ffn_w3, ffn_w2, od_w_in, od_conv_w, od_conv_b, od_dt_bias, od_a_log, od_c_d_skip, od_c_norm_g, od_lam_re, od_lam_im, od_log_dt, od_b_re, od_b_im, od_c_re, od_c_im, od_d_skip, od_w_glu, od_w_out, moe_router_w, moe_router_b, moe_w1, moe_w3, moe_w2):
    x = (x_prompt.reshape(TP, D_MODEL), x_sample.reshape(TS, D_MODEL))
    cond8 = jnp.zeros((8, D_MODEL), F32).at[0].set(c_ctx).at[1:1 + DEC_BATCH].set(c)
    mods = _adaln(cond8, w_mod, b_mod)[:, :1 + DEC_BATCH].reshape(DEPTH, 1 + DEC_BATCH, 6, D_MODEL)
    mods = jnp.pad(mods, ((0, 0), (0, 0), (0, 2), (0, 0)))
    rope_tabs = _rope_tables()
    ffn_bf = (ffn_w1.astype(BF16), ffn_w3.astype(BF16), ffn_w2.astype(BF16))
    moe_bf = (moe_w1.astype(BF16), moe_w3.astype(BF16), moe_w2.astype(BF16))
    ev_new, od_new = [], []
    for layer in range(DEPTH):
        mod = mods[layer]
        if layer % 2 == 0:
            e = layer // 2
            ew = _even_weights(ev_w_in[e], ev_a_q_lora_g[e], ev_a_w_uq[e], ev_a_kv_lora_g[e], ev_a_w_ukv[e],
                               ev_a_q_norm_g[e], ev_a_k_norm_g[e], ev_b_q_norm_g[e], ev_b_k_norm_g[e])
            ffn_w = ffn_bf + (e,)
            x, new = _even_layer(x, mod, norm1_g[layer], norm2_g[layer], ew, ev_w_out[e].astype(BF16), ffn_w,
                                 (cache_a_ckv[:, e], cache_a_krope[:, e], cache_b_k[:, e], cache_b_v[:, e]),
                                 rope_tabs)
            ev_new.append(new)
        else:
            o = layer // 2
            ow = _odd_weights(od_w_in[o], od_dt_bias[o], od_a_log[o], od_c_d_skip[o])
            sw = _s5_weights(od_lam_re[o], od_lam_im[o], od_log_dt[o], od_b_re[o], od_b_im[o], od_c_re[o],
                             od_c_im[o], od_d_skip[o])
            rw_hi = _pad_last(moe_router_w[o], LANE).astype(BF16)
            rw_lo = (_pad_last(moe_router_w[o], LANE) - rw_hi.astype(F32)).astype(BF16)
            moe_w = moe_bf + (o, (rw_hi, rw_lo, _pad_last(moe_router_b[o].reshape(1, -1), LANE)))
            x, new = _odd_layer(x, mod, norm1_g[layer], norm2_g[layer], ow, sw, od_conv_w[o], od_conv_b[o],
                                od_c_norm_g[o], od_w_glu[o].astype(BF16), od_w_out[o].astype(BF16), moe_w,
                                (state_c_ssm[:, o], state_d_re[:, o], state_d_im[:, o]), last=layer == DEPTH - 1)
            od_new.append(new)
    stack = lambda items, k: jnp.stack([it[k] for it in items], axis=1)
    y_p, y_s = x
    return (y_p.reshape(BATCH, SEQ, D_MODEL), y_s.reshape(DEC_BATCH, DEC_SEQ, D_MODEL),
            stack(ev_new, 0), stack(ev_new, 1), stack(ev_new, 2), stack(ev_new, 3),
            stack(od_new, 0), stack(od_new, 1), stack(od_new, 2))
```
